```python
import math
import jax
import jax.numpy as jnp
from jax import lax
import numpy as np

D_MODEL = 2048
BATCH = 1
SEQ = 16384
DEPTH = 2
DEC_BATCH = 4
DEC_SEQ = 4096
PAST_LEN = 128

HEAD_DIM = 128
N_HEADS_A = 8
N_HEADS_B = 8
N_HEADS_C = 16
D_A = N_HEADS_A * HEAD_DIM
D_B = N_HEADS_B * HEAD_DIM
D_C = N_HEADS_C * HEAD_DIM
MLSTM_CHUNK = 128
FORGET_BIAS = 3.0
GRID_W = 64
NA_ROWS = 8
NA_COLS = 16
DIL_PAIRS = ((128, 1), (512, 4), (2048, 16))
ROPE_THETA = 500000.0
ROPE_DIM = HEAD_DIM // 4
N_EXPERTS = 32
N_GROUPS = 4
EXPERTS_PER_GROUP = N_EXPERTS // N_GROUPS
TOP_K = 2
D_FF = 1024
MOE_BLOCK = 128
EPS = 1e-6
NEG = -1e30
N_AB_LAYERS = (DEPTH + 1) // 2
N_C_LAYERS = DEPTH // 2
AB_IN = 4 * D_A + 4 * N_HEADS_A + 3 * D_B
AB_SPLITS = (D_A, 2 * D_A, 3 * D_A, 4 * D_A, 4 * D_A + 4 * N_HEADS_A,
             4 * D_A + 4 * N_HEADS_A + D_B, 4 * D_A + 4 * N_HEADS_A + 2 * D_B)

kernel_name = 'hybrid_bidir_mlstm_natten_dilated_moe'


def _rmsnorm(x, g):
    xf = x.astype(jnp.float32)
    y = xf * lax.rsqrt(jnp.mean(xf * xf, axis=-1, keepdims=True) + EPS)
    return (y * g.astype(jnp.float32)).astype(x.dtype)


def _head_rmsnorm(t, g):
    tf = t.astype(jnp.float32)
    return tf * lax.rsqrt(jnp.mean(tf * tf, axis=-1, keepdims=True) + EPS) * g.astype(jnp.float32)


def _rope_partial(t):
    half = ROPE_DIM // 2
    inv_freq = ROPE_THETA ** (-jnp.arange(half, dtype=jnp.float32) / half)
    ang = jnp.arange(t.shape[1], dtype=jnp.float32)[:, None] * inv_freq[None, :]
    cos = jnp.cos(ang)[None, :, None, :]
    sin = jnp.sin(ang)[None, :, None, :]
    t1, t2, rest = t[..., :half], t[..., half:ROPE_DIM], t[..., ROPE_DIM:]
    return jnp.concatenate([t1 * cos - t2 * sin, t1 * sin + t2 * cos, rest], axis=-1)


def _mlstm_direction(q, k, v, i_pre, f_pre):
    bn, h, l, d = q.shape
    ch = MLSTM_CHUNK
    nc = l // ch
    qc = q.reshape(bn, h, nc, ch, d)
    kc = k.reshape(bn, h, nc, ch, d)
    vc = v.reshape(bn, h, nc, ch, d)
    ic = i_pre.reshape(bn, h, nc, ch)
    b = jnp.cumsum(jax.nn.log_sigmoid(f_pre).reshape(bn, h, nc, ch), axis=-1)
    b_tot = b[..., -1]
    causal = jnp.tril(jnp.ones((ch, ch), dtype=bool))
    log_d = jnp.where(causal, b[..., :, None] - b[..., None, :] + ic[..., None, :], NEG)
    m_intra = jnp.max(log_d, axis=-1)
    w_kv = b_tot[..., None] - b + ic
    m_kv = jnp.max(w_kv, axis=-1)

    def step(carry, inp):
        c_st, n_st, m_st = carry
        k_j, v_j, w_j, bt_j, mkv_j = inp
        m_new = jnp.maximum(bt_j + m_st, mkv_j)
        decay = jnp.exp(bt_j + m_st - m_new)
        wexp = jnp.exp(w_j - m_new[..., None])
        c_new = decay[..., None, None] * c_st + jnp.einsum('bhs,bhsk,bhsv->bhkv', wexp, k_j, v_j)
        n_new = decay[..., None] * n_st + jnp.einsum('bhs,bhsk->bhk', wexp, k_j)
        return (c_new, n_new, m_new), (c_st, n_st, m_st)

    init = (jnp.zeros((bn, h, d, d), jnp.float32), jnp.zeros((bn, h, d), jnp.float32),
            jnp.zeros((bn, h), jnp.float32))
    xs = (jnp.moveaxis(kc, 2, 0), jnp.moveaxis(vc, 2, 0), jnp.moveaxis(w_kv, 2, 0),
          jnp.moveaxis(b_tot, 2, 0), jnp.moveaxis(m_kv, 2, 0))
    _, (c_prev, n_prev, m_prev) = lax.scan(step, init, xs)
    c_prev = jnp.moveaxis(c_prev, 0, 2)
    n_prev = jnp.moveaxis(n_prev, 0, 2)
    m_prev = jnp.moveaxis(m_prev, 0, 2)

    log_inter = b + m_prev[..., None]
    m_row = jnp.maximum(log_inter, m_intra)
    p = jnp.einsum('bhcjd,bhcsd->bhcjs', qc, kc) * jnp.exp(log_d - m_row[..., None])
    w_inter = jnp.exp(log_inter - m_row)
    num = (jnp.einsum('bhcjs,bhcsv->bhcjv', p, vc)
           + w_inter[..., None] * jnp.einsum('bhcjk,bhckv->bhcjv', qc, c_prev))
    den = jnp.sum(p, axis=-1) + w_inter * jnp.einsum('bhcjk,bhck->bhcj', qc, n_prev)
    h_out = num / jnp.maximum(jnp.abs(den), jnp.exp(-m_row))[..., None]
    return h_out.reshape(bn, h, l, d)


def _mlstm_mixer(q, k, v, o_pre, gate_pre, gate_bias, hnorm_g):
    to_bhld = lambda t: jnp.transpose(t.astype(jnp.float32), (0, 2, 1, 3))
    qh = to_bhld(q)
    kh = to_bhld(k) * (HEAD_DIM ** -0.5)
    vh = to_bhld(v)
    g = jnp.transpose(gate_pre.astype(jnp.float32) + gate_bias.astype(jnp.float32), (2, 0, 3, 1))
    rev = lambda t: jnp.flip(t, axis=2)
    h_fwd = _mlstm_direction(qh, kh, vh, g[0], g[1])
    h_bwd = rev(_mlstm_direction(rev(qh), rev(kh), rev(vh), rev(g[2]), rev(g[3])))
    h_sum = jnp.transpose(h_fwd + h_bwd, (0, 2, 1, 3))
    return jax.nn.sigmoid(o_pre.astype(jnp.float32)) * _head_rmsnorm(h_sum, hnorm_g)


def _neighbourhood_attention(q, k, v, rel_bias):
    bn, l, h, d = q.shape
    rows = l // GRID_W
    kr = min(NA_ROWS, rows)
    qg = q.reshape(bn, rows, GRID_W, h, d)
    kg = k.reshape(bn, rows, GRID_W, h, d)
    vg = v.reshape(bn, rows, GRID_W, h, d)
    r = jnp.arange(rows)
    row_idx = jnp.clip(r - kr // 2, 0, rows - kr)[:, None] + jnp.arange(kr)[None, :]
    k_blk = kg[:, row_idx]
    v_blk = vg[:, row_idx]
    c = jnp.arange(GRID_W)
    c_start = jnp.clip(c - NA_COLS // 2, 0, GRID_W - NA_COLS)
    col_ok = (c[None, :] >= c_start[:, None]) & (c[None, :] < c_start[:, None] + NA_COLS)
    dr = row_idx - r[:, None]
    dc = jnp.clip(c[None, :] - c[:, None], -(NA_COLS - 1), NA_COLS - 1)
    bias = rel_bias.astype(jnp.float32)[:, dr[:, None, :, None] + NA_ROWS - 1,
                                        dc[None, :, None, :] + NA_COLS - 1]
    s = jnp.einsum('brqhd,brkwhd->bhrqkw', qg, k_blk) * (d ** -0.5) + bias[None]
    s = jnp.where(col_ok[:, None, :], s, NEG)
    p = jax.nn.softmax(s.reshape(bn, h, rows, GRID_W, kr * GRID_W), axis=-1).reshape(s.shape)
    o = jnp.einsum('bhrqkw,brkwhd->brqhd', p, v_blk)
    return o.reshape(bn, l, h, d)


def _dilated_branch(q, k, v, dilation, radius):
    bn, l, h, d = q.shape
    lr = l // dilation
    blk = math.gcd(radius, lr)
    nb = lr // blk
    nk = blk + 2 * radius
    to_res = lambda t: jnp.transpose(t.reshape(bn, lr, dilation, h, d), (0, 2, 1, 3, 4))
    pad = ((0, 0), (0, 0), (radius, radius), (0, 0), (0, 0))
    kp = jnp.pad(to_res(k), pad)
    vp = jnp.pad(to_res(v), pad)
    win = jnp.arange(nb)[:, None] * blk + jnp.arange(nk)[None, :]
    k_win = kp[:, :, win]
    v_win = vp[:, :, win]
    qb = to_res(q).reshape(bn, dilation, nb, blk, h, d)
    key_m = win - radius
    q_m = jnp.arange(lr).reshape(nb, blk)
    rel = key_m[:, None, :] - q_m[:, :, None]
    valid = (jnp.abs(rel) <= radius) & (key_m[:, None, :] >= 0) & (key_m[:, None, :] < lr)
    s = jnp.einsum('bgnqhd,bgnkhd->bgnhqk', qb, k_win) * (d ** -0.5)
    s = jnp.where(valid[:, None], s, NEG)
    mx = jnp.max(s, axis=-1)
    e = jnp.exp(s - mx[..., None])
    den = jnp.sum(e, axis=-1)
    num = jnp.einsum('bgnhqk,bgnkhd->bgnqhd', e, v_win)
    num = jnp.transpose(num.reshape(bn, dilation, lr, h, d), (0, 2, 1, 3, 4)).reshape(bn, l, h, d)
    back = lambda t: jnp.transpose(jnp.transpose(t, (0, 1, 2, 4, 3)).reshape(bn, dilation, lr, h),
                                   (0, 2, 1, 3)).reshape(bn, l, h)
    return num, back(mx), back(den)


def _ab_layer(x, ln_g, w_in, gate_bias, hnorm_g, qnorm_g, knorm_g, rel_bias, w_out):
    bn, l, _ = x.shape
    u = _rmsnorm(x, ln_g) @ w_in
    q_a, k_a, v_a, o_a, g_a, q_b, k_b, v_b = jnp.split(u, AB_SPLITS, axis=-1)
    heads = lambda t, nh: t.reshape(bn, l, nh, HEAD_DIM)
    y_a = _mlstm_mixer(heads(q_a, N_HEADS_A), heads(k_a, N_HEADS_A), heads(v_a, N_HEADS_A),
                       heads(o_a, N_HEADS_A), g_a.reshape(bn, l, 4, N_HEADS_A), gate_bias, hnorm_g)
    y_b = _neighbourhood_attention(_head_rmsnorm(heads(q_b, N_HEADS_B), qnorm_g),
                                   _head_rmsnorm(heads(k_b, N_HEADS_B), knorm_g),
                                   heads(v_b, N_HEADS_B).astype(jnp.float32), rel_bias)
    y = jnp.concatenate([y_a.reshape(bn, l, D_A), y_b.reshape(bn, l, D_B)], axis=-1).astype(x.dtype)
    return x + y @ w_out


def _dilated_layer(x, ln_g, w_in, qnorm_g, knorm_g, w_out):
    bn, l, _ = x.shape
    u = _rmsnorm(x, ln_g) @ w_in
    q, k, v = jnp.split(u, 3, axis=-1)
    q = _rope_partial(_head_rmsnorm(q.reshape(bn, l, N_HEADS_C, HEAD_DIM), qnorm_g))
    k = _rope_partial(_head_rmsnorm(k.reshape(bn, l, N_HEADS_C, HEAD_DIM), knorm_g))
    v = v.reshape(bn, l, N_HEADS_C, HEAD_DIM).astype(jnp.float32)
    nums, maxes, dens = [], [], []
    for window, dilation in DIL_PAIRS:
        n_g, m_g, d_g = _dilated_branch(q, k, v, dilation, window // (2 * dilation))
        nums.append(n_g)
        maxes.append(m_g)
        dens.append(d_g)
    m_all = jnp.stack(maxes)
    m_top = jnp.max(m_all, axis=0)
    scale = jnp.exp(m_all - m_top)
    num = sum(scale[i][..., None] * nums[i] for i in range(len(DIL_PAIRS)))
    den = jnp.sum(scale * jnp.stack(dens), axis=0)
    y = (num / den[..., None]).reshape(bn, l, D_C).astype(x.dtype)
    return x + y @ w_out


def _route(xt, w_router, router_bias):
    t = xt.shape[0]
    scores = jax.nn.sigmoid((xt @ w_router).astype(jnp.float32))
    sel = (scores + router_bias.astype(jnp.float32)).reshape(t, N_GROUPS, EXPERTS_PER_GROUP)
    group_score = jnp.sum(lax.top_k(sel, TOP_K)[0], axis=-1)
    g_best = jnp.argmax(group_score, axis=-1).astype(jnp.int32)
    sel_in = jnp.take_along_axis(sel, g_best[:, None, None], axis=1)[:, 0]
    _, local = lax.top_k(sel_in, TOP_K)
    expert = g_best[:, None] * EXPERTS_PER_GROUP + local.astype(jnp.int32)
    w = jnp.take_along_axis(scores, expert, axis=1)
    return expert, w / jnp.sum(w, axis=-1, keepdims=True)


def _moe(x, w_router, router_bias, w_gate, w_up, w_down):
    bn, l, dm = x.shape
    t = bn * l
    xt = x.reshape(t, dm)
    expert, gate = _route(xt, w_router, router_bias)
    a = t * TOP_K
    e_flat = expert.reshape(a)
    tok_flat = jnp.repeat(jnp.arange(t, dtype=jnp.int32), TOP_K)
    order = jnp.argsort(e_flat)
    e_sorted = e_flat[order]
    tok_sorted = tok_flat[order]
    g_sorted = gate.reshape(a)[order]
    counts = jnp.zeros((N_EXPERTS,), jnp.int32).at[e_flat].add(1)
    starts = jnp.cumsum(counts) - counts
    padded = (counts + MOE_BLOCK - 1) // MOE_BLOCK * MOE_BLOCK
    pad_end = jnp.cumsum(padded)
    pad_start = pad_end - padded
    dest = pad_start[e_sorted] + jnp.arange(a, dtype=jnp.int32) - starts[e_sorted]
    cap = a + N_EXPERTS * MOE_BLOCK
    n_blk = cap // MOE_BLOCK
    slot_tok = jnp.zeros((cap,), jnp.int32).at[dest].set(tok_sorted)
    slot_gate = jnp.zeros((cap,), jnp.float32).at[dest].set(g_sorted)
    blk_expert = jnp.minimum(jnp.searchsorted(pad_end, jnp.arange(n_blk, dtype=jnp.int32) * MOE_BLOCK,
                                              side='right'), N_EXPERTS - 1)

    def expert_block(args):
        toks, e = args
        xb = xt[toks]
        hb = jax.nn.silu(xb @ w_gate[e]) * (xb @ w_up[e])
        return hb @ w_down[e]

    y_slots = lax.map(expert_block, (slot_tok.reshape(n_blk, MOE_BLOCK), blk_expert))
    y = jnp.zeros((t, dm), jnp.float32).at[slot_tok].add(
        slot_gate[:, None] * y_slots.reshape(cap, dm).astype(jnp.float32))
    return y.astype(x.dtype).reshape(bn, l, dm)


def setup_inputs(seed: int = 0) -> dict:
    key = jax.random.key(seed)
    ks = jax.random.split(key, 22)
    nrm = lambda k, shape, s: s * jax.random.normal(k, shape, jnp.float32)
    gain = lambda k, shape: 1.0 + 0.05 * jax.random.normal(k, shape, jnp.float32)
    gate_base = jnp.array([0.0, FORGET_BIAS, 0.0, FORGET_BIAS], jnp.float32)[None, :, None]
    return {
        'x_prompt': nrm(ks[0], (BATCH, SEQ, D_MODEL), 1.0),
        'x_sample': nrm(ks[1], (DEC_BATCH, DEC_SEQ, D_MODEL), 1.0),
        'ab_ln': gain(ks[2], (N_AB_LAYERS, D_MODEL)),
        'ab_w_in': nrm(ks[3], (N_AB_LAYERS, D_MODEL, AB_IN), D_MODEL ** -0.5),
        'ab_gate_bias': gate_base + nrm(ks[4], (N_AB_LAYERS, 4, N_HEADS_A), 0.1),
        'ab_hnorm': gain(ks[5], (N_AB_LAYERS, N_HEADS_A, HEAD_DIM)),
        'ab_qnorm': gain(ks[6], (N_AB_LAYERS, HEAD_DIM)),
        'ab_knorm': gain(ks[7], (N_AB_LAYERS, HEAD_DIM)),
        'ab_relbias': nrm(ks[8], (N_AB_LAYERS, N_HEADS_B, 2 * NA_ROWS - 1, 2 * NA_COLS - 1), 0.1),
        'ab_w_out': nrm(ks[9], (N_AB_LAYERS, D_A + D_B, D_MODEL), 0.5 * (D_A + D_B) ** -0.5),
        'c_ln': gain(ks[10], (N_C_LAYERS, D_MODEL)),
        'c_w_in': nrm(ks[11], (N_C_LAYERS, D_MODEL, 3 * D_C), D_MODEL ** -0.5),
        'c_qnorm': gain(ks[12], (N_C_LAYERS, HEAD_DIM)),
        'c_knorm': gain(ks[13], (N_C_LAYERS, HEAD_DIM)),
        'c_w_out': nrm(ks[14], (N_C_LAYERS, D_C, D_MODEL), 0.5 * D_C ** -0.5),
        'ffn_ln': gain(ks[15], (DEPTH, D_MODEL)),
        'w_router': nrm(ks[16], (D_MODEL, N_EXPERTS), D_MODEL ** -0.5),
        'router_bias': nrm(ks[17], (N_EXPERTS,), 0.01),
        'w_gate': nrm(ks[18], (DEPTH, N_EXPERTS, D_MODEL, D_FF), D_MODEL ** -0.5),
        'w_up': nrm(ks[19], (DEPTH, N_EXPERTS, D_MODEL, D_FF), D_MODEL ** -0.5),
        'w_down': nrm(ks[20], (DEPTH, N_EXPERTS, D_FF, D_MODEL), 0.5 * D_FF ** -0.5),
    }


def reference(x_prompt, x_sample, ab_ln, ab_w_in, ab_gate_bias, ab_hnorm, ab_qnorm, ab_knorm,
              ab_relbias, ab_w_out, c_ln, c_w_in, c_qnorm, c_knorm, c_w_out, ffn_ln, w_router,
              router_bias, w_gate, w_up, w_down):
    def trunk(x):
        for layer in range(DEPTH):
            j = layer // 2
            if layer % 2 == 0:
                x = _ab_layer(x, ab_ln[j], ab_w_in[j], ab_gate_bias[j], ab_hnorm[j], ab_qnorm[j],
                              ab_knorm[j], ab_relbias[j], ab_w_out[j])
            else:
                x = _dilated_layer(x, c_ln[j], c_w_in[j], c_qnorm[j], c_knorm[j], c_w_out[j])
            x = x + _moe(_rmsnorm(x, ffn_ln[layer]), w_router, router_bias,
                         w_gate[layer], w_up[layer], w_down[layer])
        return x

    y_prompt = trunk(x_prompt)
    y_sample = trunk(x_sample)
    return (y_prompt, y_sample)
```

```python
import functools
import math

import jax
import jax.numpy as jnp
from jax import lax
from jax.experimental import pallas as pl
from jax.experimental.pallas import tpu as pltpu

HEAD_DIM = 128
MLSTM_CHUNK = 128
GRID_W = 64
NA_ROWS = 8
NA_COLS = 16
DIL_PAIRS = ((128, 1), (512, 4), (2048, 16))
ROPE_THETA = 500000.0
ROPE_DIM = HEAD_DIM // 4
N_GROUPS = 4
TOP_K = 2
EPS = 1e-6
NEG = -1e30
VMEM_LIMIT_BYTES = 56 * 1024 * 1024

F32 = jnp.float32
BF16 = jnp.bfloat16


def _cparams(*sem):
    return pltpu.CompilerParams(dimension_semantics=sem, vmem_limit_bytes=VMEM_LIMIT_BYTES)


def _dot(a, b):
    return jnp.dot(a, b, preferred_element_type=F32)


def _dot_nt(a, b):
    return lax.dot_general(a, b, (((1,), (1,)), ((), ())), preferred_element_type=F32)


def _dot_tn(a, b):
    return lax.dot_general(a, b, (((0,), (0,)), ((), ())), preferred_element_type=F32)


def _split3(x):
    hi = x.astype(BF16)
    r1 = x - hi.astype(F32)
    mid = r1.astype(BF16)
    lo = (r1 - mid.astype(F32)).astype(BF16)
    return hi, mid, lo


def _seq_lookup(u, table, default):
    out = default
    for thr, val in table:
        out = jnp.where(u >= thr, val, out)
    return out


def _rmsnorm_kernel(x_ref, g_ref, o_ref):
    x = x_ref[...]
    ms = jnp.mean(x * x, axis=-1, keepdims=True)
    o_ref[...] = (x * lax.rsqrt(ms + EPS) * g_ref[...]).astype(o_ref.dtype)


def _rmsnorm(x, g, tm=512):
    t, d = x.shape
    return pl.pallas_call(
        _rmsnorm_kernel,
        grid=(t // tm,),
        in_specs=[pl.BlockSpec((tm, d), lambda i: (i, 0)), pl.BlockSpec((1, d), lambda i: (0, 0))],
        out_specs=pl.BlockSpec((tm, d), lambda i: (i, 0)),
        out_shape=jax.ShapeDtypeStruct((t, d), BF16),
        compiler_params=_cparams("parallel"),
        name="rmsnorm",
    )(x, g.reshape(1, d).astype(F32))


def _head_epilogue(acc, gain, rope):
    outs = []
    for h in range(acc.shape[1] // HEAD_DIM):
        a = acc[:, h * HEAD_DIM:(h + 1) * HEAD_DIM]
        y = a * lax.rsqrt(jnp.mean(a * a, axis=-1, keepdims=True) + EPS) * gain[:, h * HEAD_DIM:(h + 1) * HEAD_DIM]
        if rope is not None:
            c, s1, s2 = rope
            half = ROPE_DIM // 2
            y = y * c + pltpu.roll(y, half, 1) * s1 + pltpu.roll(y, HEAD_DIM - half, 1) * s2
        outs.append(y)
    return jnp.concatenate(outs, axis=1) if len(outs) > 1 else outs[0]


def _mm_kernel(*refs, n_in, epi):
    xs, ws, rest = refs[:n_in], refs[n_in:2 * n_in], refs[2 * n_in:]
    acc = _dot(xs[0][...], ws[0][...])
    for x_ref, w_ref in zip(xs[1:], ws[1:]):
        acc = acc + _dot(x_ref[...], w_ref[...])
    o_ref = rest[-1]
    if epi == "plain":
        out = acc
    elif epi == "residual":
        out = acc + rest[0][...]
    elif epi == "headnorm":
        out = _head_epilogue(acc, rest[0][...], None)
    elif epi == "headnorm_rope":
        out = _head_epilogue(acc, rest[0][...], (rest[1][...], rest[2][...], rest[3][...]))
    else:
        raise ValueError(epi)
    o_ref[...] = out.astype(o_ref.dtype)


def _matmul(xs, ws, out_dtype, epi="plain", extra=(), pos_blk=None, tm=1024, tn=512):
    t = xs[0].shape[0]
    n = ws[0].shape[1]
    tm, tn = min(tm, t), min(tn, n)
    in_specs = [pl.BlockSpec((tm, x.shape[1]), lambda i, j: (i, 0)) for x in xs]
    in_specs += [pl.BlockSpec((w.shape[0], tn), lambda i, j: (0, j)) for w in ws]
    if epi == "residual":
        in_specs.append(pl.BlockSpec((tm, tn), lambda i, j: (i, j)))
    elif epi in ("headnorm", "headnorm_rope"):
        in_specs.append(pl.BlockSpec((1, tn), lambda i, j: (0, j)))
        if epi == "headnorm_rope":
            in_specs += [pl.BlockSpec((tm, HEAD_DIM), lambda i, j: (pos_blk(i), 0))] * 3
    return pl.pallas_call(
        functools.partial(_mm_kernel, n_in=len(xs), epi=epi),
        grid=(t // tm, n // tn),
        in_specs=in_specs,
        out_specs=pl.BlockSpec((tm, tn), lambda i, j: (i, j)),
        out_shape=jax.ShapeDtypeStruct((t, n), out_dtype),
        compiler_params=_cparams("parallel", "parallel"),
        name="matmul_" + epi,
    )(*xs, *ws, *extra)


def _mlstm_kernel(qf_ref, kf_ref, vf_ref, gf_ref, qb_ref, kb_ref, vb_ref, gb_ref, bias_ref,
                  hf_ref, hb_ref, c_ref, n_ref, m_ref, *, nh, nchunks, start_chunks, last_chunks):
    c = pl.program_id(0)
    cb = nchunks - 1 - c
    ch = MLSTM_CHUNK
    is_start = functools.reduce(jnp.logical_or, [c == s for s in start_chunks])
    is_last = functools.reduce(jnp.logical_or, [cb == e for e in last_chunks])

    def _reset(lo):
        c_ref[lo:lo + nh] = jnp.zeros((nh, HEAD_DIM, HEAD_DIM), F32)
        n_ref[lo:lo + nh] = jnp.zeros((nh, 1, HEAD_DIM), F32)
        m_ref[lo:lo + nh] = jnp.zeros((nh, 1, HEAD_DIM), F32)

    pl.when(is_start)(lambda: _reset(0))
    pl.when(is_last)(lambda: _reset(nh))

    row = lax.broadcasted_iota(jnp.int32, (ch, ch), 0)
    col = lax.broadcasted_iota(jnp.int32, (ch, ch), 1)
    scale = HEAD_DIM ** -0.5

    for d, (q_ref, k_ref, v_ref, g_ref, h_ref) in enumerate(
            ((qf_ref, kf_ref, vf_ref, gf_ref, hf_ref), (qb_ref, kb_ref, vb_ref, gb_ref, hb_ref))):
        mask = (row >= col) if d == 0 else (col >= row)
        icol0 = 2 * nh * d
        fcol0 = icol0 + nh
        tot_row = ch - 1 if d == 0 else 0
        g = g_ref[...] + bias_ref[...]
        lf = jnp.minimum(g, 0.0) - jnp.log1p(jnp.exp(-jnp.abs(g)))
        tri = jnp.where(mask, 1.0, 0.0).astype(BF16)
        hi, mid, lo = _split3(lf)
        bcum = _dot(tri, hi) + _dot(tri, mid) + _dot(tri, lo)
        a = g - pltpu.roll(bcum, HEAD_DIM - nh, 1)
        a_t = a.T
        for h in range(nh):
            idx = d * nh + h
            hs = slice(h * HEAD_DIM, (h + 1) * HEAD_DIM)
            bcol = bcum[:, fcol0 + h:fcol0 + h + 1]
            acol = a[:, icol0 + h:icol0 + h + 1]
            arow = a_t[icol0 + h:icol0 + h + 1, :]
            b_tot = bcum[tot_row:tot_row + 1, fcol0 + h:fcol0 + h + 1]
            m_st = m_ref[idx][:, 0:1]
            c_prev = c_ref[idx]
            n_prev = n_ref[idx]

            log_d = jnp.where(mask, bcol + arow, NEG)
            m_intra = jnp.max(log_d, axis=1, keepdims=True)
            log_inter = bcol + m_st
            m_row = jnp.maximum(log_inter, m_intra)
            dmat = jnp.exp(log_d - m_row)
            q = q_ref[:, hs]
            ks = (k_ref[:, hs].astype(F32) * scale).astype(BF16)
            v = v_ref[:, hs]
            p = _dot_nt(q, ks) * dmat
            w_inter = jnp.exp(log_inter - m_row)
            num = _dot(p.astype(BF16), v) + w_inter * _dot(q, c_prev.astype(BF16))
            den = (jnp.sum(p, axis=1, keepdims=True)
                   + w_inter * jnp.sum(q.astype(F32) * n_prev, axis=1, keepdims=True))
            h_ref[:, hs] = num / jnp.maximum(jnp.abs(den), jnp.exp(-m_row))

            m_kv = b_tot + jnp.max(arow, axis=1, keepdims=True)
            m_new = jnp.maximum(b_tot + m_st, m_kv)
            decay = jnp.exp(b_tot + m_st - m_new)
            wexp = jnp.exp(b_tot + acol - m_new)
            kw = ks.astype(F32) * wexp
            c_ref[idx] = decay * c_prev + _dot_tn(kw.astype(BF16), v)
            n_ref[idx] = decay * n_prev + jnp.sum(kw, axis=0, keepdims=True)
            m_ref[idx] = jnp.broadcast_to(m_new, (1, HEAD_DIM))


def _mlstm(u_a, gates, gate_bias, seqs, nh):
    t = u_a.shape[0]
    ch = MLSTM_CHUNK
    nchunks = t // ch
    da = nh * HEAD_DIM
    start_chunks = [s // ch for s, _ in seqs]
    last_chunks = [(s + l) // ch - 1 for s, l in seqs]
    fwd = lambda j: pl.BlockSpec((ch, da), lambda c, j=j: (c, j))
    bwd = lambda j: pl.BlockSpec((ch, da), lambda c, j=j: (nchunks - 1 - c, j))
    gspec_f = pl.BlockSpec((ch, 128), lambda c: (c, 0))
    gspec_b = pl.BlockSpec((ch, 128), lambda c: (nchunks - 1 - c, 0))
    bias = jnp.zeros((1, 128), F32).at[0, :4 * nh].set(gate_bias.reshape(-1).astype(F32))
    return pl.pallas_call(
        functools.partial(_mlstm_kernel, nh=nh, nchunks=nchunks, start_chunks=start_chunks,
                          last_chunks=last_chunks),
        grid=(nchunks,),
        in_specs=[fwd(0), fwd(1), fwd(2), gspec_f, bwd(0), bwd(1), bwd(2), gspec_b,
                  pl.BlockSpec((1, 128), lambda c: (0, 0))],
        out_specs=[pl.BlockSpec((ch, da), lambda c: (c, 0)),
                   pl.BlockSpec((ch, da), lambda c: (nchunks - 1 - c, 0))],
        out_shape=[jax.ShapeDtypeStruct((t, da), F32)] * 2,
        scratch_shapes=[pltpu.VMEM((2 * nh, HEAD_DIM, HEAD_DIM), F32),
                        pltpu.VMEM((2 * nh, 1, HEAD_DIM), F32),
                        pltpu.VMEM((2 * nh, 1, HEAD_DIM), F32)],
        compiler_params=_cparams("arbitrary"),
        name="mlstm",
    )(u_a, u_a, u_a, gates, u_a, u_a, u_a, gates, bias)


def _mlstm_gate_kernel(hf_ref, hb_ref, o_ref, g_ref, y_ref):
    hsum = hf_ref[...] + hb_ref[...]
    o = o_ref[...].astype(F32)
    g = g_ref[...]
    outs = []
    for h in range(hsum.shape[1] // HEAD_DIM):
        hs = slice(h * HEAD_DIM, (h + 1) * HEAD_DIM)
        a = hsum[:, hs]
        y = a * lax.rsqrt(jnp.mean(a * a, axis=-1, keepdims=True) + EPS) * g[:, hs]
        outs.append(jax.nn.sigmoid(o[:, hs]) * y)
    y_ref[...] = jnp.concatenate(outs, axis=1).astype(y_ref.dtype)


def _mlstm_gate(h_f, h_b, u_a, hnorm_g, tm=512):
    t, da = h_f.shape
    return pl.pallas_call(
        _mlstm_gate_kernel,
        grid=(t // tm,),
        in_specs=[pl.BlockSpec((tm, da), lambda i: (i, 0)), pl.BlockSpec((tm, da), lambda i: (i, 0)),
                  pl.BlockSpec((tm, da), lambda i: (i, 3)), pl.BlockSpec((1, da), lambda i: (0, 0))],
        out_specs=pl.BlockSpec((tm, da), lambda i: (i, 0)),
        out_shape=jax.ShapeDtypeStruct((t, da), BF16),
        compiler_params=_cparams("parallel"),
        name="mlstm_gate",
    )(h_f, h_b, u_a, hnorm_g.reshape(1, da).astype(F32))


def _na_bias_table(rel_bias):
    h = rel_bias.shape[0]
    c = jnp.arange(GRID_W)
    dc = jnp.clip(c[None, :] - c[:, None], -(NA_COLS - 1), NA_COLS - 1) + NA_COLS - 1
    c_start = jnp.clip(c - NA_COLS // 2, 0, GRID_W - NA_COLS)
    col_ok = (c[None, :] >= c_start[:, None]) & (c[None, :] < c_start[:, None] + NA_COLS)
    dr = jnp.arange(NA_ROWS)[None, :] - jnp.arange(NA_ROWS)[:, None] + NA_ROWS - 1
    tbl = rel_bias.astype(F32)[:, dr[:, :, None, None], dc[None, None, :, :]]
    tbl = jnp.where(col_ok[None, None, None], tbl, NEG)
    return jnp.transpose(tbl, (0, 1, 3, 2, 4)).reshape(h, NA_ROWS, GRID_W, NA_ROWS * GRID_W)


def _na_kernel(q_ref, kp_ref, km_ref, kn_ref, vp_ref, vm_ref, vn_ref, bias_ref, o_ref, kcat, vcat,
               *, seg_rows, row_lo, row_hi):
    s = pl.program_id(1)
    halo = NA_ROWS * GRID_W
    seg = seg_rows * GRID_W
    kcat[0:halo] = kp_ref[...]
    kcat[halo:halo + seg] = km_ref[...]
    kcat[halo + seg:halo + seg + halo] = kn_ref[...]
    vcat[0:halo] = vp_ref[...]
    vcat[halo:halo + seg] = vm_ref[...]
    vcat[halo + seg:halo + seg + halo] = vn_ref[...]
    r_lo = _seq_lookup(s, row_lo, 0)
    r_hi = _seq_lookup(s, row_hi, 0)
    scale = HEAD_DIM ** -0.5
    nk = NA_ROWS * GRID_W

    def body(rl, carry):
        r = s * seg_rows + rl
        ks = jnp.clip(r - NA_ROWS // 2, r_lo, r_hi - NA_ROWS)
        off = pl.multiple_of((ks - s * seg_rows + NA_ROWS) * GRID_W, GRID_W)
        qoff = pl.multiple_of(rl * GRID_W, GRID_W)
        q = q_ref[pl.ds(qoff, GRID_W), :]
        sc = _dot_nt(q, kcat[pl.ds(off, nk), :]) * scale + bias_ref[r - ks]
        e = jnp.exp(sc - jnp.max(sc, axis=1, keepdims=True))
        l = jnp.sum(e, axis=1, keepdims=True)
        o_ref[pl.ds(qoff, GRID_W), :] = (_dot(e.astype(BF16), vcat[pl.ds(off, nk), :]) / l).astype(o_ref.dtype)
        return carry

    lax.fori_loop(0, seg_rows, body, 0)


def _neighbourhood_attention(q, k, v, bias_tbl, seqs, seg):
    t, dh = q.shape
    nh = dh // HEAD_DIM
    nseg = t // seg
    seg_rows = seg // GRID_W
    halo = NA_ROWS * GRID_W
    hpb = seg // halo
    nhb = t // halo
    row_lo = [(s // seg, s // GRID_W) for s, _ in seqs]
    row_hi = [(s // seg, (s + l) // GRID_W) for s, l in seqs]
    main = pl.BlockSpec((seg, HEAD_DIM), lambda h, s: (s, h))
    prev = pl.BlockSpec((halo, HEAD_DIM), lambda h, s: (jnp.maximum(s * hpb - 1, 0), h))
    nxt = pl.BlockSpec((halo, HEAD_DIM), lambda h, s: (jnp.minimum((s + 1) * hpb, nhb - 1), h))
    return pl.pallas_call(
        functools.partial(_na_kernel, seg_rows=seg_rows, row_lo=row_lo, row_hi=row_hi),
        grid=(nh, nseg),
        in_specs=[main, prev, main, nxt, prev, main, nxt,
                  pl.BlockSpec((None, NA_ROWS, GRID_W, NA_ROWS * GRID_W), lambda h, s: (h, 0, 0, 0))],
        out_specs=main,
        out_shape=jax.ShapeDtypeStruct((t, dh), BF16),
        scratch_shapes=[pltpu.VMEM((seg + 2 * halo, HEAD_DIM), BF16)] * 2,
        compiler_params=_cparams("parallel", "parallel"),
        name="natten",
    )(q, k, k, k, v, v, v, bias_tbl)


DIL_BLK = 128
DIL_KBLK = 64


def _dilated_kernel(q_ref, k0_ref, k1_ref, k2_ref, k3_ref, v0_ref, v1_ref, v2_ref, v3_ref,
                    num_ref, mx_ref, den_ref, *, nh, radius, m_lo, m_hi):
    mb = pl.program_id(1)
    nk = DIL_BLK + 2 * DIL_KBLK
    j = lax.broadcasted_iota(jnp.int32, (DIL_BLK, nk), 0)
    s = lax.broadcasted_iota(jnp.int32, (DIL_BLK, nk), 1)
    kpos = mb * DIL_BLK - DIL_KBLK + s
    lo = _seq_lookup(mb, m_lo, 0)
    hi = _seq_lookup(mb, m_hi, 0)
    valid = (jnp.abs(s - DIL_KBLK - j) <= radius) & (kpos >= lo) & (kpos < hi)
    madd = jnp.where(valid, 0.0, NEG)
    lane = lax.broadcasted_iota(jnp.int32, (DIL_BLK, 128), 1)
    mx_all = jnp.zeros((DIL_BLK, 128), F32)
    den_all = jnp.zeros((DIL_BLK, 128), F32)
    scale = HEAD_DIM ** -0.5
    for h in range(nh):
        hs = slice(h * HEAD_DIM, (h + 1) * HEAD_DIM)
        kk = jnp.concatenate([k0_ref[:, hs], k1_ref[:, hs], k2_ref[:, hs], k3_ref[:, hs]], axis=0)
        vv = jnp.concatenate([v0_ref[:, hs], v1_ref[:, hs], v2_ref[:, hs], v3_ref[:, hs]], axis=0)
        sc = _dot_nt(q_ref[:, hs], kk) * scale + madd
        mx = jnp.max(sc, axis=1, keepdims=True)
        e = jnp.exp(sc - mx)
        den = jnp.sum(e, axis=1, keepdims=True)
        num_ref[:, hs] = _dot(e.astype(BF16), vv)
        mx_all = jnp.where(lane == h, mx, mx_all)
        den_all = jnp.where(lane == h, den, den_all)
    mx_ref[...] = mx_all
    den_ref[...] = den_all


def _dilated_branch(q, k, v, dilation, radius, seqs):
    t, dh = q.shape
    nh = dh // HEAD_DIM
    assert radius == DIL_KBLK
    m = t // dilation
    nmb = m // DIL_BLK
    nkb = m // DIL_KBLK
    r = DIL_BLK // DIL_KBLK
    qv, kv, vv = (a.reshape(m, dilation * dh) for a in (q, k, v))
    m_lo = [(s // dilation // DIL_BLK, s // dilation) for s, _ in seqs]
    m_hi = [(s // dilation // DIL_BLK, (s + l) // dilation) for s, l in seqs]
    qspec = pl.BlockSpec((DIL_BLK, dh), lambda g, mb: (mb, g))
    kspec = lambda o: pl.BlockSpec((DIL_KBLK, dh), lambda g, mb, o=o: (jnp.clip(r * mb + o, 0, nkb - 1), g))
    sspec = pl.BlockSpec((DIL_BLK, 128), lambda g, mb: (mb, g))
    num, mx, den = pl.pallas_call(
        functools.partial(_dilated_kernel, nh=nh, radius=radius, m_lo=m_lo, m_hi=m_hi),
        grid=(dilation, nmb),
        in_specs=[qspec] + [kspec(o) for o in (-1, 0, 1, 2)] * 2,
        out_specs=[qspec, sspec, sspec],
        out_shape=[jax.ShapeDtypeStruct((m, dilation * dh), F32),
                   jax.ShapeDtypeStruct((m, dilation * 128), F32),
                   jax.ShapeDtypeStruct((m, dilation * 128), F32)],
        compiler_params=_cparams("parallel", "parallel"),
        name=f"dilated_{dilation}",
    )(qv, kv, kv, kv, kv, vv, vv, vv, vv)
    return num.reshape(t, dh), mx.reshape(t, 128), den.reshape(t, 128)


def _dilated_merge_kernel(*refs, nb, nh):
    nums, mxs, dens, y_ref = refs[:nb], refs[nb:2 * nb], refs[2 * nb:3 * nb], refs[3 * nb]
    mx = [r[...] for r in mxs]
    m_top = functools.reduce(jnp.maximum, mx)
    sc = [jnp.exp(m - m_top) for m in mx]
    den = functools.reduce(jnp.add, [s * d[...] for s, d in zip(sc, dens)])
    outs = []
    for h in range(nh):
        hs = slice(h * HEAD_DIM, (h + 1) * HEAD_DIM)
        num = functools.reduce(jnp.add, [s[:, h:h + 1] * n[:, hs] for s, n in zip(sc, nums)])
        outs.append(num / den[:, h:h + 1])
    y_ref[...] = jnp.concatenate(outs, axis=1).astype(y_ref.dtype)


def _dilated_merge(nums, mxs, dens, tm=256):
    t, dh = nums[0].shape
    nb = len(nums)
    big = pl.BlockSpec((tm, dh), lambda i: (i, 0))
    small = pl.BlockSpec((tm, 128), lambda i: (i, 0))
    return pl.pallas_call(
        functools.partial(_dilated_merge_kernel, nb=nb, nh=dh // HEAD_DIM),
        grid=(t // tm,),
        in_specs=[big] * nb + [small] * (2 * nb),
        out_specs=big,
        out_shape=jax.ShapeDtypeStruct((t, dh), BF16),
        compiler_params=_cparams("parallel"),
        name="dilated_merge",
    )(*nums, *mxs, *dens)


def _rope_tables(max_len):
    half = ROPE_DIM // 2
    inv_freq = ROPE_THETA ** (-jnp.arange(half, dtype=F32) / half)
    ang = jnp.arange(max_len, dtype=F32)[:, None] * inv_freq[None, :]
    cos, sin = jnp.cos(ang), jnp.sin(ang)
    ones = jnp.ones((max_len, HEAD_DIM - ROPE_DIM), F32)
    zeros = jnp.zeros((max_len, HEAD_DIM - half), F32)
    c = jnp.concatenate([cos, cos, ones], axis=1)
    s1 = jnp.concatenate([jnp.zeros((max_len, half), F32), sin, zeros[:, :HEAD_DIM - ROPE_DIM]], axis=1)
    s2 = jnp.concatenate([-sin, zeros], axis=1)
    return c, s1, s2


def _router_kernel(x_ref, g_ref, wr_ref, rb_ref, xn_ref, e_ref, gt_ref, *, n_exp):
    x = x_ref[...]
    xn = x * lax.rsqrt(jnp.mean(x * x, axis=-1, keepdims=True) + EPS) * g_ref[...]
    xn_ref[...] = xn.astype(xn_ref.dtype)
    tm = x.shape[0]
    epg = n_exp // N_GROUPS
    xh, xm, _ = _split3(xn)
    wh, wm, _ = _split3(wr_ref[...])
    logits = _dot_nt(wh, xh) + _dot_nt(wh, xm) + _dot_nt(wm, xh)
    scores = jax.nn.sigmoid(logits)
    sel = (scores + rb_ref[:, 0:1]).reshape(N_GROUPS, epg, tm)
    sc3 = scores.reshape(N_GROUPS, epg, tm)
    io = lax.broadcasted_iota(jnp.int32, (N_GROUPS, epg, tm), 1)
    gio = lax.broadcasted_iota(jnp.int32, (N_GROUPS, 1, tm), 0)
    m1 = jnp.max(sel, axis=1, keepdims=True)
    i1 = jnp.min(jnp.where(sel == m1, io, epg), axis=1, keepdims=True)
    sel_b = jnp.where(io == i1, -jnp.inf, sel)
    m2 = jnp.max(sel_b, axis=1, keepdims=True)
    i2 = jnp.min(jnp.where(sel_b == m2, io, epg), axis=1, keepdims=True)
    gs = m1 + m2
    gbest = jnp.min(jnp.where(gs == jnp.max(gs, axis=0, keepdims=True), gio, N_GROUPS), axis=0, keepdims=True)
    in_g = gio == gbest
    outs_e, outs_w = [], []
    for ik in (i1, i2):
        pick = in_g & (io == ik)
        outs_e.append(jnp.sum(jnp.where(in_g, gbest * epg + ik, 0), axis=0))
        outs_w.append(jnp.sum(jnp.sum(jnp.where(pick, sc3, 0.0), axis=1, keepdims=True), axis=0))
    wsum = outs_w[0] + outs_w[1]
    e_ref[0:1, :] = outs_e[0]
    e_ref[1:2, :] = outs_e[1]
    gt_ref[0:1, :] = outs_w[0] / wsum
    gt_ref[1:2, :] = outs_w[1] / wsum


def _router(x, g, w_router, router_bias, tm=512):
    t, d = x.shape
    n_exp = w_router.shape[1]
    rb = jnp.broadcast_to(router_bias.astype(F32)[:, None], (n_exp, 128))
    return pl.pallas_call(
        functools.partial(_router_kernel, n_exp=n_exp),
        grid=(t // tm,),
        in_specs=[pl.BlockSpec((tm, d), lambda i: (i, 0)), pl.BlockSpec((1, d), lambda i: (0, 0)),
                  pl.BlockSpec((n_exp, d), lambda i: (0, 0)), pl.BlockSpec((n_exp, 128), lambda i: (0, 0))],
        out_specs=[pl.BlockSpec((tm, d), lambda i: (i, 0)), pl.BlockSpec((2, tm), lambda i: (0, i)),
                   pl.BlockSpec((2, tm), lambda i: (0, i))],
        out_shape=[jax.ShapeDtypeStruct((t, d), BF16), jax.ShapeDtypeStruct((2, t), jnp.int32),
                   jax.ShapeDtypeStruct((2, t), F32)],
        compiler_params=_cparams("parallel"),
        name="router",
    )(x, g.reshape(1, d).astype(F32), w_router.T.astype(F32), rb)


def _rank_kernel(e_ref, rank_ref, cnt_ref, run_ref, *, n_exp):
    i = pl.program_id(0)
    tm = e_ref.shape[1]

    @pl.when(i == 0)
    def _():
        run_ref[...] = jnp.zeros_like(run_ref)

    eio = lax.broadcasted_iota(jnp.int32, (n_exp, tm), 0)
    oh = [(eio == e_ref[k:k + 1, :]) for k in range(TOP_K)]
    ohf = [jnp.where(o, 1.0, 0.0) for o in oh]
    both = ohf[0] + ohf[1]
    r_i = lax.broadcasted_iota(jnp.int32, (tm, tm), 0)
    c_i = lax.broadcasted_iota(jnp.int32, (tm, tm), 1)
    upper = jnp.where(r_i <= c_i, 1.0, 0.0).astype(BF16)
    cum = _dot(both.astype(BF16), upper)
    base = run_ref[:, 0:1] + cum - both
    for k in range(TOP_K):
        rank_ref[k:k + 1, :] = jnp.sum(jnp.where(oh[k], base, 0.0), axis=0, keepdims=True).astype(jnp.int32)
    run_new = run_ref[...] + cum[:, tm - 1:tm]
    run_ref[...] = run_new
    cnt_ref[...] = run_new.astype(jnp.int32)


def _rank(e, n_exp, tm=512):
    t = e.shape[1]
    return pl.pallas_call(
        functools.partial(_rank_kernel, n_exp=n_exp),
        grid=(t // tm,),
        in_specs=[pl.BlockSpec((2, tm), lambda i: (0, i))],
        out_specs=[pl.BlockSpec((2, tm), lambda i: (0, i)), pl.BlockSpec((n_exp, 128), lambda i: (0, 0))],
        out_shape=[jax.ShapeDtypeStruct((2, t), jnp.int32), jax.ShapeDtypeStruct((n_exp, 128), jnp.int32)],
        scratch_shapes=[pltpu.VMEM((n_exp, 128), F32)],
        compiler_params=_cparams("arbitrary"),
        name="moe_rank",
    )(e)


MOE_BLK = 256


def _ffn_kernel(be_ref, nb_ref, x_ref, wg_ref, wu_ref, wd_ref, y_ref):
    b = pl.program_id(0)

    @pl.when(b < nb_ref[0])
    def _():
        x = x_ref[...]
        hg = _dot(x, wg_ref[...])
        hu = _dot(x, wu_ref[...])
        hb = (hg * jax.nn.sigmoid(hg)) * hu
        y_ref[...] = _dot(hb.astype(BF16), wd_ref[...]).astype(y_ref.dtype)

    @pl.when(b >= nb_ref[0])
    def _():
        y_ref[...] = jnp.zeros_like(y_ref)


def _expert_ffn(xs, blk_expert, n_used, w_gate, w_up, w_down):
    cap, d = xs.shape
    f = w_gate.shape[2]
    nblk = cap // MOE_BLK
    grid_spec = pltpu.PrefetchScalarGridSpec(
        num_scalar_prefetch=2,
        grid=(nblk,),
        in_specs=[pl.BlockSpec((MOE_BLK, d), lambda b, be, nb: (b, 0)),
                  pl.BlockSpec((None, d, f), lambda b, be, nb: (be[b], 0, 0)),
                  pl.BlockSpec((None, d, f), lambda b, be, nb: (be[b], 0, 0)),
                  pl.BlockSpec((None, f, d), lambda b, be, nb: (be[b], 0, 0))],
        out_specs=pl.BlockSpec((MOE_BLK, d), lambda b, be, nb: (b, 0)),
    )
    return pl.pallas_call(
        _ffn_kernel,
        grid_spec=grid_spec,
        out_shape=jax.ShapeDtypeStruct((cap, d), F32),
        compiler_params=_cparams("arbitrary"),
        name="moe_ffn",
    )(blk_expert, n_used, xs, w_gate, w_up, w_down)


def _moe(x, ln_g, w_router, router_bias, w_gate, w_up, w_down):
    t, d = x.shape
    n_exp = w_router.shape[1]
    xn, e, gt = _router(x, ln_g, w_router, router_bias)
    rank, cnt = _rank(e, n_exp)
    counts = cnt[:, 0]
    padded = (counts + MOE_BLK - 1) // MOE_BLK * MOE_BLK
    pad_end = jnp.cumsum(padded)
    pad_start = pad_end - padded
    cap = t * TOP_K + n_exp * MOE_BLK
    nblk = cap // MOE_BLK
    slot = pad_start[e] + rank
    blk_expert = jnp.minimum(jnp.searchsorted(pad_end, jnp.arange(nblk, dtype=jnp.int32) * MOE_BLK, side="right"),
                             n_exp - 1).astype(jnp.int32)
    n_used = (pad_end[-1] // MOE_BLK).astype(jnp.int32).reshape(1)
    tok = jnp.broadcast_to(jnp.arange(t, dtype=jnp.int32)[None, :], (TOP_K, t))
    slot_tok = jnp.zeros((cap,), jnp.int32).at[slot.reshape(-1)].set(tok.reshape(-1))
    xs = xn[slot_tok]
    ys = _expert_ffn(xs, blk_expert, n_used, w_gate, w_up, w_down)
    y = gt[0][:, None] * ys[slot[0]] + gt[1][:, None] * ys[slot[1]]
    return x + y


def _ab_layer(x, seqs, seg, ln_g, w_in, gate_bias, hnorm_g, qnorm_g, knorm_g, rel_bias, w_out):
    nh_a = gate_bias.shape[1]
    nh_b = rel_bias.shape[0]
    d_a, d_b = nh_a * HEAD_DIM, nh_b * HEAD_DIM
    xn = _rmsnorm(x, ln_g)
    w = w_in.astype(BF16)
    g0 = 4 * d_a
    g1 = g0 + 4 * nh_a
    u_a = _matmul([xn], [w[:, :g0]], BF16)
    w_g = jnp.zeros((w.shape[0], 128), BF16).at[:, :4 * nh_a].set(w[:, g0:g1])
    gates = _matmul([xn], [w_g], F32)
    gain = jnp.concatenate([jnp.tile(qnorm_g, nh_b), jnp.tile(knorm_g, nh_b)]).reshape(1, -1).astype(F32)
    qk_b = _matmul([xn], [w[:, g1:g1 + 2 * d_b]], BF16, epi="headnorm", extra=(gain,), tn=1024)
    v_b = _matmul([xn], [w[:, g1 + 2 * d_b:]], BF16)
    h_f, h_b = _mlstm(u_a, gates, gate_bias, seqs, nh_a)
    y_a = _mlstm_gate(h_f, h_b, u_a, hnorm_g)
    y_b = _neighbourhood_attention(qk_b[:, :d_b], qk_b[:, d_b:], v_b, _na_bias_table(rel_bias), seqs, seg)
    wo = w_out.astype(BF16)
    return _matmul([y_a, y_b], [wo[:d_a], wo[d_a:]], F32, epi="residual", extra=(x,))


def _dilated_layer(x, seqs, ln_g, w_in, qnorm_g, knorm_g, w_out):
    t = x.shape[0]
    d_c = w_in.shape[1] // 3
    nh = d_c // HEAD_DIM
    xn = _rmsnorm(x, ln_g)
    w = w_in.astype(BF16)
    tm = 1024
    max_len = max(l for _, l in seqs)
    tables = _rope_tables(max_len)
    pos_tbl = [(s // tm, s // tm) for s, _ in seqs]
    pos_blk = lambda i: i - _seq_lookup(i, pos_tbl, 0)
    gain = jnp.concatenate([jnp.tile(qnorm_g, nh), jnp.tile(knorm_g, nh)]).reshape(1, -1).astype(F32)
    qk = _matmul([xn], [w[:, :2 * d_c]], BF16, epi="headnorm_rope", extra=(gain,) + tables,
                 pos_blk=pos_blk, tm=tm, tn=1024)
    v = _matmul([xn], [w[:, 2 * d_c:]], BF16, tm=tm)
    q, k = qk[:, :d_c], qk[:, d_c:]
    parts = [_dilated_branch(q, k, v, dil, window // (2 * dil), seqs) for window, dil in DIL_PAIRS]
    y = _dilated_merge([p[0] for p in parts], [p[1] for p in parts], [p[2] for p in parts])
    return _matmul([y], [w_out.astype(BF16)], F32, epi="residual", extra=(x,))


def kernel(x_prompt, x_sample, ab_ln, ab_w_in, ab_gate_bias, ab_hnorm, ab_qnorm, ab_knorm, ab_relbias, ab_w_out,
           c_ln, c_w_in, c_qnorm, c_knorm, c_w_out, ffn_ln, w_router, router_bias, w_gate, w_up, w_down):
    bp, lp, d = x_prompt.shape
    bs, ls, _ = x_sample.shape
    tp = bp * lp
    seqs = [(b * lp, lp) for b in range(bp)] + [(tp + b * ls, ls) for b in range(bs)]
    seg = math.gcd(lp, ls)
    x = jnp.concatenate([x_prompt.reshape(tp, d), x_sample.reshape(bs * ls, d)], axis=0)
    depth = ffn_ln.shape[0]
    wg, wu, wd = w_gate.astype(BF16), w_up.astype(BF16), w_down.astype(BF16)
    for layer in range(depth):
        j = layer // 2
        if layer % 2 == 0:
            x = _ab_layer(x, seqs, seg, ab_ln[j], ab_w_in[j], ab_gate_bias[j], ab_hnorm[j].reshape(-1),
                          ab_qnorm[j], ab_knorm[j], ab_relbias[j], ab_w_out[j])
        else:
            x = _dilated_layer(x, seqs, c_ln[j], c_w_in[j], c_qnorm[j], c_knorm[j], c_w_out[j])
        x = _moe(x, ffn_ln[layer], w_router, router_bias, wg[layer], wu[layer], wd[layer])
    return x[:tp].reshape(bp, lp, d), x[tp:].reshape(bs, ls, d)
```

```python
import functools
import math

import jax
import jax.numpy as jnp
from jax import lax
from jax.experimental import pallas as pl
from jax.experimental.pallas import tpu as pltpu

HEAD_DIM = 128
MLSTM_CHUNK = 128
GRID_W = 64
NA_ROWS = 8
NA_COLS = 16
DIL_PAIRS = ((128, 1), (512, 4), (2048, 16))
ROPE_THETA = 500000.0
ROPE_DIM = HEAD_DIM // 4
N_GROUPS = 4
TOP_K = 2
EPS = 1e-6
NEG = -1e30
VMEM_LIMIT_BYTES = 56 * 1024 * 1024

F32 = jnp.float32
BF16 = jnp.bfloat16


def _cparams(*sem):
    return pltpu.CompilerParams(dimension_semantics=sem, vmem_limit_bytes=VMEM_LIMIT_BYTES)


def _dot(a, b):
    return jnp.dot(a, b, preferred_element_type=F32)


def _dot_nt(a, b):
    return lax.dot_general(a, b, (((1,), (1,)), ((), ())), preferred_element_type=F32)


def _dot_tn(a, b):
    return lax.dot_general(a, b, (((0,), (0,)), ((), ())), preferred_element_type=F32)


def _split3(x):
    hi = x.astype(BF16)
    r1 = x - hi.astype(F32)
    mid = r1.astype(BF16)
    lo = (r1 - mid.astype(F32)).astype(BF16)
    return hi, mid, lo


def _seq_lookup(u, table, default):
    out = default
    for thr, val in table:
        out = jnp.where(u >= thr, val, out)
    return out


def _rmsnorm_kernel(x_ref, g_ref, o_ref):
    x = x_ref[...]
    ms = jnp.mean(x * x, axis=-1, keepdims=True)
    o_ref[...] = (x * lax.rsqrt(ms + EPS) * g_ref[...]).astype(o_ref.dtype)


def _rmsnorm(x, g, tm=512):
    t, d = x.shape
    return pl.pallas_call(
        _rmsnorm_kernel,
        grid=(t // tm,),
        in_specs=[pl.BlockSpec((tm, d), lambda i: (i, 0)), pl.BlockSpec((1, d), lambda i: (0, 0))],
        out_specs=pl.BlockSpec((tm, d), lambda i: (i, 0)),
        out_shape=jax.ShapeDtypeStruct((t, d), BF16),
        compiler_params=_cparams("parallel"),
        name="rmsnorm",
    )(x, g.reshape(1, d).astype(F32))


def _head_epilogue(acc, gain, rope):
    outs = []
    for h in range(acc.shape[1] // HEAD_DIM):
        a = acc[:, h * HEAD_DIM:(h + 1) * HEAD_DIM]
        y = a * lax.rsqrt(jnp.mean(a * a, axis=-1, keepdims=True) + EPS) * gain[:, h * HEAD_DIM:(h + 1) * HEAD_DIM]
        if rope is not None:
            c, s1, s2 = rope
            half = ROPE_DIM // 2
            y = y * c + pltpu.roll(y, half, 1) * s1 + pltpu.roll(y, HEAD_DIM - half, 1) * s2
        outs.append(y)
    return jnp.concatenate(outs, axis=1) if len(outs) > 1 else outs[0]


EPI_COLS = 256


def _mm_kernel(*refs, n_in, epi):
    xs, ws, rest = refs[:n_in], refs[n_in:2 * n_in], refs[2 * n_in:]
    o_ref = rest[-1]
    if epi in ("headnorm", "headnorm_rope"):
        rope = (rest[1][...], rest[2][...], rest[3][...]) if epi == "headnorm_rope" else None
        x = xs[0][...]
        for c0 in range(0, o_ref.shape[1], EPI_COLS):
            cs = slice(c0, c0 + EPI_COLS)
            o_ref[:, cs] = _head_epilogue(_dot(x, ws[0][:, cs]), rest[0][:, cs], rope).astype(o_ref.dtype)
        return
    acc = _dot(xs[0][...], ws[0][...])
    for x_ref, w_ref in zip(xs[1:], ws[1:]):
        acc = acc + _dot(x_ref[...], w_ref[...])
    if epi == "plain":
        out = acc
    elif epi == "residual":
        out = acc + rest[0][...]
    else:
        raise ValueError(epi)
    o_ref[...] = out.astype(o_ref.dtype)


def _matmul(xs, ws, out_dtype, epi="plain", extra=(), pos_blk=None, tm=1024, tn=512):
    t = xs[0].shape[0]
    n = ws[0].shape[1]
    tm, tn = min(tm, t), min(tn, n)
    in_specs = [pl.BlockSpec((tm, x.shape[1]), lambda i, j: (i, 0)) for x in xs]
    in_specs += [pl.BlockSpec((w.shape[0], tn), lambda i, j: (0, j)) for w in ws]
    if epi == "residual":
        in_specs.append(pl.BlockSpec((tm, tn), lambda i, j: (i, j)))
    elif epi in ("headnorm", "headnorm_rope"):
        in_specs.append(pl.BlockSpec((1, tn), lambda i, j: (0, j)))
        if epi == "headnorm_rope":
            in_specs += [pl.BlockSpec((tm, HEAD_DIM), lambda i, j: (pos_blk(i), 0))] * 3
    return pl.pallas_call(
        functools.partial(_mm_kernel, n_in=len(xs), epi=epi),
        grid=(t // tm, n // tn),
        in_specs=in_specs,
        out_specs=pl.BlockSpec((tm, tn), lambda i, j: (i, j)),
        out_shape=jax.ShapeDtypeStruct((t, n), out_dtype),
        compiler_params=_cparams("parallel", "parallel"),
        name="matmul_" + epi,
    )(*xs, *ws, *extra)


def _mlstm_kernel(qf_ref, kf_ref, vf_ref, gf_ref, qb_ref, kb_ref, vb_ref, gb_ref, bias_ref,
                  hf_ref, hb_ref, c_ref, n_ref, m_ref, *, nh, nchunks, start_chunks, last_chunks):
    c = pl.program_id(0)
    cb = nchunks - 1 - c
    ch = MLSTM_CHUNK
    is_start = functools.reduce(jnp.logical_or, [c == s for s in start_chunks])
    is_last = functools.reduce(jnp.logical_or, [cb == e for e in last_chunks])

    def _reset(d):
        c_ref[d] = jnp.zeros((nh, HEAD_DIM, HEAD_DIM), F32)
        n_ref[d] = jnp.zeros((nh, 1, HEAD_DIM), F32)
        m_ref[d] = jnp.zeros((nh, 1, HEAD_DIM), F32)

    pl.when(is_start)(lambda: _reset(0))
    pl.when(is_last)(lambda: _reset(1))

    row = lax.broadcasted_iota(jnp.int32, (ch, ch), 0)
    col = lax.broadcasted_iota(jnp.int32, (ch, ch), 1)
    scale = HEAD_DIM ** -0.5
    heads = lambda ref: jnp.stack([ref[:, h * HEAD_DIM:(h + 1) * HEAD_DIM] for h in range(nh)])
    bdot = lambda a, b, ca, cb: lax.dot_general(a, b, (((ca,), (cb,)), ((0,), (0,))), preferred_element_type=F32)

    for d, (q_ref, k_ref, v_ref, g_ref, h_ref) in enumerate(
            ((qf_ref, kf_ref, vf_ref, gf_ref, hf_ref), (qb_ref, kb_ref, vb_ref, gb_ref, hb_ref))):
        mask = (row >= col) if d == 0 else (col >= row)
        icol0 = 2 * nh * d
        fcol0 = icol0 + nh
        tot_row = ch - 1 if d == 0 else 0
        g = g_ref[...] + bias_ref[...]
        lf = jnp.minimum(g, 0.0) - jnp.log1p(jnp.exp(-jnp.abs(g)))
        tri = jnp.where(mask, 1.0, 0.0).astype(BF16)
        hi, mid, lo = _split3(lf)
        bcum = _dot(tri, hi) + _dot(tri, mid) + _dot(tri, lo)
        a = g - pltpu.roll(bcum, HEAD_DIM - nh, 1)
        a_t = a.T
        bcol = jnp.stack([bcum[:, fcol0 + h:fcol0 + h + 1] for h in range(nh)])
        acol = jnp.stack([a[:, icol0 + h:icol0 + h + 1] for h in range(nh)])
        arow = jnp.stack([a_t[icol0 + h:icol0 + h + 1, :] for h in range(nh)])
        b_tot = bcol[:, tot_row:tot_row + 1, :]
        m_st = m_ref[d][:, :, 0:1]
        c_prev = c_ref[d]
        n_prev = n_ref[d]

        log_d = jnp.where(mask[None], bcol + arow, NEG)
        m_intra = jnp.max(log_d, axis=2, keepdims=True)
        log_inter = bcol + m_st
        m_row = jnp.maximum(log_inter, m_intra)
        dmat = jnp.exp(log_d - m_row)
        q = heads(q_ref)
        ks = (heads(k_ref).astype(F32) * scale).astype(BF16)
        v = heads(v_ref)
        p = bdot(q, ks, 2, 2) * dmat
        w_inter = jnp.exp(log_inter - m_row)
        num = bdot(p.astype(BF16), v, 2, 1) + w_inter * bdot(q, c_prev.astype(BF16), 2, 1)
        den = (jnp.sum(p, axis=2, keepdims=True)
               + w_inter * jnp.sum(q.astype(F32) * n_prev, axis=2, keepdims=True))
        h_out = num / jnp.maximum(jnp.abs(den), jnp.exp(-m_row))
        for h in range(nh):
            h_ref[:, h * HEAD_DIM:(h + 1) * HEAD_DIM] = h_out[h]

        m_kv = b_tot + jnp.max(arow, axis=2, keepdims=True)
        m_new = jnp.maximum(b_tot + m_st, m_kv)
        decay = jnp.exp(b_tot + m_st - m_new)
        wexp = jnp.exp(b_tot + acol - m_new)
        kw = ks.astype(F32) * wexp
        c_ref[d] = decay * c_prev + bdot(kw.astype(BF16), v, 1, 1)
        n_ref[d] = decay * n_prev + jnp.sum(kw, axis=1, keepdims=True)
        m_ref[d] = jnp.broadcast_to(m_new, (nh, 1, HEAD_DIM))


def _mlstm(u_a, gates, gate_bias, seqs, nh):
    t = u_a.shape[0]
    ch = MLSTM_CHUNK
    nchunks = t // ch
    da = nh * HEAD_DIM
    start_chunks = [s // ch for s, _ in seqs]
    last_chunks = [(s + l) // ch - 1 for s, l in seqs]
    fwd = lambda j: pl.BlockSpec((ch, da), lambda c, j=j: (c, j))
    bwd = lambda j: pl.BlockSpec((ch, da), lambda c, j=j: (nchunks - 1 - c, j))
    gspec_f = pl.BlockSpec((ch, 128), lambda c: (c, 0))
    gspec_b = pl.BlockSpec((ch, 128), lambda c: (nchunks - 1 - c, 0))
    bias = jnp.zeros((1, 128), F32).at[0, :4 * nh].set(gate_bias.reshape(-1).astype(F32))
    return pl.pallas_call(
        functools.partial(_mlstm_kernel, nh=nh, nchunks=nchunks, start_chunks=start_chunks,
                          last_chunks=last_chunks),
        grid=(nchunks,),
        in_specs=[fwd(0), fwd(1), fwd(2), gspec_f, bwd(0), bwd(1), bwd(2), gspec_b,
                  pl.BlockSpec((1, 128), lambda c: (0, 0))],
        out_specs=[pl.BlockSpec((ch, da), lambda c: (c, 0)),
                   pl.BlockSpec((ch, da), lambda c: (nchunks - 1 - c, 0))],
        out_shape=[jax.ShapeDtypeStruct((t, da), F32)] * 2,
        scratch_shapes=[pltpu.VMEM((2, nh, HEAD_DIM, HEAD_DIM), F32),
                        pltpu.VMEM((2, nh, 1, HEAD_DIM), F32),
                        pltpu.VMEM((2, nh, 1, HEAD_DIM), F32)],
        compiler_params=_cparams("arbitrary"),
        name="mlstm",
    )(u_a, u_a, u_a, gates, u_a, u_a, u_a, gates, bias)


def _mlstm_gate_kernel(hf_ref, hb_ref, o_ref, g_ref, y_ref):
    hsum = hf_ref[...] + hb_ref[...]
    o = o_ref[...].astype(F32)
    g = g_ref[...]
    outs = []
    for h in range(hsum.shape[1] // HEAD_DIM):
        hs = slice(h * HEAD_DIM, (h + 1) * HEAD_DIM)
        a = hsum[:, hs]
        y = a * lax.rsqrt(jnp.mean(a * a, axis=-1, keepdims=True) + EPS) * g[:, hs]
        outs.append(jax.nn.sigmoid(o[:, hs]) * y)
    y_ref[...] = jnp.concatenate(outs, axis=1).astype(y_ref.dtype)


def _mlstm_gate(h_f, h_b, u_a, hnorm_g, tm=512):
    t, da = h_f.shape
    return pl.pallas_call(
        _mlstm_gate_kernel,
        grid=(t // tm,),
        in_specs=[pl.BlockSpec((tm, da), lambda i: (i, 0)), pl.BlockSpec((tm, da), lambda i: (i, 0)),
                  pl.BlockSpec((tm, da), lambda i: (i, 3)), pl.BlockSpec((1, da), lambda i: (0, 0))],
        out_specs=pl.BlockSpec((tm, da), lambda i: (i, 0)),
        out_shape=jax.ShapeDtypeStruct((t, da), BF16),
        compiler_params=_cparams("parallel"),
        name="mlstm_gate",
    )(h_f, h_b, u_a, hnorm_g.reshape(1, da).astype(F32))


NA_QROWS = 4
NA_KROWS = NA_QROWS + NA_ROWS
NA_UNROLL = 2


def _na_bias_table(rel_bias):
    h = rel_bias.shape[0]
    c = jnp.arange(GRID_W)
    dc = jnp.clip(c[None, :] - c[:, None], -(NA_COLS - 1), NA_COLS - 1) + NA_COLS - 1
    c_start = jnp.clip(c - NA_COLS // 2, 0, GRID_W - NA_COLS)
    col_ok = (c[None, :] >= c_start[:, None]) & (c[None, :] < c_start[:, None] + NA_COLS)
    var = jnp.arange(3)[:, None, None]
    qi = jnp.arange(NA_QROWS)[None, :, None]
    kr = jnp.arange(NA_KROWS)[None, None, :]
    dr = kr - var * NA_QROWS - qi
    first = jnp.where(var == 0, 0, jnp.where(var == 1, qi, NA_QROWS))
    row_ok = (kr >= first) & (kr < first + NA_ROWS)
    dr_idx = jnp.clip(dr + NA_ROWS - 1, 0, 2 * NA_ROWS - 2)
    tbl = rel_bias.astype(F32)[:, dr_idx[:, :, :, None, None], dc[None, None, None, :, :]]
    ok = row_ok[:, :, :, None, None] & col_ok[None, None, None, :, :]
    tbl = jnp.where(ok[None], tbl, NEG)
    return jnp.transpose(tbl, (0, 1, 2, 4, 3, 5)).reshape(h, 3, NA_QROWS * GRID_W, NA_KROWS * GRID_W)


def _na_kernel(q_ref, kp_ref, km_ref, kn_ref, vp_ref, vm_ref, vn_ref, bias_ref, o_ref, kcat, vcat,
               *, seg_rows, row_lo, row_hi):
    s = pl.program_id(1)
    halo = NA_ROWS * GRID_W
    seg = seg_rows * GRID_W
    kcat[0:halo] = kp_ref[...]
    kcat[halo:halo + seg] = km_ref[...]
    kcat[halo + seg:halo + seg + halo] = kn_ref[...]
    vcat[0:halo] = vp_ref[...]
    vcat[halo:halo + seg] = vm_ref[...]
    vcat[halo + seg:halo + seg + halo] = vn_ref[...]
    r_lo = _seq_lookup(s, row_lo, 0)
    r_hi = _seq_lookup(s, row_hi, 0)
    scale = HEAD_DIM ** -0.5
    nq = NA_QROWS * GRID_W
    nk = NA_KROWS * GRID_W

    def one_group(gl):
        r0 = s * seg_rows + gl * NA_QROWS
        u = jnp.clip(r0 - NA_ROWS // 2, r_lo, r_hi - NA_KROWS)
        off = pl.multiple_of((u - s * seg_rows + NA_ROWS) * GRID_W, GRID_W)
        qoff = pl.multiple_of(gl * nq, nq)
        q = q_ref[pl.ds(qoff, nq), :]
        sc = _dot_nt(q, kcat[pl.ds(off, nk), :]) * scale + bias_ref[(r0 - u) // NA_QROWS]
        e = jnp.exp(sc - jnp.max(sc, axis=1, keepdims=True))
        l = jnp.sum(e, axis=1, keepdims=True)
        o_ref[pl.ds(qoff, nq), :] = (_dot(e.astype(BF16), vcat[pl.ds(off, nk), :]) / l).astype(o_ref.dtype)

    def body(it, carry):
        for k in range(NA_UNROLL):
            one_group(it * NA_UNROLL + k)
        return carry

    lax.fori_loop(0, seg_rows // (NA_QROWS * NA_UNROLL), body, 0)


def _neighbourhood_attention(qk, v, bias_tbl, seqs, seg):
    t, dh = v.shape
    nh = dh // HEAD_DIM
    nseg = t // seg
    seg_rows = seg // GRID_W
    assert all(l // GRID_W >= NA_KROWS and (s // GRID_W) % NA_QROWS == 0 and (l // GRID_W) % NA_QROWS == 0
               for s, l in seqs)
    halo = NA_ROWS * GRID_W
    hpb = seg // halo
    nhb = t // halo
    row_lo = [(s // seg, s // GRID_W) for s, _ in seqs]
    row_hi = [(s // seg, (s + l) // GRID_W) for s, l in seqs]
    main = lambda c0: pl.BlockSpec((seg, HEAD_DIM), lambda h, s: (s, c0 + h))
    prev = lambda c0: pl.BlockSpec((halo, HEAD_DIM), lambda h, s: (jnp.maximum(s * hpb - 1, 0), c0 + h))
    nxt = lambda c0: pl.BlockSpec((halo, HEAD_DIM), lambda h, s: (jnp.minimum((s + 1) * hpb, nhb - 1), c0 + h))
    return pl.pallas_call(
        functools.partial(_na_kernel, seg_rows=seg_rows, row_lo=row_lo, row_hi=row_hi),
        grid=(nh, nseg),
        in_specs=[main(0), prev(nh), main(nh), nxt(nh), prev(0), main(0), nxt(0),
                  pl.BlockSpec((None, 3, NA_QROWS * GRID_W, NA_KROWS * GRID_W), lambda h, s: (h, 0, 0, 0))],
        out_specs=main(0),
        out_shape=jax.ShapeDtypeStruct((t, dh), BF16),
        scratch_shapes=[pltpu.VMEM((seg + 2 * halo, HEAD_DIM), BF16)] * 2,
        compiler_params=_cparams("parallel", "parallel"),
        name="natten",
    )(qk, qk, qk, qk, v, v, v, bias_tbl)


DIL_BLK = 128
DIL_RADIUS = 64
DIL_MAX = max(d for _, d in DIL_PAIRS)
DIL_QB = DIL_BLK * DIL_MAX
DIL_HALO = DIL_RADIUS * DIL_MAX


def _dilated_kernel(q_ref, kp_ref, kc_ref, kn_ref, vp_ref, vc_ref, vn_ref, y_ref, num_s, mx_s, den_s,
                    *, first_blocks, last_blocks):
    i = pl.program_id(1)
    has_prev = jnp.logical_not(functools.reduce(jnp.logical_or, [i == b for b in first_blocks]))
    has_next = jnp.logical_not(functools.reduce(jnp.logical_or, [i == b for b in last_blocks]))
    nk = DIL_BLK + 2 * DIL_RADIUS
    j_io = lax.broadcasted_iota(jnp.int32, (DIL_BLK, nk), 0)
    s_io = lax.broadcasted_iota(jnp.int32, (DIL_BLK, nk), 1)
    band = jnp.abs(s_io - DIL_RADIUS - j_io) <= DIL_RADIUS
    ok_prev = jnp.logical_or(s_io >= DIL_RADIUS, has_prev)
    ok_next = jnp.logical_or(s_io < DIL_BLK + DIL_RADIUS, has_next)
    madd = {}
    for up in (False, True):
        for un in (False, True):
            ok = band
            ok = jnp.logical_and(ok, ok_prev) if up else ok
            ok = jnp.logical_and(ok, ok_next) if un else ok
            madd[up, un] = jnp.where(ok, 0.0, NEG)
    scale = HEAD_DIM ** -0.5

    def rows(start, size, d):
        return pl.ds(start, size, stride=d) if d > 1 else pl.ds(start, size)

    for b, (_, d) in enumerate(DIL_PAIRS):
        per_class = DIL_QB // d
        nj = per_class // DIL_BLK
        for g in range(d):
            for j in range(nj):
                up, un = j == 0, j == nj - 1
                qrows = rows(g + d * DIL_BLK * j, DIL_BLK, d)
                m0 = max(DIL_BLK * j - DIL_RADIUS, 0)
                m1 = min(DIL_BLK * (j + 1) + DIL_RADIUS, per_class)
                kparts, vparts = [], []
                if up:
                    r = rows(DIL_HALO - DIL_RADIUS * d + g, DIL_RADIUS, d)
                    kparts.append(kp_ref[r, :])
                    vparts.append(vp_ref[r, :])
                r = rows(g + d * m0, m1 - m0, d)
                kparts.append(kc_ref[r, :])
                vparts.append(vc_ref[r, :])
                if un:
                    r = rows(g, DIL_RADIUS, d)
                    kparts.append(kn_ref[r, :])
                    vparts.append(vn_ref[r, :])
                kk = jnp.concatenate(kparts, axis=0).astype(BF16)
                vv = jnp.concatenate(vparts, axis=0).astype(BF16)
                sc = _dot_nt(q_ref[qrows, :].astype(BF16), kk) * scale + madd[up, un]
                mx = jnp.max(sc, axis=1, keepdims=True)
                e = jnp.exp(sc - mx)
                den = jnp.sum(e, axis=1, keepdims=True)
                num_s[b, qrows, :] = _dot(e.astype(BF16), vv)
                mx_s[b, qrows, :] = jnp.broadcast_to(mx, (DIL_BLK, HEAD_DIM))
                den_s[b, qrows, :] = jnp.broadcast_to(den, (DIL_BLK, HEAD_DIM))

    nb = len(DIL_PAIRS)
    mxs = [mx_s[b] for b in range(nb)]
    m_top = functools.reduce(jnp.maximum, mxs)
    w = [jnp.exp(m - m_top) for m in mxs]
    num = functools.reduce(jnp.add, [w[b] * num_s[b] for b in range(nb)])
    den = functools.reduce(jnp.add, [w[b] * den_s[b] for b in range(nb)])
    y_ref[...] = (num / den).astype(y_ref.dtype)


def _dilated_attention(qk, v, seqs):
    t, dh = v.shape
    nh = dh // HEAD_DIM
    assert all(window // (2 * dil) == DIL_RADIUS for window, dil in DIL_PAIRS)
    assert all(s % DIL_QB == 0 and l % DIL_QB == 0 for s, l in seqs)
    nq = t // DIL_QB
    hb = DIL_QB // DIL_HALO
    nhalo = t // DIL_HALO
    first_blocks = [s // DIL_QB for s, _ in seqs]
    last_blocks = [(s + l) // DIL_QB - 1 for s, l in seqs]
    cur = lambda c0: pl.BlockSpec((DIL_QB, HEAD_DIM), lambda h, i: (i, c0 + h))
    prev = lambda c0: pl.BlockSpec((DIL_HALO, HEAD_DIM), lambda h, i: (jnp.maximum(i * hb - 1, 0), c0 + h))
    nxt = lambda c0: pl.BlockSpec((DIL_HALO, HEAD_DIM), lambda h, i: (jnp.minimum((i + 1) * hb, nhalo - 1), c0 + h))
    nb = len(DIL_PAIRS)
    return pl.pallas_call(
        functools.partial(_dilated_kernel, first_blocks=first_blocks, last_blocks=last_blocks),
        grid=(nh, nq),
        in_specs=[cur(0), prev(nh), cur(nh), nxt(nh), prev(0), cur(0), nxt(0)],
        out_specs=cur(0),
        out_shape=jax.ShapeDtypeStruct((t, dh), BF16),
        scratch_shapes=[pltpu.VMEM((nb, DIL_QB, HEAD_DIM), F32)] * 3,
        compiler_params=_cparams("parallel", "parallel"),
        name="dilated",
    )(qk, qk, qk, qk, v, v, v)


def _rope_tables(max_len):
    half = ROPE_DIM // 2
    inv_freq = ROPE_THETA ** (-jnp.arange(half, dtype=F32) / half)
    ang = jnp.arange(max_len, dtype=F32)[:, None] * inv_freq[None, :]
    cos, sin = jnp.cos(ang), jnp.sin(ang)
    ones = jnp.ones((max_len, HEAD_DIM - ROPE_DIM), F32)
    zeros = jnp.zeros((max_len, HEAD_DIM - half), F32)
    c = jnp.concatenate([cos, cos, ones], axis=1)
    s1 = jnp.concatenate([jnp.zeros((max_len, half), F32), sin, zeros[:, :HEAD_DIM - ROPE_DIM]], axis=1)
    s2 = jnp.concatenate([-sin, zeros], axis=1)
    return c, s1, s2


def _router_kernel(x_ref, g_ref, wr_ref, rb_ref, xn_ref, e_ref, gt_ref, *, n_exp):
    x = x_ref[...]
    xn = x * lax.rsqrt(jnp.mean(x * x, axis=-1, keepdims=True) + EPS) * g_ref[...]
    xn_ref[...] = xn.astype(xn_ref.dtype)
    tm = x.shape[0]
    epg = n_exp // N_GROUPS
    xh, xm, _ = _split3(xn)
    wh, wm, _ = _split3(wr_ref[...])
    logits = _dot_nt(wh, xh) + _dot_nt(wh, xm) + _dot_nt(wm, xh)
    scores = jax.nn.sigmoid(logits)
    sel = (scores + rb_ref[:, 0:1]).reshape(N_GROUPS, epg, tm)
    sc3 = scores.reshape(N_GROUPS, epg, tm)
    io = lax.broadcasted_iota(jnp.int32, (N_GROUPS, epg, tm), 1)
    gio = lax.broadcasted_iota(jnp.int32, (N_GROUPS, 1, tm), 0)
    m1 = jnp.max(sel, axis=1, keepdims=True)
    i1 = jnp.min(jnp.where(sel == m1, io, epg), axis=1, keepdims=True)
    sel_b = jnp.where(io == i1, -jnp.inf, sel)
    m2 = jnp.max(sel_b, axis=1, keepdims=True)
    i2 = jnp.min(jnp.where(sel_b == m2, io, epg), axis=1, keepdims=True)
    gs = m1 + m2
    gbest = jnp.min(jnp.where(gs == jnp.max(gs, axis=0, keepdims=True), gio, N_GROUPS), axis=0, keepdims=True)
    in_g = gio == gbest
    outs_e, outs_w = [], []
    for ik in (i1, i2):
        pick = in_g & (io == ik)
        outs_e.append(jnp.sum(jnp.where(in_g, gbest * epg + ik, 0), axis=0))
        outs_w.append(jnp.sum(jnp.sum(jnp.where(pick, sc3, 0.0), axis=1, keepdims=True), axis=0))
    wsum = outs_w[0] + outs_w[1]
    e_ref[0:1, :] = outs_e[0]
    e_ref[1:2, :] = outs_e[1]
    gt_ref[0:1, :] = outs_w[0] / wsum
    gt_ref[1:2, :] = outs_w[1] / wsum


def _router(x, g, w_router, router_bias, tm=512):
    t, d = x.shape
    n_exp = w_router.shape[1]
    rb = jnp.broadcast_to(router_bias.astype(F32)[:, None], (n_exp, 128))
    return pl.pallas_call(
        functools.partial(_router_kernel, n_exp=n_exp),
        grid=(t // tm,),
        in_specs=[pl.BlockSpec((tm, d), lambda i: (i, 0)), pl.BlockSpec((1, d), lambda i: (0, 0)),
                  pl.BlockSpec((n_exp, d), lambda i: (0, 0)), pl.BlockSpec((n_exp, 128), lambda i: (0, 0))],
        out_specs=[pl.BlockSpec((tm, d), lambda i: (i, 0)), pl.BlockSpec((2, tm), lambda i: (0, i)),
                   pl.BlockSpec((2, tm), lambda i: (0, i))],
        out_shape=[jax.ShapeDtypeStruct((t, d), F32), jax.ShapeDtypeStruct((2, t), jnp.int32),
                   jax.ShapeDtypeStruct((2, t), F32)],
        compiler_params=_cparams("parallel"),
        name="router",
    )(x, g.reshape(1, d).astype(F32), w_router.T.astype(F32), rb)


def _rank_kernel(e_ref, rank_ref, cnt_ref, run_ref, *, n_exp):
    i = pl.program_id(0)
    tm = e_ref.shape[1]

    @pl.when(i == 0)
    def _():
        run_ref[...] = jnp.zeros_like(run_ref)

    eio = lax.broadcasted_iota(jnp.int32, (n_exp, tm), 0)
    oh = [(eio == e_ref[k:k + 1, :]) for k in range(TOP_K)]
    ohf = [jnp.where(o, 1.0, 0.0) for o in oh]
    both = ohf[0] + ohf[1]
    r_i = lax.broadcasted_iota(jnp.int32, (tm, tm), 0)
    c_i = lax.broadcasted_iota(jnp.int32, (tm, tm), 1)
    upper = jnp.where(r_i <= c_i, 1.0, 0.0).astype(BF16)
    cum = _dot(both.astype(BF16), upper)
    base = run_ref[:, 0:1] + cum - both
    for k in range(TOP_K):
        rank_ref[k:k + 1, :] = jnp.sum(jnp.where(oh[k], base, 0.0), axis=0, keepdims=True).astype(jnp.int32)
    run_new = run_ref[...] + cum[:, tm - 1:tm]
    run_ref[...] = run_new
    cnt_ref[...] = run_new.astype(jnp.int32)


def _rank(e, n_exp, tm=512):
    t = e.shape[1]
    return pl.pallas_call(
        functools.partial(_rank_kernel, n_exp=n_exp),
        grid=(t // tm,),
        in_specs=[pl.BlockSpec((2, tm), lambda i: (0, i))],
        out_specs=[pl.BlockSpec((2, tm), lambda i: (0, i)), pl.BlockSpec((n_exp, 128), lambda i: (0, 0))],
        out_shape=[jax.ShapeDtypeStruct((2, t), jnp.int32), jax.ShapeDtypeStruct((n_exp, 128), jnp.int32)],
        scratch_shapes=[pltpu.VMEM((n_exp, 128), F32)],
        compiler_params=_cparams("arbitrary"),
        name="moe_rank",
    )(e)


MOE_BLK = 256


CAST_ROWS = 256


def _row_copies(idx_ref, n_rows, make_copy):
    def start(r, c):
        for k in range(TOP_K):
            make_copy(k, r, idx_ref[k, r]).start()
        return c

    def wait(r, c):
        for k in range(TOP_K):
            make_copy(k, r, idx_ref[k, r]).wait()
        return c

    lax.fori_loop(0, n_rows, start, 0)
    lax.fori_loop(0, n_rows, wait, 0)


def _dispatch_kernel(fill_ref, slot_ref, xn_hbm, xs_out, zbuf, sem, zsem, *, tm, n_fill):
    i = pl.program_id(0)

    @pl.when(i == 0)
    def _():
        zbuf[...] = jnp.zeros_like(zbuf)

        def fill(j):
            rows = pl.ds(pl.multiple_of(fill_ref[j] * MOE_BLK, MOE_BLK), MOE_BLK)
            return pltpu.make_async_copy(zbuf, xs_out.at[rows, :], zsem)

        def start(j, c):
            pl.when(fill_ref[n_fill + j] > 0)(lambda: fill(j).start())
            return c

        def wait(j, c):
            pl.when(fill_ref[n_fill + j] > 0)(lambda: fill(j).wait())
            return c

        lax.fori_loop(0, n_fill, start, 0)
        lax.fori_loop(0, n_fill, wait, 0)

    base = i * tm
    _row_copies(slot_ref, tm, lambda k, r, s: pltpu.make_async_copy(
        xn_hbm.at[pl.ds(base + r, 1), :], xs_out.at[pl.ds(s, 1), :], sem))


def _dispatch(slot, xn, fill, cap, tm=512):
    t, d = xn.shape
    return pl.pallas_call(
        functools.partial(_dispatch_kernel, tm=tm, n_fill=fill.shape[0] // 2),
        grid_spec=pltpu.PrefetchScalarGridSpec(
            num_scalar_prefetch=1, grid=(t // tm,),
            in_specs=[pl.BlockSpec((TOP_K, tm), lambda i, f: (0, i), memory_space=pltpu.SMEM),
                      pl.BlockSpec(memory_space=pl.ANY)],
            out_specs=pl.BlockSpec(memory_space=pl.ANY),
            scratch_shapes=[pltpu.VMEM((MOE_BLK, d), F32), pltpu.SemaphoreType.DMA, pltpu.SemaphoreType.DMA]),
        out_shape=jax.ShapeDtypeStruct((cap, d), F32),
        compiler_params=_cparams("arbitrary"),
        name="moe_dispatch",
    )(fill, slot, xn)


def _cast_resident(w_ref, s_ref):
    step = min(CAST_ROWS, w_ref.shape[0])

    def body(i, c):
        rows = pl.ds(pl.multiple_of(i * step, step), step)
        s_ref[rows, :] = w_ref[rows, :].astype(s_ref.dtype)
        return c
    lax.fori_loop(0, w_ref.shape[0] // step, body, 0)


def _expert_changed(b, be_ref):
    return jnp.logical_or(b == 0, be_ref[b] != be_ref[jnp.maximum(b - 1, 0)])


def _ffn_up_kernel(be_ref, nb_ref, x_ref, wg_ref, wu_ref, h_ref, wg_s, wu_s):
    b = pl.program_id(0)

    @pl.when(_expert_changed(b, be_ref))
    def _():
        _cast_resident(wg_ref, wg_s)
        _cast_resident(wu_ref, wu_s)

    @pl.when(b < nb_ref[0])
    def _():
        x = x_ref[...].astype(BF16)
        hg = _dot(x, wg_s[...])
        hu = _dot(x, wu_s[...])
        h_ref[...] = ((hg * jax.nn.sigmoid(hg)) * hu).astype(h_ref.dtype)

    @pl.when(b >= nb_ref[0])
    def _():
        h_ref[...] = jnp.zeros_like(h_ref)


def _ffn_down_kernel(be_ref, nb_ref, h_ref, wd_ref, y_ref, wd_s):
    b = pl.program_id(0)
    pl.when(_expert_changed(b, be_ref))(lambda: _cast_resident(wd_ref, wd_s))

    @pl.when(b < nb_ref[0])
    def _():
        y_ref[...] = _dot(h_ref[...], wd_s[...])

    @pl.when(b >= nb_ref[0])
    def _():
        y_ref[...] = jnp.zeros_like(y_ref)


def _expert_ffn(xs, blk_expert, n_used, w_gate, w_up, w_down):
    cap, d = xs.shape
    f = w_gate.shape[2]
    nblk = cap // MOE_BLK
    used = lambda b, be, nb: (jnp.minimum(b, nb[0] - 1), 0)
    wsel = lambda b, be, nb: (be[b], 0, 0)
    h = pl.pallas_call(
        _ffn_up_kernel,
        grid_spec=pltpu.PrefetchScalarGridSpec(
            num_scalar_prefetch=2, grid=(nblk,),
            in_specs=[pl.BlockSpec((MOE_BLK, d), used), pl.BlockSpec((None, d, f), wsel),
                      pl.BlockSpec((None, d, f), wsel)],
            out_specs=pl.BlockSpec((MOE_BLK, f), lambda b, be, nb: (b, 0)),
            scratch_shapes=[pltpu.VMEM((d, f), BF16)] * 2),
        out_shape=jax.ShapeDtypeStruct((cap, f), BF16),
        compiler_params=_cparams("arbitrary"),
        name="moe_ffn_up",
    )(blk_expert, n_used, xs, w_gate, w_up)
    return pl.pallas_call(
        _ffn_down_kernel,
        grid_spec=pltpu.PrefetchScalarGridSpec(
            num_scalar_prefetch=2, grid=(nblk,),
            in_specs=[pl.BlockSpec((MOE_BLK, f), used), pl.BlockSpec((None, f, d), wsel)],
            out_specs=pl.BlockSpec((MOE_BLK, d), lambda b, be, nb: (b, 0)),
            scratch_shapes=[pltpu.VMEM((f, d), BF16)]),
        out_shape=jax.ShapeDtypeStruct((cap, d), F32),
        compiler_params=_cparams("arbitrary"),
        name="moe_ffn_down",
    )(blk_expert, n_used, h, w_down)


def _combine_kernel(slot_ref, x_ref, gt_ref, ys_hbm, o_ref, buf, sem, *, tm):
    _row_copies(slot_ref, tm, lambda k, r, s: pltpu.make_async_copy(
        ys_hbm.at[pl.ds(s, 1), :], buf.at[k, pl.ds(r, 1), :], sem))
    o_ref[...] = x_ref[...] + (gt_ref[:, 0:1] * buf[0] + gt_ref[:, 1:2] * buf[1])


def _combine(slot, x, gt_cols, ys, tm=256):
    t, d = x.shape
    return pl.pallas_call(
        functools.partial(_combine_kernel, tm=tm),
        grid=(t // tm,),
        in_specs=[pl.BlockSpec((TOP_K, tm), lambda i: (0, i), memory_space=pltpu.SMEM),
                  pl.BlockSpec((tm, d), lambda i: (i, 0)), pl.BlockSpec((tm, TOP_K), lambda i: (i, 0)),
                  pl.BlockSpec(memory_space=pl.ANY)],
        out_specs=pl.BlockSpec((tm, d), lambda i: (i, 0)),
        out_shape=jax.ShapeDtypeStruct((t, d), F32),
        scratch_shapes=[pltpu.VMEM((TOP_K, tm, d), F32), pltpu.SemaphoreType.DMA],
        compiler_params=_cparams("arbitrary"),
        name="moe_combine",
    )(slot, x, gt_cols, ys)


def _moe(x, ln_g, w_router, router_bias, w_gate, w_up, w_down):
    t, d = x.shape
    n_exp = w_router.shape[1]
    xn, e, gt = _router(x, ln_g, w_router, router_bias)
    rank, cnt = _rank(e, n_exp)
    counts = cnt[:, 0]
    padded = (counts + MOE_BLK - 1) // MOE_BLK * MOE_BLK
    pad_end = jnp.cumsum(padded)
    pad_start = pad_end - padded
    cap = t * TOP_K + n_exp * MOE_BLK
    nblk = cap // MOE_BLK
    slot = (pad_start[e] + rank).astype(jnp.int32)
    blk_expert = jnp.minimum(jnp.searchsorted(pad_end, jnp.arange(nblk, dtype=jnp.int32) * MOE_BLK, side="right"),
                             n_exp - 1).astype(jnp.int32)
    n_used = (pad_end[-1] // MOE_BLK).astype(jnp.int32).reshape(1)
    trail = n_used[0] + jnp.arange(n_exp, dtype=jnp.int32)
    fill_blk = jnp.concatenate([pad_end // MOE_BLK - 1, jnp.minimum(trail, nblk - 1)])
    fill_ok = jnp.concatenate([padded > 0, trail < nblk])
    fill = jnp.concatenate([jnp.maximum(fill_blk, 0), fill_ok.astype(jnp.int32)]).astype(jnp.int32)
    xs = _dispatch(slot, xn, fill, cap)
    ys = _expert_ffn(xs, blk_expert, n_used, w_gate, w_up, w_down)
    return _combine(slot, x, gt.T, ys)


def _ab_layer(x, seqs, seg, ln_g, w_in, gate_bias, hnorm_g, qnorm_g, knorm_g, rel_bias, w_out):
    nh_a = gate_bias.shape[1]
    nh_b = rel_bias.shape[0]
    d_a, d_b = nh_a * HEAD_DIM, nh_b * HEAD_DIM
    xn = _rmsnorm(x, ln_g)
    w = w_in.astype(BF16)
    g0 = 4 * d_a
    g1 = g0 + 4 * nh_a
    u_a = _matmul([xn], [w[:, :g0]], BF16)
    w_g = jnp.zeros((w.shape[0], 128), BF16).at[:, :4 * nh_a].set(w[:, g0:g1])
    gates = _matmul([xn], [w_g], F32)
    gain = jnp.concatenate([jnp.tile(qnorm_g, nh_b), jnp.tile(knorm_g, nh_b)]).reshape(1, -1).astype(F32)
    qk_b = _matmul([xn], [w[:, g1:g1 + 2 * d_b]], BF16, epi="headnorm", extra=(gain,), tn=1024)
    v_b = _matmul([xn], [w[:, g1 + 2 * d_b:]], BF16)
    h_f, h_b = _mlstm(u_a, gates, gate_bias, seqs, nh_a)
    y_a = _mlstm_gate(h_f, h_b, u_a, hnorm_g)
    y_b = _neighbourhood_attention(qk_b, v_b, _na_bias_table(rel_bias), seqs, seg)
    wo = w_out.astype(BF16)
    return _matmul([y_a, y_b], [wo[:d_a], wo[d_a:]], F32, epi="residual", extra=(x,))


def _dilated_layer(x, seqs, ln_g, w_in, qnorm_g, knorm_g, w_out):
    t = x.shape[0]
    d_c = w_in.shape[1] // 3
    nh = d_c // HEAD_DIM
    xn = _rmsnorm(x, ln_g)
    w = w_in.astype(BF16)
    tm = 1024
    max_len = max(l for _, l in seqs)
    tables = _rope_tables(max_len)
    pos_tbl = [(s // tm, s // tm) for s, _ in seqs]
    pos_blk = lambda i: i - _seq_lookup(i, pos_tbl, 0)
    gain = jnp.concatenate([jnp.tile(qnorm_g, nh), jnp.tile(knorm_g, nh)]).reshape(1, -1).astype(F32)
    qk = _matmul([xn], [w[:, :2 * d_c]], F32, epi="headnorm_rope", extra=(gain,) + tables,
                 pos_blk=pos_blk, tm=tm, tn=1024)
    v = _matmul([xn], [w[:, 2 * d_c:]], F32, tm=tm)
    y = _dilated_attention(qk, v, seqs)
    return _matmul([y], [w_out.astype(BF16)], F32, epi="residual", extra=(x,))


def kernel(x_prompt, x_sample, ab_ln, ab_w_in, ab_gate_bias, ab_hnorm, ab_qnorm, ab_knorm, ab_relbias, ab_w_out,
           c_ln, c_w_in, c_qnorm, c_knorm, c_w_out, ffn_ln, w_router, router_bias, w_gate, w_up, w_down):
    bp, lp, d = x_prompt.shape
    bs, ls, _ = x_sample.shape
    tp = bp * lp
    seqs = [(b * lp, lp) for b in range(bp)] + [(tp + b * ls, ls) for b in range(bs)]
    seg = math.gcd(lp, ls)
    x = jnp.concatenate([x_prompt.reshape(tp, d), x_sample.reshape(bs * ls, d)], axis=0)
    depth = ffn_ln.shape[0]
    for layer in range(depth):
        j = layer // 2
        if layer % 2 == 0:
            x = _ab_layer(x, seqs, seg, ab_ln[j], ab_w_in[j], ab_gate_bias[j], ab_hnorm[j].reshape(-1),
                          ab_qnorm[j], ab_knorm[j], ab_relbias[j], ab_w_out[j])
        else:
            x = _dilated_layer(x, seqs, c_ln[j], c_w_in[j], c_qnorm[j], c_knorm[j], c_w_out[j])
        x = _moe(x, ffn_ln[layer], w_router, router_bias, w_gate[layer], w_up[layer], w_down[layer])
    return x[:tp].reshape(bp, lp, d), x[tp:].reshape(bs, ls, d)
```

```python
import functools
import math

import jax
import jax.numpy as jnp
import numpy as np
from jax import lax
from jax.experimental import pallas as pl
from jax.experimental.pallas import tpu as pltpu

HEAD_DIM = 128
MLSTM_CHUNK = 128
GRID_W = 64
NA_ROWS = 8
NA_COLS = 16
DIL_PAIRS = ((128, 1), (512, 4), (2048, 16))
ROPE_THETA = 500000.0
ROPE_DIM = HEAD_DIM // 4
N_GROUPS = 4
TOP_K = 2
EPS = 1e-6
NEG = -1e30
VMEM_LIMIT_BYTES = 56 * 1024 * 1024

F32 = jnp.float32
BF16 = jnp.bfloat16


def _cparams(*sem):
    return pltpu.CompilerParams(dimension_semantics=sem, vmem_limit_bytes=VMEM_LIMIT_BYTES)


def _dot(a, b):
    return jnp.dot(a, b, preferred_element_type=F32)


def _dot_nt(a, b):
    return lax.dot_general(a, b, (((1,), (1,)), ((), ())), preferred_element_type=F32)


def _dot_tn(a, b):
    return lax.dot_general(a, b, (((0,), (0,)), ((), ())), preferred_element_type=F32)


def _split3(x):
    hi = x.astype(BF16)
    r1 = x - hi.astype(F32)
    mid = r1.astype(BF16)
    lo = (r1 - mid.astype(F32)).astype(BF16)
    return hi, mid, lo


def _seq_lookup(u, table, default):
    out = default
    for thr, val in table:
        out = jnp.where(u >= thr, val, out)
    return out


def _rmsnorm_kernel(x_ref, g_ref, o_ref):
    x = x_ref[...]
    ms = jnp.mean(x * x, axis=-1, keepdims=True)
    o_ref[...] = (x * lax.rsqrt(ms + EPS) * g_ref[...]).astype(o_ref.dtype)


def _rmsnorm(x, g, tm=512):
    t, d = x.shape
    return pl.pallas_call(
        _rmsnorm_kernel,
        grid=(t // tm,),
        in_specs=[pl.BlockSpec((tm, d), lambda i: (i, 0)), pl.BlockSpec((1, d), lambda i: (0, 0))],
        out_specs=pl.BlockSpec((tm, d), lambda i: (i, 0)),
        out_shape=jax.ShapeDtypeStruct((t, d), BF16),
        compiler_params=_cparams("parallel"),
        name="rmsnorm",
    )(x, g.reshape(1, d).astype(F32))


def _head_epilogue(acc, gain, rope):
    outs = []
    for h in range(acc.shape[1] // HEAD_DIM):
        a = acc[:, h * HEAD_DIM:(h + 1) * HEAD_DIM]
        y = a * lax.rsqrt(jnp.mean(a * a, axis=-1, keepdims=True) + EPS) * gain[:, h * HEAD_DIM:(h + 1) * HEAD_DIM]
        if rope is not None:
            c, s1, s2 = rope
            half = ROPE_DIM // 2
            y = y * c + pltpu.roll(y, half, 1) * s1 + pltpu.roll(y, HEAD_DIM - half, 1) * s2
        outs.append(y)
    return jnp.concatenate(outs, axis=1) if len(outs) > 1 else outs[0]


EPI_COLS = 256


def _mm_kernel(*refs, n_in, epi):
    xs, ws, rest = refs[:n_in], refs[n_in:2 * n_in], refs[2 * n_in:]
    o_ref = rest[-1]
    if epi in ("headnorm", "headnorm_rope"):
        rope = (rest[1][...], rest[2][...], rest[3][...]) if epi == "headnorm_rope" else None
        x = xs[0][...]
        for c0 in range(0, o_ref.shape[1], EPI_COLS):
            cs = slice(c0, c0 + EPI_COLS)
            o_ref[:, cs] = _head_epilogue(_dot(x, ws[0][:, cs]), rest[0][:, cs], rope).astype(o_ref.dtype)
        return
    acc = _dot(xs[0][...], ws[0][...])
    for x_ref, w_ref in zip(xs[1:], ws[1:]):
        acc = acc + _dot(x_ref[...], w_ref[...])
    if epi == "plain":
        out = acc
    elif epi == "residual":
        out = acc + rest[0][...]
    else:
        raise ValueError(epi)
    o_ref[...] = out.astype(o_ref.dtype)


def _matmul(xs, ws, out_dtype, epi="plain", extra=(), pos_blk=None, tm=1024, tn=512):
    t = xs[0].shape[0]
    n = ws[0].shape[1]
    tm, tn = min(tm, t), min(tn, n)
    in_specs = [pl.BlockSpec((tm, x.shape[1]), lambda i, j: (i, 0)) for x in xs]
    in_specs += [pl.BlockSpec((w.shape[0], tn), lambda i, j: (0, j)) for w in ws]
    if epi == "residual":
        in_specs.append(pl.BlockSpec((tm, tn), lambda i, j: (i, j)))
    elif epi in ("headnorm", "headnorm_rope"):
        in_specs.append(pl.BlockSpec((1, tn), lambda i, j: (0, j)))
        if epi == "headnorm_rope":
            in_specs += [pl.BlockSpec((tm, HEAD_DIM), lambda i, j: (pos_blk(i), 0))] * 3
    return pl.pallas_call(
        functools.partial(_mm_kernel, n_in=len(xs), epi=epi),
        grid=(t // tm, n // tn),
        in_specs=in_specs,
        out_specs=pl.BlockSpec((tm, tn), lambda i, j: (i, j)),
        out_shape=jax.ShapeDtypeStruct((t, n), out_dtype),
        compiler_params=_cparams("parallel", "parallel"),
        name="matmul_" + epi,
    )(*xs, *ws, *extra)


def _mlstm_kernel(qf_ref, kf_ref, vf_ref, gf_ref, qb_ref, kb_ref, vb_ref, gb_ref, bias_ref,
                  hf_ref, hb_ref, c_ref, n_ref, m_ref, *, nh, nchunks, start_chunks, last_chunks):
    c = pl.program_id(0)
    cb = nchunks - 1 - c
    ch = MLSTM_CHUNK
    is_start = functools.reduce(jnp.logical_or, [c == s for s in start_chunks])
    is_last = functools.reduce(jnp.logical_or, [cb == e for e in last_chunks])

    def _reset(d):
        c_ref[d] = jnp.zeros((nh, HEAD_DIM, HEAD_DIM), F32)
        n_ref[d] = jnp.zeros((nh, 1, HEAD_DIM), F32)
        m_ref[d] = jnp.zeros((nh, 1, HEAD_DIM), F32)

    pl.when(is_start)(lambda: _reset(0))
    pl.when(is_last)(lambda: _reset(1))

    row = lax.broadcasted_iota(jnp.int32, (ch, ch), 0)
    col = lax.broadcasted_iota(jnp.int32, (ch, ch), 1)
    scale = HEAD_DIM ** -0.5
    heads = lambda ref: jnp.stack([ref[:, h * HEAD_DIM:(h + 1) * HEAD_DIM] for h in range(nh)])
    bdot = lambda a, b, ca, cb: lax.dot_general(a, b, (((ca,), (cb,)), ((0,), (0,))), preferred_element_type=F32)

    for d, (q_ref, k_ref, v_ref, g_ref, h_ref) in enumerate(
            ((qf_ref, kf_ref, vf_ref, gf_ref, hf_ref), (qb_ref, kb_ref, vb_ref, gb_ref, hb_ref))):
        mask = (row >= col) if d == 0 else (col >= row)
        icol0 = 2 * nh * d
        fcol0 = icol0 + nh
        tot_row = ch - 1 if d == 0 else 0
        g = g_ref[...] + bias_ref[...]
        lf = jnp.minimum(g, 0.0) - jnp.log1p(jnp.exp(-jnp.abs(g)))
        tri = jnp.where(mask, 1.0, 0.0).astype(BF16)
        hi, mid, lo = _split3(lf)
        bcum = _dot(tri, hi) + _dot(tri, mid) + _dot(tri, lo)
        a = g - pltpu.roll(bcum, HEAD_DIM - nh, 1)
        a_t = a.T
        bcol = jnp.stack([bcum[:, fcol0 + h:fcol0 + h + 1] for h in range(nh)])
        acol = jnp.stack([a[:, icol0 + h:icol0 + h + 1] for h in range(nh)])
        arow = jnp.stack([a_t[icol0 + h:icol0 + h + 1, :] for h in range(nh)])
        b_tot = bcol[:, tot_row:tot_row + 1, :]
        m_st = m_ref[d][:, :, 0:1]
        c_prev = c_ref[d]
        n_prev = n_ref[d]

        log_d = jnp.where(mask[None], bcol + arow, NEG)
        m_intra = jnp.max(log_d, axis=2, keepdims=True)
        log_inter = bcol + m_st
        m_row = jnp.maximum(log_inter, m_intra)
        dmat = jnp.exp(log_d - m_row)
        q = heads(q_ref)
        ks = (heads(k_ref).astype(F32) * scale).astype(BF16)
        v = heads(v_ref)
        p = bdot(q, ks, 2, 2) * dmat
        w_inter = jnp.exp(log_inter - m_row)
        num = bdot(p.astype(BF16), v, 2, 1) + w_inter * bdot(q, c_prev.astype(BF16), 2, 1)
        den = (jnp.sum(p, axis=2, keepdims=True)
               + w_inter * jnp.sum(q.astype(F32) * n_prev, axis=2, keepdims=True))
        h_out = num / jnp.maximum(jnp.abs(den), jnp.exp(-m_row))
        for h in range(nh):
            h_ref[:, h * HEAD_DIM:(h + 1) * HEAD_DIM] = h_out[h]

        m_kv = b_tot + jnp.max(arow, axis=2, keepdims=True)
        m_new = jnp.maximum(b_tot + m_st, m_kv)
        decay = jnp.exp(b_tot + m_st - m_new)
        wexp = jnp.exp(b_tot + acol - m_new)
        kw = ks.astype(F32) * wexp
        c_ref[d] = decay * c_prev + bdot(kw.astype(BF16), v, 1, 1)
        n_ref[d] = decay * n_prev + jnp.sum(kw, axis=1, keepdims=True)
        m_ref[d] = jnp.broadcast_to(m_new, (nh, 1, HEAD_DIM))


def _mlstm(u_a, gates, gate_bias, seqs, nh):
    t = u_a.shape[0]
    ch = MLSTM_CHUNK
    nchunks = t // ch
    da = nh * HEAD_DIM
    start_chunks = [s // ch for s, _ in seqs]
    last_chunks = [(s + l) // ch - 1 for s, l in seqs]
    fwd = lambda j: pl.BlockSpec((ch, da), lambda c, j=j: (c, j))
    bwd = lambda j: pl.BlockSpec((ch, da), lambda c, j=j: (nchunks - 1 - c, j))
    gspec_f = pl.BlockSpec((ch, 128), lambda c: (c, 0))
    gspec_b = pl.BlockSpec((ch, 128), lambda c: (nchunks - 1 - c, 0))
    bias = jnp.zeros((1, 128), F32).at[0, :4 * nh].set(gate_bias.reshape(-1).astype(F32))
    return pl.pallas_call(
        functools.partial(_mlstm_kernel, nh=nh, nchunks=nchunks, start_chunks=start_chunks,
                          last_chunks=last_chunks),
        grid=(nchunks,),
        in_specs=[fwd(0), fwd(1), fwd(2), gspec_f, bwd(0), bwd(1), bwd(2), gspec_b,
                  pl.BlockSpec((1, 128), lambda c: (0, 0))],
        out_specs=[pl.BlockSpec((ch, da), lambda c: (c, 0)),
                   pl.BlockSpec((ch, da), lambda c: (nchunks - 1 - c, 0))],
        out_shape=[jax.ShapeDtypeStruct((t, da), F32)] * 2,
        scratch_shapes=[pltpu.VMEM((2, nh, HEAD_DIM, HEAD_DIM), F32),
                        pltpu.VMEM((2, nh, 1, HEAD_DIM), F32),
                        pltpu.VMEM((2, nh, 1, HEAD_DIM), F32)],
        compiler_params=_cparams("arbitrary"),
        name="mlstm",
    )(u_a, u_a, u_a, gates, u_a, u_a, u_a, gates, bias)


def _mlstm_gate_kernel(hf_ref, hb_ref, o_ref, g_ref, y_ref):
    hsum = hf_ref[...] + hb_ref[...]
    o = o_ref[...].astype(F32)
    g = g_ref[...]
    outs = []
    for h in range(hsum.shape[1] // HEAD_DIM):
        hs = slice(h * HEAD_DIM, (h + 1) * HEAD_DIM)
        a = hsum[:, hs]
        y = a * lax.rsqrt(jnp.mean(a * a, axis=-1, keepdims=True) + EPS) * g[:, hs]
        outs.append(jax.nn.sigmoid(o[:, hs]) * y)
    y_ref[...] = jnp.concatenate(outs, axis=1).astype(y_ref.dtype)


def _mlstm_gate(h_f, h_b, u_a, hnorm_g, tm=512):
    t, da = h_f.shape
    return pl.pallas_call(
        _mlstm_gate_kernel,
        grid=(t // tm,),
        in_specs=[pl.BlockSpec((tm, da), lambda i: (i, 0)), pl.BlockSpec((tm, da), lambda i: (i, 0)),
                  pl.BlockSpec((tm, da), lambda i: (i, 3)), pl.BlockSpec((1, da), lambda i: (0, 0))],
        out_specs=pl.BlockSpec((tm, da), lambda i: (i, 0)),
        out_shape=jax.ShapeDtypeStruct((t, da), BF16),
        compiler_params=_cparams("parallel"),
        name="mlstm_gate",
    )(h_f, h_b, u_a, hnorm_g.reshape(1, da).astype(F32))


NA_QROWS = 4
NA_KROWS = NA_QROWS + NA_ROWS
NA_UNROLL = 2


def _na_bias_table(rel_bias):
    h = rel_bias.shape[0]
    nr, nc = 2 * NA_ROWS - 1, 2 * NA_COLS - 1
    c = np.arange(GRID_W)
    dc = np.clip(c[None, :] - c[:, None], -(NA_COLS - 1), NA_COLS - 1) + NA_COLS - 1
    c_start = np.clip(c - NA_COLS // 2, 0, GRID_W - NA_COLS)
    col_ok = (c[None, :] >= c_start[:, None]) & (c[None, :] < c_start[:, None] + NA_COLS)
    var = np.arange(3)[:, None, None]
    qi = np.arange(NA_QROWS)[None, :, None]
    kr = np.arange(NA_KROWS)[None, None, :]
    dr = kr - var * NA_QROWS - qi
    first = np.where(var == 0, 0, np.where(var == 1, qi, NA_QROWS))
    row_ok = (kr >= first) & (kr < first + NA_ROWS)
    dr_idx = np.clip(dr + NA_ROWS - 1, 0, nr - 1).reshape(-1)
    onehot = (dc.reshape(1, -1) == np.arange(nc)[:, None]).astype(np.float32)
    cols = jnp.dot(rel_bias.astype(F32).reshape(h * nr, nc), onehot, precision=lax.Precision.HIGHEST)
    cols = cols.reshape(h, nr, GRID_W, GRID_W)
    tbl = jnp.stack([cols[:, int(i)] for i in dr_idx], axis=1)
    tbl = tbl.reshape(h, 3, NA_QROWS, NA_KROWS, GRID_W, GRID_W)
    ok = row_ok[:, :, :, None, None] & col_ok[None, None, None, :, :]
    tbl = jnp.where(ok[None], tbl, NEG)
    return jnp.transpose(tbl, (0, 1, 2, 4, 3, 5)).reshape(h, 3, NA_QROWS * GRID_W, NA_KROWS * GRID_W)


def _na_kernel(q_ref, kp_ref, km_ref, kn_ref, vp_ref, vm_ref, vn_ref, bias_ref, o_ref, kcat, vcat,
               *, seg_rows, row_lo, row_hi):
    s = pl.program_id(1)
    halo = NA_ROWS * GRID_W
    seg = seg_rows * GRID_W
    kcat[0:halo] = kp_ref[...]
    kcat[halo:halo + seg] = km_ref[...]
    kcat[halo + seg:halo + seg + halo] = kn_ref[...]
    vcat[0:halo] = vp_ref[...]
    vcat[halo:halo + seg] = vm_ref[...]
    vcat[halo + seg:halo + seg + halo] = vn_ref[...]
    r_lo = _seq_lookup(s, row_lo, 0)
    r_hi = _seq_lookup(s, row_hi, 0)
    scale = HEAD_DIM ** -0.5
    nq = NA_QROWS * GRID_W
    nk = NA_KROWS * GRID_W

    def one_group(gl):
        r0 = s * seg_rows + gl * NA_QROWS
        u = jnp.clip(r0 - NA_ROWS // 2, r_lo, r_hi - NA_KROWS)
        off = pl.multiple_of((u - s * seg_rows + NA_ROWS) * GRID_W, GRID_W)
        qoff = pl.multiple_of(gl * nq, nq)
        q = q_ref[pl.ds(qoff, nq), :]
        sc = _dot_nt(q, kcat[pl.ds(off, nk), :]) * scale + bias_ref[(r0 - u) // NA_QROWS]
        e = jnp.exp(sc - jnp.max(sc, axis=1, keepdims=True))
        l = jnp.sum(e, axis=1, keepdims=True)
        o_ref[pl.ds(qoff, nq), :] = (_dot(e.astype(BF16), vcat[pl.ds(off, nk), :]) / l).astype(o_ref.dtype)

    def body(it, carry):
        for k in range(NA_UNROLL):
            one_group(it * NA_UNROLL + k)
        return carry

    lax.fori_loop(0, seg_rows // (NA_QROWS * NA_UNROLL), body, 0)


def _neighbourhood_attention(qk, v, bias_tbl, seqs, seg):
    t, dh = v.shape
    nh = dh // HEAD_DIM
    nseg = t // seg
    seg_rows = seg // GRID_W
    assert all(l // GRID_W >= NA_KROWS and (s // GRID_W) % NA_QROWS == 0 and (l // GRID_W) % NA_QROWS == 0
               for s, l in seqs)
    halo = NA_ROWS * GRID_W
    hpb = seg // halo
    nhb = t // halo
    row_lo = [(s // seg, s // GRID_W) for s, _ in seqs]
    row_hi = [(s // seg, (s + l) // GRID_W) for s, l in seqs]
    main = lambda c0: pl.BlockSpec((seg, HEAD_DIM), lambda h, s: (s, c0 + h))
    prev = lambda c0: pl.BlockSpec((halo, HEAD_DIM), lambda h, s: (jnp.maximum(s * hpb - 1, 0), c0 + h))
    nxt = lambda c0: pl.BlockSpec((halo, HEAD_DIM), lambda h, s: (jnp.minimum((s + 1) * hpb, nhb - 1), c0 + h))
    return pl.pallas_call(
        functools.partial(_na_kernel, seg_rows=seg_rows, row_lo=row_lo, row_hi=row_hi),
        grid=(nh, nseg),
        in_specs=[main(0), prev(nh), main(nh), nxt(nh), prev(0), main(0), nxt(0),
                  pl.BlockSpec((None, 3, NA_QROWS * GRID_W, NA_KROWS * GRID_W), lambda h, s: (h, 0, 0, 0))],
        out_specs=main(0),
        out_shape=jax.ShapeDtypeStruct((t, dh), BF16),
        scratch_shapes=[pltpu.VMEM((seg + 2 * halo, HEAD_DIM), BF16)] * 2,
        compiler_params=_cparams("parallel", "parallel"),
        name="natten",
    )(qk, qk, qk, qk, v, v, v, bias_tbl)


DIL_BLK = 128
DIL_RADIUS = 64
DIL_MAX = max(d for _, d in DIL_PAIRS)
DIL_QB = DIL_BLK * DIL_MAX
DIL_HALO = DIL_RADIUS * DIL_MAX


def _dilated_kernel(q_ref, kp_ref, kc_ref, kn_ref, vp_ref, vc_ref, vn_ref, y_ref, num_s, mx_s, den_s,
                    *, first_blocks, last_blocks):
    i = pl.program_id(1)
    has_prev = jnp.logical_not(functools.reduce(jnp.logical_or, [i == b for b in first_blocks]))
    has_next = jnp.logical_not(functools.reduce(jnp.logical_or, [i == b for b in last_blocks]))
    nk = DIL_BLK + 2 * DIL_RADIUS
    j_io = lax.broadcasted_iota(jnp.int32, (DIL_BLK, nk), 0)
    s_io = lax.broadcasted_iota(jnp.int32, (DIL_BLK, nk), 1)
    band = jnp.abs(s_io - DIL_RADIUS - j_io) <= DIL_RADIUS
    ok_prev = jnp.logical_or(s_io >= DIL_RADIUS, has_prev)
    ok_next = jnp.logical_or(s_io < DIL_BLK + DIL_RADIUS, has_next)
    madd = {}
    for up in (False, True):
        for un in (False, True):
            ok = band
            ok = jnp.logical_and(ok, ok_prev) if up else ok
            ok = jnp.logical_and(ok, ok_next) if un else ok
            madd[up, un] = jnp.where(ok, 0.0, NEG)
    scale = HEAD_DIM ** -0.5

    def rows(start, size, d):
        return pl.ds(start, size, stride=d) if d > 1 else pl.ds(start, size)

    for b, (_, d) in enumerate(DIL_PAIRS):
        per_class = DIL_QB // d
        nj = per_class // DIL_BLK
        for g in range(d):
            for j in range(nj):
                up, un = j == 0, j == nj - 1
                qrows = rows(g + d * DIL_BLK * j, DIL_BLK, d)
                m0 = max(DIL_BLK * j - DIL_RADIUS, 0)
                m1 = min(DIL_BLK * (j + 1) + DIL_RADIUS, per_class)
                kparts, vparts = [], []
                if up:
                    r = rows(DIL_HALO - DIL_RADIUS * d + g, DIL_RADIUS, d)
                    kparts.append(kp_ref[r, :])
                    vparts.append(vp_ref[r, :])
                r = rows(g + d * m0, m1 - m0, d)
                kparts.append(kc_ref[r, :])
                vparts.append(vc_ref[r, :])
                if un:
                    r = rows(g, DIL_RADIUS, d)
                    kparts.append(kn_ref[r, :])
                    vparts.append(vn_ref[r, :])
                kk = jnp.concatenate(kparts, axis=0).astype(BF16)
                vv = jnp.concatenate(vparts, axis=0).astype(BF16)
                sc = _dot_nt(q_ref[qrows, :].astype(BF16), kk) * scale + madd[up, un]
                mx = jnp.max(sc, axis=1, keepdims=True)
                e = jnp.exp(sc - mx)
                den = jnp.sum(e, axis=1, keepdims=True)
                num_s[b, qrows, :] = _dot(e.astype(BF16), vv)
                mx_s[b, qrows, :] = jnp.broadcast_to(mx, (DIL_BLK, HEAD_DIM))
                den_s[b, qrows, :] = jnp.broadcast_to(den, (DIL_BLK, HEAD_DIM))

    nb = len(DIL_PAIRS)
    mxs = [mx_s[b] for b in range(nb)]
    m_top = functools.reduce(jnp.maximum, mxs)
    w = [jnp.exp(m - m_top) for m in mxs]
    num = functools.reduce(jnp.add, [w[b] * num_s[b] for b in range(nb)])
    den = functools.reduce(jnp.add, [w[b] * den_s[b] for b in range(nb)])
    y_ref[...] = (num / den).astype(y_ref.dtype)


def _dilated_attention(qk, v, seqs):
    t, dh = v.shape
    nh = dh // HEAD_DIM
    assert all(window // (2 * dil) == DIL_RADIUS for window, dil in DIL_PAIRS)
    assert all(s % DIL_QB == 0 and l % DIL_QB == 0 for s, l in seqs)
    nq = t // DIL_QB
    hb = DIL_QB // DIL_HALO
    nhalo = t // DIL_HALO
    first_blocks = [s // DIL_QB for s, _ in seqs]
    last_blocks = [(s + l) // DIL_QB - 1 for s, l in seqs]
    cur = lambda c0: pl.BlockSpec((DIL_QB, HEAD_DIM), lambda h, i: (i, c0 + h))
    prev = lambda c0: pl.BlockSpec((DIL_HALO, HEAD_DIM), lambda h, i: (jnp.maximum(i * hb - 1, 0), c0 + h))
    nxt = lambda c0: pl.BlockSpec((DIL_HALO, HEAD_DIM), lambda h, i: (jnp.minimum((i + 1) * hb, nhalo - 1), c0 + h))
    nb = len(DIL_PAIRS)
    return pl.pallas_call(
        functools.partial(_dilated_kernel, first_blocks=first_blocks, last_blocks=last_blocks),
        grid=(nh, nq),
        in_specs=[cur(0), prev(nh), cur(nh), nxt(nh), prev(0), cur(0), nxt(0)],
        out_specs=cur(0),
        out_shape=jax.ShapeDtypeStruct((t, dh), BF16),
        scratch_shapes=[pltpu.VMEM((nb, DIL_QB, HEAD_DIM), F32)] * 3,
        compiler_params=_cparams("parallel", "parallel"),
        name="dilated",
    )(qk, qk, qk, qk, v, v, v)


def _rope_tables(max_len):
    half = ROPE_DIM // 2
    inv_freq = ROPE_THETA ** (-jnp.arange(half, dtype=F32) / half)
    ang = jnp.arange(max_len, dtype=F32)[:, None] * inv_freq[None, :]
    cos, sin = jnp.cos(ang), jnp.sin(ang)
    ones = jnp.ones((max_len, HEAD_DIM - ROPE_DIM), F32)
    zeros = jnp.zeros((max_len, HEAD_DIM - half), F32)
    c = jnp.concatenate([cos, cos, ones], axis=1)
    s1 = jnp.concatenate([jnp.zeros((max_len, half), F32), sin, zeros[:, :HEAD_DIM - ROPE_DIM]], axis=1)
    s2 = jnp.concatenate([-sin, zeros], axis=1)
    return c, s1, s2


def _router_kernel(x_ref, g_ref, wr_ref, rb_ref, xn_ref, e_ref, gt_ref, *, n_exp):
    x = x_ref[...]
    xn = x * lax.rsqrt(jnp.mean(x * x, axis=-1, keepdims=True) + EPS) * g_ref[...]
    xn_ref[...] = xn.astype(xn_ref.dtype)
    tm = x.shape[0]
    epg = n_exp // N_GROUPS
    xh, xm, _ = _split3(xn)
    wh, wm, _ = _split3(wr_ref[...])
    logits = _dot_nt(wh, xh) + _dot_nt(wh, xm) + _dot_nt(wm, xh)
    scores = jax.nn.sigmoid(logits)
    sel = (scores + rb_ref[:, 0:1]).reshape(N_GROUPS, epg, tm)
    sc3 = scores.reshape(N_GROUPS, epg, tm)
    io = lax.broadcasted_iota(jnp.int32, (N_GROUPS, epg, tm), 1)
    gio = lax.broadcasted_iota(jnp.int32, (N_GROUPS, 1, tm), 0)
    m1 = jnp.max(sel, axis=1, keepdims=True)
    i1 = jnp.min(jnp.where(sel == m1, io, epg), axis=1, keepdims=True)
    sel_b = jnp.where(io == i1, -jnp.inf, sel)
    m2 = jnp.max(sel_b, axis=1, keepdims=True)
    i2 = jnp.min(jnp.where(sel_b == m2, io, epg), axis=1, keepdims=True)
    gs = m1 + m2
    gbest = jnp.min(jnp.where(gs == jnp.max(gs, axis=0, keepdims=True), gio, N_GROUPS), axis=0, keepdims=True)
    in_g = gio == gbest
    outs_e, outs_w = [], []
    for ik in (i1, i2):
        pick = in_g & (io == ik)
        outs_e.append(jnp.sum(jnp.where(in_g, gbest * epg + ik, 0), axis=0))
        outs_w.append(jnp.sum(jnp.sum(jnp.where(pick, sc3, 0.0), axis=1, keepdims=True), axis=0))
    wsum = outs_w[0] + outs_w[1]
    e_ref[0:1, :] = outs_e[0]
    e_ref[1:2, :] = outs_e[1]
    gt_ref[0:1, :] = outs_w[0] / wsum
    gt_ref[1:2, :] = outs_w[1] / wsum


def _router(x, g, w_router, router_bias, tm=512):
    t, d = x.shape
    n_exp = w_router.shape[1]
    rb = jnp.broadcast_to(router_bias.astype(F32)[:, None], (n_exp, 128))
    return pl.pallas_call(
        functools.partial(_router_kernel, n_exp=n_exp),
        grid=(t // tm,),
        in_specs=[pl.BlockSpec((tm, d), lambda i: (i, 0)), pl.BlockSpec((1, d), lambda i: (0, 0)),
                  pl.BlockSpec((n_exp, d), lambda i: (0, 0)), pl.BlockSpec((n_exp, 128), lambda i: (0, 0))],
        out_specs=[pl.BlockSpec((tm, d), lambda i: (i, 0)), pl.BlockSpec((2, tm), lambda i: (0, i)),
                   pl.BlockSpec((2, tm), lambda i: (0, i))],
        out_shape=[jax.ShapeDtypeStruct((t, d), F32), jax.ShapeDtypeStruct((2, t), jnp.int32),
                   jax.ShapeDtypeStruct((2, t), F32)],
        compiler_params=_cparams("parallel"),
        name="router",
    )(x, g.reshape(1, d).astype(F32), w_router.T.astype(F32), rb)


def _rank_kernel(e_ref, rank_ref, cnt_ref, run_ref, *, n_exp):
    i = pl.program_id(0)
    tm = e_ref.shape[1]

    @pl.when(i == 0)
    def _():
        run_ref[...] = jnp.zeros_like(run_ref)

    eio = lax.broadcasted_iota(jnp.int32, (n_exp, tm), 0)
    oh = [(eio == e_ref[k:k + 1, :]) for k in range(TOP_K)]
    ohf = [jnp.where(o, 1.0, 0.0) for o in oh]
    both = ohf[0] + ohf[1]
    r_i = lax.broadcasted_iota(jnp.int32, (tm, tm), 0)
    c_i = lax.broadcasted_iota(jnp.int32, (tm, tm), 1)
    upper = jnp.where(r_i <= c_i, 1.0, 0.0).astype(BF16)
    cum = _dot(both.astype(BF16), upper)
    base = run_ref[:, 0:1] + cum - both
    for k in range(TOP_K):
        rank_ref[k:k + 1, :] = jnp.sum(jnp.where(oh[k], base, 0.0), axis=0, keepdims=True).astype(jnp.int32)
    run_new = run_ref[...] + cum[:, tm - 1:tm]
    run_ref[...] = run_new
    cnt_ref[...] = run_new.astype(jnp.int32)


def _rank(e, n_exp, tm=512):
    t = e.shape[1]
    return pl.pallas_call(
        functools.partial(_rank_kernel, n_exp=n_exp),
        grid=(t // tm,),
        in_specs=[pl.BlockSpec((2, tm), lambda i: (0, i))],
        out_specs=[pl.BlockSpec((2, tm), lambda i: (0, i)), pl.BlockSpec((n_exp, 128), lambda i: (0, 0))],
        out_shape=[jax.ShapeDtypeStruct((2, t), jnp.int32), jax.ShapeDtypeStruct((n_exp, 128), jnp.int32)],
        scratch_shapes=[pltpu.VMEM((n_exp, 128), F32)],
        compiler_params=_cparams("arbitrary"),
        name="moe_rank",
    )(e)


MOE_BLK = 256


CAST_ROWS = 256


ROW_UNROLL = 8


def _start_rows(n_rows, make_copy):
    def body(r, c):
        for k in range(TOP_K):
            make_copy(k, r).start(priority=k)
        return c
    lax.fori_loop(0, n_rows, body, 0, unroll=ROW_UNROLL)


def _wait_rows(n_rows, make_copy):
    def body(r, c):
        for k in range(TOP_K):
            make_copy(k, 0).wait()
        return c
    lax.fori_loop(0, n_rows, body, 0, unroll=ROW_UNROLL)


def _dispatch_kernel(fill_ref, ps_ref, e_ref, rank_ref, xn_ref, xs_out, zbuf, sem, zsem, *, tm, n_fill):
    i = pl.program_id(0)

    @pl.when(i == 0)
    def _():
        zbuf[...] = jnp.zeros_like(zbuf)

        def fill(j):
            rows = pl.ds(pl.multiple_of(fill_ref[j] * MOE_BLK, MOE_BLK), MOE_BLK)
            return pltpu.make_async_copy(zbuf, xs_out.at[rows, :], zsem)

        def start(j, c):
            pl.when(fill_ref[n_fill + j] > 0)(lambda: fill(j).start())
            return c

        def wait(j, c):
            pl.when(fill_ref[n_fill + j] > 0)(lambda: fill(j).wait())
            return c

        lax.fori_loop(0, n_fill, start, 0)
        lax.fori_loop(0, n_fill, wait, 0)

    def copy(k, r):
        slot = ps_ref[e_ref[k, r]] + rank_ref[k, r]
        return pltpu.make_async_copy(xn_ref.at[pl.ds(r, 1), :], xs_out.at[pl.ds(slot, 1), :], sem)

    _start_rows(tm, copy)
    _wait_rows(tm, copy)


def _dispatch(e, rank, pad_start, xn, fill, cap, tm=256):
    t, d = xn.shape
    smem = pl.BlockSpec((TOP_K, tm), lambda i, f, ps: (0, i), memory_space=pltpu.SMEM)
    return pl.pallas_call(
        functools.partial(_dispatch_kernel, tm=tm, n_fill=fill.shape[0] // 2),
        grid_spec=pltpu.PrefetchScalarGridSpec(
            num_scalar_prefetch=2, grid=(t // tm,),
            in_specs=[smem, smem, pl.BlockSpec((tm, d), lambda i, f, ps: (i, 0))],
            out_specs=pl.BlockSpec(memory_space=pl.ANY),
            scratch_shapes=[pltpu.VMEM((MOE_BLK, d), F32), pltpu.SemaphoreType.DMA, pltpu.SemaphoreType.DMA]),
        out_shape=jax.ShapeDtypeStruct((cap, d), F32),
        compiler_params=_cparams("arbitrary"),
        name="moe_dispatch",
    )(fill, pad_start, e, rank, xn)


def _cast_resident(w_ref, s_ref):
    step = min(CAST_ROWS, w_ref.shape[0])

    def body(i, c):
        rows = pl.ds(pl.multiple_of(i * step, step), step)
        s_ref[rows, :] = w_ref[rows, :].astype(s_ref.dtype)
        return c
    lax.fori_loop(0, w_ref.shape[0] // step, body, 0)


def _expert_changed(b, be_ref):
    return jnp.logical_or(b == 0, be_ref[b] != be_ref[jnp.maximum(b - 1, 0)])


def _ffn_up_kernel(be_ref, nb_ref, x_ref, wg_ref, wu_ref, h_ref, wg_s, wu_s):
    b = pl.program_id(0)

    @pl.when(_expert_changed(b, be_ref))
    def _():
        _cast_resident(wg_ref, wg_s)
        _cast_resident(wu_ref, wu_s)

    @pl.when(b < nb_ref[0])
    def _():
        x = x_ref[...].astype(BF16)
        hg = _dot(x, wg_s[...])
        hu = _dot(x, wu_s[...])
        h_ref[...] = ((hg * jax.nn.sigmoid(hg)) * hu).astype(h_ref.dtype)

    @pl.when(b >= nb_ref[0])
    def _():
        h_ref[...] = jnp.zeros_like(h_ref)


def _ffn_down_kernel(be_ref, nb_ref, h_ref, wd_ref, y_ref, wd_s):
    b = pl.program_id(0)
    pl.when(_expert_changed(b, be_ref))(lambda: _cast_resident(wd_ref, wd_s))

    @pl.when(b < nb_ref[0])
    def _():
        y_ref[...] = _dot(h_ref[...], wd_s[...])

    @pl.when(b >= nb_ref[0])
    def _():
        y_ref[...] = jnp.zeros_like(y_ref)


def _expert_ffn(xs, blk_expert, n_used, layer, w_gate, w_up, w_down):
    cap, d = xs.shape
    f = w_gate.shape[3]
    nblk = cap // MOE_BLK
    used = lambda b, be, nb: (jnp.minimum(b, nb[0] - 1), 0)
    wsel = lambda b, be, nb: (layer, be[b], 0, 0)
    h = pl.pallas_call(
        _ffn_up_kernel,
        grid_spec=pltpu.PrefetchScalarGridSpec(
            num_scalar_prefetch=2, grid=(nblk,),
            in_specs=[pl.BlockSpec((MOE_BLK, d), used), pl.BlockSpec((None, None, d, f), wsel),
                      pl.BlockSpec((None, None, d, f), wsel)],
            out_specs=pl.BlockSpec((MOE_BLK, f), lambda b, be, nb: (b, 0)),
            scratch_shapes=[pltpu.VMEM((d, f), BF16)] * 2),
        out_shape=jax.ShapeDtypeStruct((cap, f), BF16),
        compiler_params=_cparams("arbitrary"),
        name="moe_ffn_up",
    )(blk_expert, n_used, xs, w_gate, w_up)
    return pl.pallas_call(
        _ffn_down_kernel,
        grid_spec=pltpu.PrefetchScalarGridSpec(
            num_scalar_prefetch=2, grid=(nblk,),
            in_specs=[pl.BlockSpec((MOE_BLK, f), used), pl.BlockSpec((None, None, f, d), wsel)],
            out_specs=pl.BlockSpec((MOE_BLK, d), lambda b, be, nb: (b, 0)),
            scratch_shapes=[pltpu.VMEM((f, d), BF16)]),
        out_shape=jax.ShapeDtypeStruct((cap, d), F32),
        compiler_params=_cparams("arbitrary"),
        name="moe_ffn_down",
    )(blk_expert, n_used, h, w_down)


def _combine_kernel(ps_ref, e_ref, rank_ref, e_nx, rank_nx, x_ref, gt_ref, ys_hbm, o_ref, buf, sem, *, tm, nsteps):
    i = pl.program_id(0)
    cur = lax.rem(i, 2)

    def gather(eref, rref, b):
        def copy(k, r):
            slot = ps_ref[eref[k, r]] + rref[k, r]
            return pltpu.make_async_copy(ys_hbm.at[pl.ds(slot, 1), :], buf.at[b, k, pl.ds(r, 1), :], sem.at[b])
        return copy

    pl.when(i == 0)(lambda: _start_rows(tm, gather(e_ref, rank_ref, 0)))
    pl.when(i + 1 < nsteps)(lambda: _start_rows(tm, gather(e_nx, rank_nx, 1 - cur)))
    _wait_rows(tm, gather(e_ref, rank_ref, cur))
    o_ref[...] = x_ref[...] + (gt_ref[:, 0:1] * buf[cur, 0] + gt_ref[:, 1:2] * buf[cur, 1])


def _combine(e, rank, pad_start, x, gt_cols, ys, tm=256):
    t, d = x.shape
    nsteps = t // tm
    smem = lambda nxt: pl.BlockSpec((TOP_K, tm), lambda i, ps: (0, jnp.minimum(i + nxt, nsteps - 1)),
                                    memory_space=pltpu.SMEM)
    return pl.pallas_call(
        functools.partial(_combine_kernel, tm=tm, nsteps=nsteps),
        grid_spec=pltpu.PrefetchScalarGridSpec(
            num_scalar_prefetch=1, grid=(nsteps,),
            in_specs=[smem(0), smem(0), smem(1), smem(1),
                      pl.BlockSpec((tm, d), lambda i, ps: (i, 0)), pl.BlockSpec((tm, TOP_K), lambda i, ps: (i, 0)),
                      pl.BlockSpec(memory_space=pl.ANY)],
            out_specs=pl.BlockSpec((tm, d), lambda i, ps: (i, 0)),
            scratch_shapes=[pltpu.VMEM((2, TOP_K, tm, d), F32), pltpu.SemaphoreType.DMA((2,))]),
        out_shape=jax.ShapeDtypeStruct((t, d), F32),
        compiler_params=_cparams("arbitrary"),
        name="moe_combine",
    )(pad_start, e, rank, e, rank, x, gt_cols, ys)


def _moe(x, ln_g, w_router, router_bias, layer, w_gate, w_up, w_down):
    t, d = x.shape
    n_exp = w_router.shape[1]
    xn, e, gt = _router(x, ln_g, w_router, router_bias)
    rank, cnt = _rank(e, n_exp)
    counts = cnt[:, 0]
    padded = (counts + MOE_BLK - 1) // MOE_BLK * MOE_BLK
    pad_end = jnp.cumsum(padded)
    pad_start = (pad_end - padded).astype(jnp.int32)
    cap = t * TOP_K + n_exp * MOE_BLK
    nblk = cap // MOE_BLK
    blk_first = jnp.arange(nblk, dtype=jnp.int32) * MOE_BLK
    blk_expert = jnp.minimum(jnp.sum(pad_end[None, :] <= blk_first[:, None], axis=1), n_exp - 1).astype(jnp.int32)
    n_used = (pad_end[-1] // MOE_BLK).astype(jnp.int32).reshape(1)
    trail = n_used[0] + jnp.arange(n_exp, dtype=jnp.int32)
    fill_blk = jnp.concatenate([pad_end // MOE_BLK - 1, jnp.minimum(trail, nblk - 1)])
    fill_ok = jnp.concatenate([padded > 0, trail < nblk])
    fill = jnp.concatenate([jnp.maximum(fill_blk, 0), fill_ok.astype(jnp.int32)]).astype(jnp.int32)
    xs = _dispatch(e, rank, pad_start, xn, fill, cap)
    ys = _expert_ffn(xs, blk_expert, n_used, layer, w_gate, w_up, w_down)
    return _combine(e, rank, pad_start, x, gt.T, ys)


def _ab_layer(x, seqs, seg, ln_g, w_in, gate_bias, hnorm_g, qnorm_g, knorm_g, rel_bias, w_out):
    nh_a = gate_bias.shape[1]
    nh_b = rel_bias.shape[0]
    d_a, d_b = nh_a * HEAD_DIM, nh_b * HEAD_DIM
    xn = _rmsnorm(x, ln_g)
    w = w_in.astype(BF16)
    g0 = 4 * d_a
    g1 = g0 + 4 * nh_a
    u_a = _matmul([xn], [w[:, :g0]], BF16)
    w_g = jnp.zeros((w.shape[0], 128), BF16).at[:, :4 * nh_a].set(w[:, g0:g1])
    gates = _matmul([xn], [w_g], F32)
    gain = jnp.concatenate([jnp.tile(qnorm_g, nh_b), jnp.tile(knorm_g, nh_b)]).reshape(1, -1).astype(F32)
    qk_b = _matmul([xn], [w[:, g1:g1 + 2 * d_b]], BF16, epi="headnorm", extra=(gain,), tn=1024)
    v_b = _matmul([xn], [w[:, g1 + 2 * d_b:]], BF16)
    h_f, h_b = _mlstm(u_a, gates, gate_bias, seqs, nh_a)
    y_a = _mlstm_gate(h_f, h_b, u_a, hnorm_g)
    y_b = _neighbourhood_attention(qk_b, v_b, _na_bias_table(rel_bias), seqs, seg)
    wo = w_out.astype(BF16)
    return _matmul([y_a, y_b], [wo[:d_a], wo[d_a:]], F32, epi="residual", extra=(x,))


def _dilated_layer(x, seqs, ln_g, w_in, qnorm_g, knorm_g, w_out):
    t = x.shape[0]
    d_c = w_in.shape[1] // 3
    nh = d_c // HEAD_DIM
    xn = _rmsnorm(x, ln_g)
    w = w_in.astype(BF16)
    tm = 1024
    max_len = max(l for _, l in seqs)
    tables = _rope_tables(max_len)
    pos_tbl = [(s // tm, s // tm) for s, _ in seqs]
    pos_blk = lambda i: i - _seq_lookup(i, pos_tbl, 0)
    gain = jnp.concatenate([jnp.tile(qnorm_g, nh), jnp.tile(knorm_g, nh)]).reshape(1, -1).astype(F32)
    qk = _matmul([xn], [w[:, :2 * d_c]], F32, epi="headnorm_rope", extra=(gain,) + tables,
                 pos_blk=pos_blk, tm=tm, tn=1024)
    v = _matmul([xn], [w[:, 2 * d_c:]], F32, tm=tm)
    y = _dilated_attention(qk, v, seqs)
    return _matmul([y], [w_out.astype(BF16)], F32, epi="residual", extra=(x,))


def kernel(x_prompt, x_sample, ab_ln, ab_w_in, ab_gate_bias, ab_hnorm, ab_qnorm, ab_knorm, ab_relbias, ab_w_out,
           c_ln, c_w_in, c_qnorm, c_knorm, c_w_out, ffn_ln, w_router, router_bias, w_gate, w_up, w_down):
    bp, lp, d = x_prompt.shape
    bs, ls, _ = x_sample.shape
    tp = bp * lp
    seqs = [(b * lp, lp) for b in range(bp)] + [(tp + b * ls, ls) for b in range(bs)]
    seg = math.gcd(lp, ls)
    x = jnp.concatenate([x_prompt.reshape(tp, d), x_sample.reshape(bs * ls, d)], axis=0)
    depth = ffn_ln.shape[0]
    for layer in range(depth):
        j = layer // 2
        if layer % 2 == 0:
            x = _ab_layer(x, seqs, seg, ab_ln[j], ab_w_in[j], ab_gate_bias[j], ab_hnorm[j].reshape(-1),
                          ab_qnorm[j], ab_knorm[j], ab_relbias[j], ab_w_out[j])
        else:
            x = _dilated_layer(x, seqs, c_ln[j], c_w_in[j], c_qnorm[j], c_knorm[j], c_w_out[j])
        x = _moe(x, ffn_ln[layer], w_router, router_bias, layer, w_gate, w_up, w_down)
    return x[:tp].reshape(bp, lp, d), x[tp:].reshape(bs, ls, d)
```

```python
import functools
import math

import jax
import jax.numpy as jnp
import numpy as np
from jax import lax
from jax.experimental import pallas as pl
from jax.experimental.pallas import tpu as pltpu

HEAD_DIM = 128
MLSTM_CHUNK = 128
GRID_W = 64
NA_ROWS = 8
NA_COLS = 16
DIL_PAIRS = ((128, 1), (512, 4), (2048, 16))
ROPE_THETA = 500000.0
ROPE_DIM = HEAD_DIM // 4
N_GROUPS = 4
TOP_K = 2
EPS = 1e-6
NEG = -1e30
VMEM_LIMIT_BYTES = 56 * 1024 * 1024

F32 = jnp.float32
BF16 = jnp.bfloat16


def _cparams(*sem):
    return pltpu.CompilerParams(dimension_semantics=sem, vmem_limit_bytes=VMEM_LIMIT_BYTES)


def _dot(a, b):
    return jnp.dot(a, b, preferred_element_type=F32)


def _dot_nt(a, b):
    return lax.dot_general(a, b, (((1,), (1,)), ((), ())), preferred_element_type=F32)


def _dot_tn(a, b):
    return lax.dot_general(a, b, (((0,), (0,)), ((), ())), preferred_element_type=F32)


def _split3(x):
    hi = x.astype(BF16)
    r1 = x - hi.astype(F32)
    mid = r1.astype(BF16)
    lo = (r1 - mid.astype(F32)).astype(BF16)
    return hi, mid, lo


def _seq_lookup(u, table, default):
    out = default
    for thr, val in table:
        out = jnp.where(u >= thr, val, out)
    return out


def _rmsnorm_kernel(*refs, starts):
    g_ref, o_ref = refs[-2:]
    x = _pick_part(refs[:-2], starts, pl.program_id(0))
    ms = jnp.mean(x * x, axis=-1, keepdims=True)
    o_ref[...] = (x * lax.rsqrt(ms + EPS) * g_ref[...]).astype(o_ref.dtype)


def _rmsnorm(parts, g, tm=512):
    d = parts[0].shape[1]
    t = sum(p.shape[0] for p in parts)
    specs, starts = _part_specs(parts, tm, d, lambda: 0)
    return pl.pallas_call(
        functools.partial(_rmsnorm_kernel, starts=starts),
        grid=(t // tm,),
        in_specs=specs + [pl.BlockSpec((1, d), lambda i: (0, 0))],
        out_specs=pl.BlockSpec((tm, d), lambda i: (i, 0)),
        out_shape=jax.ShapeDtypeStruct((t, d), BF16),
        compiler_params=_cparams("parallel"),
        name="rmsnorm",
    )(*parts, g.reshape(1, d).astype(F32))


EPI_COLS = 256


def _head_mats(n):
    k = lax.broadcasted_iota(jnp.int32, (n, n), 0)
    i = lax.broadcasted_iota(jnp.int32, (n, n), 1)
    shift = HEAD_DIM.bit_length() - 1
    same = jnp.right_shift(k, shift) == jnp.right_shift(i, shift)
    kl, il = jnp.bitwise_and(k, HEAD_DIM - 1), jnp.bitwise_and(i, HEAD_DIM - 1)
    half = ROPE_DIM // 2
    partner = jnp.where(il < half, il + half, jnp.where(il < 2 * half, il - half, -1))
    ones = jnp.where(same, 1.0, 0.0).astype(BF16)
    swap = jnp.where(jnp.logical_and(same, kl == partner), 1.0, 0.0).astype(BF16)
    return ones, swap


def _head_epilogue(acc, gain, rope, mats):
    ones, swap = mats
    y = acc * lax.rsqrt(_dot((acc * acc).astype(BF16), ones) * (1.0 / HEAD_DIM) + EPS) * gain
    if rope is not None:
        reps = acc.shape[1] // HEAD_DIM
        c, s = (jnp.concatenate([t] * reps, axis=1) for t in rope)
        y = y * c + _dot(y.astype(BF16), swap) * s
    return y


def _pick_part(refs, starts, i):
    val = refs[0][...]
    for ref, st in zip(refs[1:], starts[1:]):
        val = jnp.where(i >= st, ref[...], val)
    return val


def _mm_kernel(*refs, n_in, epi, res_starts):
    xs, ws, rest = refs[:n_in], refs[n_in:2 * n_in], refs[2 * n_in:]
    o_ref = rest[-1]
    if epi in ("headnorm", "headnorm_rope"):
        rope = (rest[1][...], rest[2][...]) if epi == "headnorm_rope" else None
        mats = _head_mats(EPI_COLS)
        x = xs[0][...]
        for c0 in range(0, o_ref.shape[1], 2 * EPI_COLS):
            acc = _dot(x, ws[0][:, c0:c0 + 2 * EPI_COLS])
            for h0 in range(0, acc.shape[1], EPI_COLS):
                cs = slice(c0 + h0, c0 + h0 + EPI_COLS)
                o_ref[:, cs] = _head_epilogue(acc[:, h0:h0 + EPI_COLS], rest[0][:, cs], rope, mats).astype(o_ref.dtype)
        return
    acc = _dot(xs[0][...], ws[0][...])
    for x_ref, w_ref in zip(xs[1:], ws[1:]):
        acc = acc + _dot(x_ref[...], w_ref[...])
    if epi == "plain":
        out = acc
    elif epi == "residual":
        out = acc + _pick_part(rest[:-1], res_starts, pl.program_id(0))
    else:
        raise ValueError(epi)
    o_ref[...] = out.astype(o_ref.dtype)


def _part_specs(parts, tm, block_cols, col_of):
    specs, starts, st = [], [], 0
    for p in parts:
        nb = p.shape[0] // tm
        specs.append(pl.BlockSpec((tm, block_cols), lambda i, *j, st=st, nb=nb: (jnp.clip(i - st, 0, nb - 1), col_of(*j))))
        starts.append(st)
        st += nb
    return specs, starts


def _matmul(xs, ws, out_dtype, epi="plain", extra=(), pos_blk=None, tm=1024, tn=1024):
    t = xs[0].shape[0]
    n = ws[0].shape[1]
    tm, tn = min(tm, t), min(tn, n)
    in_specs = [pl.BlockSpec((tm, x.shape[1]), lambda i, j: (i, 0)) for x in xs]
    in_specs += [pl.BlockSpec((w.shape[0], tn), lambda i, j: (0, j)) for w in ws]
    res_starts = None
    if epi == "residual":
        specs, res_starts = _part_specs(extra, tm, tn, lambda j: j)
        in_specs += specs
    elif epi in ("headnorm", "headnorm_rope"):
        in_specs.append(pl.BlockSpec((1, tn), lambda i, j: (0, j)))
        if epi == "headnorm_rope":
            in_specs += [pl.BlockSpec((tm, HEAD_DIM), lambda i, j: (pos_blk(i), 0))] * 2
    return pl.pallas_call(
        functools.partial(_mm_kernel, n_in=len(xs), epi=epi, res_starts=res_starts),
        grid=(t // tm, n // tn),
        in_specs=in_specs,
        out_specs=pl.BlockSpec((tm, tn), lambda i, j: (i, j)),
        out_shape=jax.ShapeDtypeStruct((t, n), out_dtype),
        compiler_params=_cparams("parallel", "parallel"),
        name="matmul_" + epi,
    )(*xs, *ws, *extra)


def _mlstm_kernel(qf_ref, kf_ref, vf_ref, gf_ref, qb_ref, kb_ref, vb_ref, gb_ref, bias_ref,
                  hf_ref, hb_ref, c_ref, n_ref, m_ref, *, nh, nchunks, start_chunks, last_chunks):
    c = pl.program_id(0)
    cb = nchunks - 1 - c
    ch = MLSTM_CHUNK
    is_start = functools.reduce(jnp.logical_or, [c == s for s in start_chunks])
    is_last = functools.reduce(jnp.logical_or, [cb == e for e in last_chunks])

    def _reset(d):
        c_ref[d] = jnp.zeros((nh, HEAD_DIM, HEAD_DIM), F32)
        n_ref[d] = jnp.zeros((nh, 1, HEAD_DIM), F32)
        m_ref[d] = jnp.zeros((nh, 1, HEAD_DIM), F32)

    pl.when(is_start)(lambda: _reset(0))
    pl.when(is_last)(lambda: _reset(1))

    row = lax.broadcasted_iota(jnp.int32, (ch, ch), 0)
    col = lax.broadcasted_iota(jnp.int32, (ch, ch), 1)
    scale = HEAD_DIM ** -0.5
    heads = lambda ref: jnp.stack([ref[:, h * HEAD_DIM:(h + 1) * HEAD_DIM] for h in range(nh)])
    bdot = lambda a, b, ca, cb: lax.dot_general(a, b, (((ca,), (cb,)), ((0,), (0,))), preferred_element_type=F32)

    for d, (q_ref, k_ref, v_ref, g_ref, h_ref) in enumerate(
            ((qf_ref, kf_ref, vf_ref, gf_ref, hf_ref), (qb_ref, kb_ref, vb_ref, gb_ref, hb_ref))):
        mask = (row >= col) if d == 0 else (col >= row)
        icol0 = 2 * nh * d
        fcol0 = icol0 + nh
        tot_row = ch - 1 if d == 0 else 0
        g = g_ref[...] + bias_ref[...]
        lf = jnp.minimum(g, 0.0) - jnp.log1p(jnp.exp(-jnp.abs(g)))
        tri = jnp.where(mask, 1.0, 0.0).astype(BF16)
        hi, mid, lo = _split3(lf)
        bcum = _dot(tri, hi) + _dot(tri, mid) + _dot(tri, lo)
        a = g - pltpu.roll(bcum, HEAD_DIM - nh, 1)
        a_t = a.T
        bcol = jnp.stack([bcum[:, fcol0 + h:fcol0 + h + 1] for h in range(nh)])
        acol = jnp.stack([a[:, icol0 + h:icol0 + h + 1] for h in range(nh)])
        arow = jnp.stack([a_t[icol0 + h:icol0 + h + 1, :] for h in range(nh)])
        b_tot = bcol[:, tot_row:tot_row + 1, :]
        m_st = m_ref[d][:, :, 0:1]
        c_prev = c_ref[d]
        n_prev = n_ref[d]

        log_d = jnp.where(mask[None], bcol + arow, NEG)
        m_intra = jnp.max(log_d, axis=2, keepdims=True)
        log_inter = bcol + m_st
        m_row = jnp.maximum(log_inter, m_intra)
        dmat = jnp.exp(log_d - m_row)
        q = heads(q_ref)
        ks = (heads(k_ref).astype(F32) * scale).astype(BF16)
        v = heads(v_ref)
        p = bdot(q, ks, 2, 2) * dmat
        w_inter = jnp.exp(log_inter - m_row)
        num = bdot(p.astype(BF16), v, 2, 1) + w_inter * bdot(q, c_prev.astype(BF16), 2, 1)
        den = (jnp.sum(p, axis=2, keepdims=True)
               + w_inter * jnp.sum(q.astype(F32) * n_prev, axis=2, keepdims=True))
        h_out = num / jnp.maximum(jnp.abs(den), jnp.exp(-m_row))
        for h in range(nh):
            h_ref[:, h * HEAD_DIM:(h + 1) * HEAD_DIM] = h_out[h]

        m_kv = b_tot + jnp.max(arow, axis=2, keepdims=True)
        m_new = jnp.maximum(b_tot + m_st, m_kv)
        decay = jnp.exp(b_tot + m_st - m_new)
        wexp = jnp.exp(b_tot + acol - m_new)
        kw = ks.astype(F32) * wexp
        c_ref[d] = decay * c_prev + bdot(kw.astype(BF16), v, 1, 1)
        n_ref[d] = decay * n_prev + jnp.sum(kw, axis=1, keepdims=True)
        m_ref[d] = jnp.broadcast_to(m_new, (nh, 1, HEAD_DIM))


def _mlstm(u_a, gates, gate_bias, seqs, nh):
    t = u_a.shape[0]
    ch = MLSTM_CHUNK
    nchunks = t // ch
    da = nh * HEAD_DIM
    start_chunks = [s // ch for s, _ in seqs]
    last_chunks = [(s + l) // ch - 1 for s, l in seqs]
    fwd = lambda j: pl.BlockSpec((ch, da), lambda c, j=j: (c, j))
    bwd = lambda j: pl.BlockSpec((ch, da), lambda c, j=j: (nchunks - 1 - c, j))
    gspec_f = pl.BlockSpec((ch, 128), lambda c: (c, 0))
    gspec_b = pl.BlockSpec((ch, 128), lambda c: (nchunks - 1 - c, 0))
    bias = jnp.zeros((1, 128), F32).at[0, :4 * nh].set(gate_bias.reshape(-1).astype(F32))
    return pl.pallas_call(
        functools.partial(_mlstm_kernel, nh=nh, nchunks=nchunks, start_chunks=start_chunks,
                          last_chunks=last_chunks),
        grid=(nchunks,),
        in_specs=[fwd(0), fwd(1), fwd(2), gspec_f, bwd(0), bwd(1), bwd(2), gspec_b,
                  pl.BlockSpec((1, 128), lambda c: (0, 0))],
        out_specs=[pl.BlockSpec((ch, da), lambda c: (c, 0)),
                   pl.BlockSpec((ch, da), lambda c: (nchunks - 1 - c, 0))],
        out_shape=[jax.ShapeDtypeStruct((t, da), F32)] * 2,
        scratch_shapes=[pltpu.VMEM((2, nh, HEAD_DIM, HEAD_DIM), F32),
                        pltpu.VMEM((2, nh, 1, HEAD_DIM), F32),
                        pltpu.VMEM((2, nh, 1, HEAD_DIM), F32)],
        compiler_params=_cparams("arbitrary"),
        name="mlstm",
    )(u_a, u_a, u_a, gates, u_a, u_a, u_a, gates, bias)


def _mlstm_gate_kernel(hf_ref, hb_ref, o_ref, g_ref, y_ref):
    hsum = hf_ref[...] + hb_ref[...]
    o = o_ref[...].astype(F32)
    g = g_ref[...]
    outs = []
    for h in range(hsum.shape[1] // HEAD_DIM):
        hs = slice(h * HEAD_DIM, (h + 1) * HEAD_DIM)
        a = hsum[:, hs]
        y = a * lax.rsqrt(jnp.mean(a * a, axis=-1, keepdims=True) + EPS) * g[:, hs]
        outs.append(jax.nn.sigmoid(o[:, hs]) * y)
    y_ref[...] = jnp.concatenate(outs, axis=1).astype(y_ref.dtype)


def _mlstm_gate(h_f, h_b, u_a, hnorm_g, tm=512):
    t, da = h_f.shape
    return pl.pallas_call(
        _mlstm_gate_kernel,
        grid=(t // tm,),
        in_specs=[pl.BlockSpec((tm, da), lambda i: (i, 0)), pl.BlockSpec((tm, da), lambda i: (i, 0)),
                  pl.BlockSpec((tm, da), lambda i: (i, 3)), pl.BlockSpec((1, da), lambda i: (0, 0))],
        out_specs=pl.BlockSpec((tm, da), lambda i: (i, 0)),
        out_shape=jax.ShapeDtypeStruct((t, da), BF16),
        compiler_params=_cparams("parallel"),
        name="mlstm_gate",
    )(h_f, h_b, u_a, hnorm_g.reshape(1, da).astype(F32))


NA_QROWS = 4
NA_KROWS = NA_QROWS + NA_ROWS
NA_UNROLL = 2


def _na_bias_table(rel_bias):
    h = rel_bias.shape[0]
    nr, nc = 2 * NA_ROWS - 1, 2 * NA_COLS - 1
    c = np.arange(GRID_W)
    dc = np.clip(c[None, :] - c[:, None], -(NA_COLS - 1), NA_COLS - 1) + NA_COLS - 1
    c_start = np.clip(c - NA_COLS // 2, 0, GRID_W - NA_COLS)
    col_ok = (c[None, :] >= c_start[:, None]) & (c[None, :] < c_start[:, None] + NA_COLS)
    var = np.arange(3)[:, None, None]
    qi = np.arange(NA_QROWS)[None, :, None]
    kr = np.arange(NA_KROWS)[None, None, :]
    dr = kr - var * NA_QROWS - qi
    first = np.where(var == 0, 0, np.where(var == 1, qi, NA_QROWS))
    row_ok = (kr >= first) & (kr < first + NA_ROWS)
    dr_idx = np.clip(dr + NA_ROWS - 1, 0, nr - 1).reshape(-1)
    onehot = (dc.reshape(1, -1) == np.arange(nc)[:, None]).astype(np.float32)
    cols = jnp.dot(rel_bias.astype(F32).reshape(h * nr, nc), onehot, precision=lax.Precision.HIGHEST)
    cols = cols.reshape(h, nr, GRID_W, GRID_W)
    tbl = jnp.stack([cols[:, int(i)] for i in dr_idx], axis=1)
    tbl = tbl.reshape(h, 3, NA_QROWS, NA_KROWS, GRID_W, GRID_W)
    ok = row_ok[:, :, :, None, None] & col_ok[None, None, None, :, :]
    tbl = jnp.where(ok[None], tbl, NEG)
    return jnp.transpose(tbl, (0, 1, 2, 4, 3, 5)).reshape(h, 3, NA_QROWS * GRID_W, NA_KROWS * GRID_W)


def _na_kernel(q_ref, kp_ref, km_ref, kn_ref, vp_ref, vm_ref, vn_ref, bias_ref, o_ref, kcat, vcat,
               *, seg_rows, row_lo, row_hi):
    s = pl.program_id(1)
    halo = NA_ROWS * GRID_W
    seg = seg_rows * GRID_W
    kcat[0:halo] = kp_ref[...]
    kcat[halo:halo + seg] = km_ref[...]
    kcat[halo + seg:halo + seg + halo] = kn_ref[...]
    vcat[0:halo, 0:HEAD_DIM] = vp_ref[...]
    vcat[halo:halo + seg, 0:HEAD_DIM] = vm_ref[...]
    vcat[halo + seg:halo + seg + halo, 0:HEAD_DIM] = vn_ref[...]
    vcat[:, HEAD_DIM:] = jnp.ones((seg + 2 * halo, HEAD_DIM), BF16)
    r_lo = _seq_lookup(s, row_lo, 0)
    r_hi = _seq_lookup(s, row_hi, 0)
    scale = HEAD_DIM ** -0.5
    nq = NA_QROWS * GRID_W
    nk = NA_KROWS * GRID_W

    def one_group(gl):
        r0 = s * seg_rows + gl * NA_QROWS
        u = jnp.clip(r0 - NA_ROWS // 2, r_lo, r_hi - NA_KROWS)
        off = pl.multiple_of((u - s * seg_rows + NA_ROWS) * GRID_W, GRID_W)
        qoff = pl.multiple_of(gl * nq, nq)
        q = q_ref[pl.ds(qoff, nq), :]
        sc = _dot_nt(q, kcat[pl.ds(off, nk), :]) * scale + bias_ref[(r0 - u) // NA_QROWS]
        e = jnp.exp(sc - jnp.max(sc, axis=1, keepdims=True))
        nd = _dot(e.astype(BF16), vcat[pl.ds(off, nk), :])
        o_ref[pl.ds(qoff, nq), :] = (nd[:, :HEAD_DIM] / nd[:, HEAD_DIM:]).astype(o_ref.dtype)

    def body(it, carry):
        for k in range(NA_UNROLL):
            one_group(it * NA_UNROLL + k)
        return carry

    lax.fori_loop(0, seg_rows // (NA_QROWS * NA_UNROLL), body, 0)


def _neighbourhood_attention(qk, v, bias_tbl, seqs, seg):
    t, dh = v.shape
    nh = dh // HEAD_DIM
    nseg = t // seg
    seg_rows = seg // GRID_W
    assert all(l // GRID_W >= NA_KROWS and (s // GRID_W) % NA_QROWS == 0 and (l // GRID_W) % NA_QROWS == 0
               for s, l in seqs)
    halo = NA_ROWS * GRID_W
    hpb = seg // halo
    nhb = t // halo
    row_lo = [(s // seg, s // GRID_W) for s, _ in seqs]
    row_hi = [(s // seg, (s + l) // GRID_W) for s, l in seqs]
    main = lambda c0: pl.BlockSpec((seg, HEAD_DIM), lambda h, s: (s, c0 + h))
    prev = lambda c0: pl.BlockSpec((halo, HEAD_DIM), lambda h, s: (jnp.maximum(s * hpb - 1, 0), c0 + h))
    nxt = lambda c0: pl.BlockSpec((halo, HEAD_DIM), lambda h, s: (jnp.minimum((s + 1) * hpb, nhb - 1), c0 + h))
    return pl.pallas_call(
        functools.partial(_na_kernel, seg_rows=seg_rows, row_lo=row_lo, row_hi=row_hi),
        grid=(nh, nseg),
        in_specs=[main(0), prev(nh), main(nh), nxt(nh), prev(0), main(0), nxt(0),
                  pl.BlockSpec((None, 3, NA_QROWS * GRID_W, NA_KROWS * GRID_W), lambda h, s: (h, 0, 0, 0))],
        out_specs=main(0),
        out_shape=jax.ShapeDtypeStruct((t, dh), BF16),
        scratch_shapes=[pltpu.VMEM((seg + 2 * halo, HEAD_DIM), BF16), pltpu.VMEM((seg + 2 * halo, 2 * HEAD_DIM), BF16)],
        compiler_params=_cparams("parallel", "parallel"),
        name="natten",
    )(qk, qk, qk, qk, v, v, v, bias_tbl)


DIL_BLK = 128
DIL_RADIUS = 64
DIL_MAX = max(d for _, d in DIL_PAIRS)
DIL_QB = DIL_BLK * DIL_MAX
DIL_HALO = DIL_RADIUS * DIL_MAX


def _dilated_kernel(q_ref, kp_ref, kc_ref, kn_ref, vp_ref, vc_ref, vn_ref, y_ref, num_s, mx_s, den_s,
                    *, first_blocks, last_blocks):
    i = pl.program_id(1)
    has_prev = jnp.logical_not(functools.reduce(jnp.logical_or, [i == b for b in first_blocks]))
    has_next = jnp.logical_not(functools.reduce(jnp.logical_or, [i == b for b in last_blocks]))
    nk = DIL_BLK + 2 * DIL_RADIUS
    j_io = lax.broadcasted_iota(jnp.int32, (DIL_BLK, nk), 0)
    s_io = lax.broadcasted_iota(jnp.int32, (DIL_BLK, nk), 1)
    band = jnp.abs(s_io - DIL_RADIUS - j_io) <= DIL_RADIUS
    ok_prev = jnp.logical_or(s_io >= DIL_RADIUS, has_prev)
    ok_next = jnp.logical_or(s_io < DIL_BLK + DIL_RADIUS, has_next)
    madd = {}
    for up in (False, True):
        for un in (False, True):
            ok = band
            ok = jnp.logical_and(ok, ok_prev) if up else ok
            ok = jnp.logical_and(ok, ok_next) if un else ok
            madd[up, un] = jnp.where(ok, 0.0, NEG)
    scale = HEAD_DIM ** -0.5
    ones_v = jnp.ones((nk, HEAD_DIM), BF16)

    def rows(start, size, d):
        return pl.ds(start, size, stride=d) if d > 1 else pl.ds(start, size)

    for b, (_, d) in enumerate(DIL_PAIRS):
        per_class = DIL_QB // d
        nj = per_class // DIL_BLK
        for g in range(d):
            for j in range(nj):
                up, un = j == 0, j == nj - 1
                qrows = rows(g + d * DIL_BLK * j, DIL_BLK, d)
                m0 = max(DIL_BLK * j - DIL_RADIUS, 0)
                m1 = min(DIL_BLK * (j + 1) + DIL_RADIUS, per_class)
                kparts, vparts = [], []
                if up:
                    r = rows(DIL_HALO - DIL_RADIUS * d + g, DIL_RADIUS, d)
                    kparts.append(kp_ref[r, :])
                    vparts.append(vp_ref[r, :])
                r = rows(g + d * m0, m1 - m0, d)
                kparts.append(kc_ref[r, :])
                vparts.append(vc_ref[r, :])
                if un:
                    r = rows(g, DIL_RADIUS, d)
                    kparts.append(kn_ref[r, :])
                    vparts.append(vn_ref[r, :])
                kk = jnp.concatenate(kparts, axis=0).astype(BF16)
                vv = jnp.concatenate([jnp.concatenate(vparts, axis=0).astype(BF16), ones_v], axis=1)
                sc = _dot_nt(q_ref[qrows, :].astype(BF16), kk) * scale + madd[up, un]
                mx = jnp.max(sc, axis=1, keepdims=True)
                nd = _dot(jnp.exp(sc - mx).astype(BF16), vv)
                num_s[b, qrows, :] = nd[:, :HEAD_DIM]
                mx_s[b, qrows, :] = jnp.broadcast_to(mx, (DIL_BLK, HEAD_DIM))
                den_s[b, qrows, :] = nd[:, HEAD_DIM:]

    nb = len(DIL_PAIRS)
    mxs = [mx_s[b] for b in range(nb)]
    m_top = functools.reduce(jnp.maximum, mxs)
    w = [jnp.exp(m - m_top) for m in mxs]
    num = functools.reduce(jnp.add, [w[b] * num_s[b] for b in range(nb)])
    den = functools.reduce(jnp.add, [w[b] * den_s[b] for b in range(nb)])
    y_ref[...] = (num / den).astype(y_ref.dtype)


def _dilated_attention(qk, v, seqs):
    t, dh = v.shape
    nh = dh // HEAD_DIM
    assert all(window // (2 * dil) == DIL_RADIUS for window, dil in DIL_PAIRS)
    assert all(s % DIL_QB == 0 and l % DIL_QB == 0 for s, l in seqs)
    nq = t // DIL_QB
    hb = DIL_QB // DIL_HALO
    nhalo = t // DIL_HALO
    first_blocks = [s // DIL_QB for s, _ in seqs]
    last_blocks = [(s + l) // DIL_QB - 1 for s, l in seqs]
    cur = lambda c0: pl.BlockSpec((DIL_QB, HEAD_DIM), lambda h, i: (i, c0 + h))
    prev = lambda c0: pl.BlockSpec((DIL_HALO, HEAD_DIM), lambda h, i: (jnp.maximum(i * hb - 1, 0), c0 + h))
    nxt = lambda c0: pl.BlockSpec((DIL_HALO, HEAD_DIM), lambda h, i: (jnp.minimum((i + 1) * hb, nhalo - 1), c0 + h))
    nb = len(DIL_PAIRS)
    return pl.pallas_call(
        functools.partial(_dilated_kernel, first_blocks=first_blocks, last_blocks=last_blocks),
        grid=(nh, nq),
        in_specs=[cur(0), prev(nh), cur(nh), nxt(nh), prev(0), cur(0), nxt(0)],
        out_specs=cur(0),
        out_shape=jax.ShapeDtypeStruct((t, dh), BF16),
        scratch_shapes=[pltpu.VMEM((nb, DIL_QB, HEAD_DIM), F32)] * 3,
        compiler_params=_cparams("parallel", "parallel"),
        name="dilated",
    )(qk, qk, qk, qk, v, v, v)


def _rope_tables(max_len):
    half = ROPE_DIM // 2
    inv_freq = ROPE_THETA ** (-jnp.arange(half, dtype=F32) / half)
    ang = jnp.arange(max_len, dtype=F32)[:, None] * inv_freq[None, :]
    cos, sin = jnp.cos(ang), jnp.sin(ang)
    rest = HEAD_DIM - ROPE_DIM
    c = jnp.concatenate([cos, cos, jnp.ones((max_len, rest), F32)], axis=1)
    s = jnp.concatenate([-sin, sin, jnp.zeros((max_len, rest), F32)], axis=1)
    return c, s


def _router_kernel(x_ref, g_ref, wr_ref, rb_ref, xn_ref, e_ref, gt_ref, *, n_exp):
    x = x_ref[...]
    xn = x * lax.rsqrt(jnp.mean(x * x, axis=-1, keepdims=True) + EPS) * g_ref[...]
    xn_ref[...] = xn.astype(xn_ref.dtype)
    tm = x.shape[0]
    epg = n_exp // N_GROUPS
    xh, xm, _ = _split3(xn)
    wh, wm, _ = _split3(wr_ref[...])
    logits = _dot_nt(wh, xh) + _dot_nt(wh, xm) + _dot_nt(wm, xh)
    scores = jax.nn.sigmoid(logits)
    sel = (scores + rb_ref[:, 0:1]).reshape(N_GROUPS, epg, tm)
    sc3 = scores.reshape(N_GROUPS, epg, tm)
    io = lax.broadcasted_iota(jnp.int32, (N_GROUPS, epg, tm), 1)
    gio = lax.broadcasted_iota(jnp.int32, (N_GROUPS, 1, tm), 0)
    m1 = jnp.max(sel, axis=1, keepdims=True)
    i1 = jnp.min(jnp.where(sel == m1, io, epg), axis=1, keepdims=True)
    sel_b = jnp.where(io == i1, -jnp.inf, sel)
    m2 = jnp.max(sel_b, axis=1, keepdims=True)
    i2 = jnp.min(jnp.where(sel_b == m2, io, epg), axis=1, keepdims=True)
    gs = m1 + m2
    gbest = jnp.min(jnp.where(gs == jnp.max(gs, axis=0, keepdims=True), gio, N_GROUPS), axis=0, keepdims=True)
    in_g = gio == gbest
    outs_e, outs_w = [], []
    for ik in (i1, i2):
        pick = in_g & (io == ik)
        outs_e.append(jnp.sum(jnp.where(in_g, gbest * epg + ik, 0), axis=0))
        outs_w.append(jnp.sum(jnp.sum(jnp.where(pick, sc3, 0.0), axis=1, keepdims=True), axis=0))
    wsum = outs_w[0] + outs_w[1]
    e_ref[0:1, :] = outs_e[0]
    e_ref[1:2, :] = outs_e[1]
    gt_ref[0:1, :] = outs_w[0] / wsum
    gt_ref[1:2, :] = outs_w[1] / wsum


def _router(x, g, w_router, router_bias, tm=512):
    t, d = x.shape
    n_exp = w_router.shape[1]
    rb = jnp.broadcast_to(router_bias.astype(F32)[:, None], (n_exp, 128))
    return pl.pallas_call(
        functools.partial(_router_kernel, n_exp=n_exp),
        grid=(t // tm,),
        in_specs=[pl.BlockSpec((tm, d), lambda i: (i, 0)), pl.BlockSpec((1, d), lambda i: (0, 0)),
                  pl.BlockSpec((n_exp, d), lambda i: (0, 0)), pl.BlockSpec((n_exp, 128), lambda i: (0, 0))],
        out_specs=[pl.BlockSpec((tm, d), lambda i: (i, 0)), pl.BlockSpec((2, tm), lambda i: (0, i)),
                   pl.BlockSpec((2, tm), lambda i: (0, i))],
        out_shape=[jax.ShapeDtypeStruct((t, d), F32), jax.ShapeDtypeStruct((2, t), jnp.int32),
                   jax.ShapeDtypeStruct((2, t), F32)],
        compiler_params=_cparams("parallel"),
        name="router",
    )(x, g.reshape(1, d).astype(F32), w_router.T.astype(F32), rb)


def _rank_kernel(e_ref, rank_ref, cnt_ref, run_ref, *, n_exp):
    i = pl.program_id(0)
    tm = e_ref.shape[1]

    @pl.when(i == 0)
    def _():
        run_ref[...] = jnp.zeros_like(run_ref)

    eio = lax.broadcasted_iota(jnp.int32, (n_exp, tm), 0)
    oh = [(eio == e_ref[k:k + 1, :]) for k in range(TOP_K)]
    ohf = [jnp.where(o, 1.0, 0.0) for o in oh]
    both = ohf[0] + ohf[1]
    r_i = lax.broadcasted_iota(jnp.int32, (tm, tm), 0)
    c_i = lax.broadcasted_iota(jnp.int32, (tm, tm), 1)
    upper = jnp.where(r_i <= c_i, 1.0, 0.0).astype(BF16)
    cum = _dot(both.astype(BF16), upper)
    base = run_ref[:, 0:1] + cum - both
    for k in range(TOP_K):
        rank_ref[k:k + 1, :] = jnp.sum(jnp.where(oh[k], base, 0.0), axis=0, keepdims=True).astype(jnp.int32)
    run_new = run_ref[...] + cum[:, tm - 1:tm]
    run_ref[...] = run_new
    cnt_ref[...] = run_new.astype(jnp.int32)


def _rank(e, n_exp, tm=512):
    t = e.shape[1]
    return pl.pallas_call(
        functools.partial(_rank_kernel, n_exp=n_exp),
        grid=(t // tm,),
        in_specs=[pl.BlockSpec((2, tm), lambda i: (0, i))],
        out_specs=[pl.BlockSpec((2, tm), lambda i: (0, i)), pl.BlockSpec((n_exp, 128), lambda i: (0, 0))],
        out_shape=[jax.ShapeDtypeStruct((2, t), jnp.int32), jax.ShapeDtypeStruct((n_exp, 128), jnp.int32)],
        scratch_shapes=[pltpu.VMEM((n_exp, 128), F32)],
        compiler_params=_cparams("arbitrary"),
        name="moe_rank",
    )(e)


MOE_BLK = 256


CAST_ROWS = 256


ROW_UNROLL = 8


def _start_rows(n_rows, make_copy):
    def body(r, c):
        for k in range(TOP_K):
            make_copy(k, r).start(priority=k)
        return c
    lax.fori_loop(0, n_rows, body, 0, unroll=ROW_UNROLL)


def _wait_rows(n_rows, make_copy):
    def body(r, c):
        for k in range(TOP_K):
            make_copy(k, 0).wait()
        return c
    lax.fori_loop(0, n_rows, body, 0, unroll=ROW_UNROLL)


def _dispatch_kernel(fill_ref, ps_ref, e_ref, rank_ref, xn_ref, xs_out, zbuf, sem, zsem, *, tm, n_fill):
    i = pl.program_id(0)

    @pl.when(i == 0)
    def _():
        zbuf[...] = jnp.zeros_like(zbuf)

        def fill(j):
            rows = pl.ds(pl.multiple_of(fill_ref[j] * MOE_BLK, MOE_BLK), MOE_BLK)
            return pltpu.make_async_copy(zbuf, xs_out.at[rows, :], zsem)

        def start(j, c):
            pl.when(fill_ref[n_fill + j] > 0)(lambda: fill(j).start())
            return c

        def wait(j, c):
            pl.when(fill_ref[n_fill + j] > 0)(lambda: fill(j).wait())
            return c

        lax.fori_loop(0, n_fill, start, 0)
        lax.fori_loop(0, n_fill, wait, 0)

    def copy(k, r):
        slot = ps_ref[e_ref[k, r]] + rank_ref[k, r]
        return pltpu.make_async_copy(xn_ref.at[pl.ds(r, 1), :], xs_out.at[pl.ds(slot, 1), :], sem)

    _start_rows(tm, copy)
    _wait_rows(tm, copy)


def _dispatch(e, rank, pad_start, xn, fill, cap, tm=256):
    t, d = xn.shape
    smem = pl.BlockSpec((TOP_K, tm), lambda i, f, ps: (0, i), memory_space=pltpu.SMEM)
    return pl.pallas_call(
        functools.partial(_dispatch_kernel, tm=tm, n_fill=fill.shape[0] // 2),
        grid_spec=pltpu.PrefetchScalarGridSpec(
            num_scalar_prefetch=2, grid=(t // tm,),
            in_specs=[smem, smem, pl.BlockSpec((tm, d), lambda i, f, ps: (i, 0))],
            out_specs=pl.BlockSpec(memory_space=pl.ANY),
            scratch_shapes=[pltpu.VMEM((MOE_BLK, d), F32), pltpu.SemaphoreType.DMA, pltpu.SemaphoreType.DMA]),
        out_shape=jax.ShapeDtypeStruct((cap, d), F32),
        compiler_params=_cparams("arbitrary"),
        name="moe_dispatch",
    )(fill, pad_start, e, rank, xn)


def _dot_casting(x, w_refs, s_refs):
    k = x.shape[1]
    step = min(CAST_ROWS, k)
    accs = [None] * len(w_refs)
    for r0 in range(0, k, step):
        rows = slice(r0, r0 + step)
        for n, (w_ref, s_ref) in enumerate(zip(w_refs, s_refs)):
            wc = w_ref[rows, :].astype(BF16)
            s_ref[rows, :] = wc
            part = _dot(x[:, rows], wc)
            accs[n] = part if accs[n] is None else accs[n] + part
    return accs


def _expert_changed(b, be_ref):
    return jnp.logical_or(b == 0, be_ref[b] != be_ref[jnp.maximum(b - 1, 0)])


def _ffn_up_kernel(be_ref, nb_ref, x_ref, wg_ref, wu_ref, h_ref, wg_s, wu_s):
    b = pl.program_id(0)
    live = b < nb_ref[0]
    first = _expert_changed(b, be_ref)

    def act(hg, hu):
        h_ref[...] = ((hg * jax.nn.sigmoid(hg)) * hu).astype(h_ref.dtype)

    @pl.when(jnp.logical_and(live, first))
    def _():
        act(*_dot_casting(x_ref[...].astype(BF16), (wg_ref, wu_ref), (wg_s, wu_s)))

    @pl.when(jnp.logical_and(live, jnp.logical_not(first)))
    def _():
        x = x_ref[...].astype(BF16)
        act(_dot(x, wg_s[...]), _dot(x, wu_s[...]))

    @pl.when(jnp.logical_not(live))
    def _():
        h_ref[...] = jnp.zeros_like(h_ref)


def _ffn_down_kernel(be_ref, nb_ref, h_ref, wd_ref, y_ref, wd_s):
    b = pl.program_id(0)
    live = b < nb_ref[0]
    first = _expert_changed(b, be_ref)

    @pl.when(jnp.logical_and(live, first))
    def _():
        y_ref[...] = _dot_casting(h_ref[...], (wd_ref,), (wd_s,))[0]

    @pl.when(jnp.logical_and(live, jnp.logical_not(first)))
    def _():
        y_ref[...] = _dot(h_ref[...], wd_s[...])

    @pl.when(jnp.logical_not(live))
    def _():
        y_ref[...] = jnp.zeros_like(y_ref)


def _expert_ffn(xs, blk_expert, n_used, layer, w_gate, w_up, w_down):
    cap, d = xs.shape
    f = w_gate.shape[3]
    nblk = cap // MOE_BLK
    used = lambda b, be, nb: (jnp.minimum(b, nb[0] - 1), 0)
    wsel = lambda b, be, nb: (layer, be[b], 0, 0)
    h = pl.pallas_call(
        _ffn_up_kernel,
        grid_spec=pltpu.PrefetchScalarGridSpec(
            num_scalar_prefetch=2, grid=(nblk,),
            in_specs=[pl.BlockSpec((MOE_BLK, d), used), pl.BlockSpec((None, None, d, f), wsel),
                      pl.BlockSpec((None, None, d, f), wsel)],
            out_specs=pl.BlockSpec((MOE_BLK, f), lambda b, be, nb: (b, 0)),
            scratch_shapes=[pltpu.VMEM((d, f), BF16)] * 2),
        out_shape=jax.ShapeDtypeStruct((cap, f), BF16),
        compiler_params=_cparams("arbitrary"),
        name="moe_ffn_up",
    )(blk_expert, n_used, xs, w_gate, w_up)
    return pl.pallas_call(
        _ffn_down_kernel,
        grid_spec=pltpu.PrefetchScalarGridSpec(
            num_scalar_prefetch=2, grid=(nblk,),
            in_specs=[pl.BlockSpec((MOE_BLK, f), used), pl.BlockSpec((None, None, f, d), wsel)],
            out_specs=pl.BlockSpec((MOE_BLK, d), lambda b, be, nb: (b, 0)),
            scratch_shapes=[pltpu.VMEM((f, d), BF16)]),
        out_shape=jax.ShapeDtypeStruct((cap, d), F32),
        compiler_params=_cparams("arbitrary"),
        name="moe_ffn_down",
    )(blk_expert, n_used, h, w_down)


def _combine_kernel(ps_ref, e_ref, rank_ref, e_nx, rank_nx, x_ref, gt_ref, ys_hbm, *rest, tm, nsteps,
                    out_starts, out_nblk):
    o_refs, (buf, sem) = rest[:-2], rest[-2:]
    i = pl.program_id(0)
    cur = lax.rem(i, 2)

    def gather(eref, rref, b):
        def copy(k, r):
            slot = ps_ref[eref[k, r]] + rref[k, r]
            return pltpu.make_async_copy(ys_hbm.at[pl.ds(slot, 1), :], buf.at[b, k, pl.ds(r, 1), :], sem.at[b])
        return copy

    pl.when(i == 0)(lambda: _start_rows(tm, gather(e_ref, rank_ref, 0)))
    pl.when(i + 1 < nsteps)(lambda: _start_rows(tm, gather(e_nx, rank_nx, 1 - cur)))
    _wait_rows(tm, gather(e_ref, rank_ref, cur))
    out = x_ref[...] + (gt_ref[:, 0:1] * buf[cur, 0] + gt_ref[:, 1:2] * buf[cur, 1])
    for o_ref, st, nb in zip(o_refs, out_starts, out_nblk):
        @pl.when(jnp.logical_and(i >= st, i < st + nb))
        def _(o_ref=o_ref):
            o_ref[...] = out


def _combine(e, rank, pad_start, x, gt_cols, ys, out_rows, tm=256):
    t, d = x.shape
    nsteps = t // tm
    smem = lambda nxt: pl.BlockSpec((TOP_K, tm), lambda i, ps: (0, jnp.minimum(i + nxt, nsteps - 1)),
                                    memory_space=pltpu.SMEM)
    out_nblk = [r // tm for r in out_rows]
    out_starts = [sum(out_nblk[:n]) for n in range(len(out_nblk))]
    out_specs = [pl.BlockSpec((tm, d), lambda i, ps, st=st, nb=nb: (jnp.clip(i - st, 0, nb - 1), 0))
                 for st, nb in zip(out_starts, out_nblk)]
    return pl.pallas_call(
        functools.partial(_combine_kernel, tm=tm, nsteps=nsteps, out_starts=out_starts, out_nblk=out_nblk),
        grid_spec=pltpu.PrefetchScalarGridSpec(
            num_scalar_prefetch=1, grid=(nsteps,),
            in_specs=[smem(0), smem(0), smem(1), smem(1),
                      pl.BlockSpec((tm, d), lambda i, ps: (i, 0)), pl.BlockSpec((tm, TOP_K), lambda i, ps: (i, 0)),
                      pl.BlockSpec(memory_space=pl.ANY)],
            out_specs=out_specs,
            scratch_shapes=[pltpu.VMEM((2, TOP_K, tm, d), F32), pltpu.SemaphoreType.DMA((2,))]),
        out_shape=[jax.ShapeDtypeStruct((r, d), F32) for r in out_rows],
        compiler_params=_cparams("arbitrary"),
        name="moe_combine",
    )(pad_start, e, rank, e, rank, x, gt_cols, ys)


def _moe(x, ln_g, w_router, router_bias, layer, w_gate, w_up, w_down, out_rows):
    t, d = x.shape
    n_exp = w_router.shape[1]
    xn, e, gt = _router(x, ln_g, w_router, router_bias)
    rank, cnt = _rank(e, n_exp)
    counts = cnt[:, 0]
    padded = (counts + MOE_BLK - 1) // MOE_BLK * MOE_BLK
    pad_end = jnp.cumsum(padded)
    pad_start = (pad_end - padded).astype(jnp.int32)
    cap = t * TOP_K + n_exp * MOE_BLK
    nblk = cap // MOE_BLK
    blk_first = jnp.arange(nblk, dtype=jnp.int32) * MOE_BLK
    blk_expert = jnp.minimum(jnp.sum(pad_end[None, :] <= blk_first[:, None], axis=1), n_exp - 1).astype(jnp.int32)
    n_used = (pad_end[-1] // MOE_BLK).astype(jnp.int32).reshape(1)
    trail = n_used[0] + jnp.arange(n_exp, dtype=jnp.int32)
    fill_blk = jnp.concatenate([pad_end // MOE_BLK - 1, jnp.minimum(trail, nblk - 1)])
    fill_ok = jnp.concatenate([padded > 0, trail < nblk])
    fill = jnp.concatenate([jnp.maximum(fill_blk, 0), fill_ok.astype(jnp.int32)]).astype(jnp.int32)
    xs = _dispatch(e, rank, pad_start, xn, fill, cap)
    ys = _expert_ffn(xs, blk_expert, n_used, layer, w_gate, w_up, w_down)
    return tuple(_combine(e, rank, pad_start, x, gt.T, ys, out_rows))


def _ab_layer(xp, seqs, seg, ln_g, w_in, gate_bias, hnorm_g, qnorm_g, knorm_g, rel_bias, w_out):
    nh_a = gate_bias.shape[1]
    nh_b = rel_bias.shape[0]
    d_a, d_b = nh_a * HEAD_DIM, nh_b * HEAD_DIM
    xn = _rmsnorm(xp, ln_g)
    w = w_in.astype(BF16)
    g0 = 4 * d_a
    g1 = g0 + 4 * nh_a
    u_a = _matmul([xn], [w[:, :g0]], BF16)
    w_g = jnp.zeros((w.shape[0], 128), BF16).at[:, :4 * nh_a].set(w[:, g0:g1])
    gates = _matmul([xn], [w_g], F32)
    gain = jnp.concatenate([jnp.tile(qnorm_g, nh_b), jnp.tile(knorm_g, nh_b)]).reshape(1, -1).astype(F32)
    qk_b = _matmul([xn], [w[:, g1:g1 + 2 * d_b]], BF16, epi="headnorm", extra=(gain,), tn=1024)
    v_b = _matmul([xn], [w[:, g1 + 2 * d_b:]], BF16)
    h_f, h_b = _mlstm(u_a, gates, gate_bias, seqs, nh_a)
    y_a = _mlstm_gate(h_f, h_b, u_a, hnorm_g)
    y_b = _neighbourhood_attention(qk_b, v_b, _na_bias_table(rel_bias), seqs, seg)
    wo = w_out.astype(BF16)
    return _matmul([y_a, y_b], [wo[:d_a], wo[d_a:]], F32, epi="residual", extra=xp)


def _dilated_layer(xp, seqs, ln_g, w_in, qnorm_g, knorm_g, w_out):
    d_c = w_in.shape[1] // 3
    nh = d_c // HEAD_DIM
    xn = _rmsnorm(xp, ln_g)
    w = w_in.astype(BF16)
    tm = 1024
    max_len = max(l for _, l in seqs)
    tables = _rope_tables(max_len)
    pos_tbl = [(s // tm, s // tm) for s, _ in seqs]
    pos_blk = lambda i: i - _seq_lookup(i, pos_tbl, 0)
    gain = jnp.concatenate([jnp.tile(qnorm_g, nh), jnp.tile(knorm_g, nh)]).reshape(1, -1).astype(F32)
    qk = _matmul([xn], [w[:, :2 * d_c]], F32, epi="headnorm_rope", extra=(gain,) + tables,
                 pos_blk=pos_blk, tm=tm, tn=1024)
    v = _matmul([xn], [w[:, 2 * d_c:]], F32, tm=tm)
    y = _dilated_attention(qk, v, seqs)
    return _matmul([y], [w_out.astype(BF16)], F32, epi="residual", extra=xp)


def kernel(x_prompt, x_sample, ab_ln, ab_w_in, ab_gate_bias, ab_hnorm, ab_qnorm, ab_knorm, ab_relbias, ab_w_out,
           c_ln, c_w_in, c_qnorm, c_knorm, c_w_out, ffn_ln, w_router, router_bias, w_gate, w_up, w_down):
    bp, lp, d = x_prompt.shape
    bs, ls, _ = x_sample.shape
    tp = bp * lp
    seqs = [(b * lp, lp) for b in range(bp)] + [(tp + b * ls, ls) for b in range(bs)]
    seg = math.gcd(lp, ls)
    xp = (x_prompt.reshape(tp, d), x_sample.reshape(bs * ls, d))
    depth = ffn_ln.shape[0]
    for layer in range(depth):
        j = layer // 2
        if layer % 2 == 0:
            x = _ab_layer(xp, seqs, seg, ab_ln[j], ab_w_in[j], ab_gate_bias[j], ab_hnorm[j].reshape(-1),
                          ab_qnorm[j], ab_knorm[j], ab_relbias[j], ab_w_out[j])
        else:
            x = _dilated_layer(xp, seqs, c_ln[j], c_w_in[j], c_qnorm[j], c_knorm[j], c_w_out[j])
        xp = _moe(x, ffn_ln[layer], w_router, router_bias, layer, w_gate, w_up, w_down,
                  out_rows=(tp, bs * ls) if layer == depth - 1 else (tp + bs * ls,))
    return xp[0].reshape(bp, lp, d), xp[1].reshape(bs, ls, d)
```

```python
import functools
import math

import jax
import jax.numpy as jnp
import numpy as np
from jax import lax
from jax.experimental import pallas as pl
from jax.experimental.pallas import tpu as pltpu

HEAD_DIM = 128
MLSTM_CHUNK = 128
GRID_W = 64
NA_ROWS = 8
NA_COLS = 16
DIL_PAIRS = ((128, 1), (512, 4), (2048, 16))
ROPE_THETA = 500000.0
ROPE_DIM = HEAD_DIM // 4
N_GROUPS = 4
TOP_K = 2
EPS = 1e-6
NEG = -1e30
VMEM_LIMIT_BYTES = 56 * 1024 * 1024

F32 = jnp.float32
BF16 = jnp.bfloat16


def _cparams(*sem):
    return pltpu.CompilerParams(dimension_semantics=sem, vmem_limit_bytes=VMEM_LIMIT_BYTES)


def _dot(a, b):
    return jnp.dot(a, b, preferred_element_type=F32)


def _dot_nt(a, b):
    return lax.dot_general(a, b, (((1,), (1,)), ((), ())), preferred_element_type=F32)


def _dot_tn(a, b):
    return lax.dot_general(a, b, (((0,), (0,)), ((), ())), preferred_element_type=F32)


def _split3(x):
    hi = x.astype(BF16)
    r1 = x - hi.astype(F32)
    mid = r1.astype(BF16)
    lo = (r1 - mid.astype(F32)).astype(BF16)
    return hi, mid, lo


def _seq_lookup(u, table, default):
    out = default
    for thr, val in table:
        out = jnp.where(u >= thr, val, out)
    return out


def _rmsnorm_kernel(*refs, starts):
    g_ref, o_ref = refs[-2:]
    x = _pick_part(refs[:-2], starts, pl.program_id(0))
    ms = jnp.mean(x * x, axis=-1, keepdims=True)
    o_ref[...] = (x * lax.rsqrt(ms + EPS) * g_ref[...]).astype(o_ref.dtype)


def _rmsnorm(parts, g, tm=512):
    d = parts[0].shape[1]
    t = sum(p.shape[0] for p in parts)
    specs, starts = _part_specs(parts, tm, d, lambda: 0)
    return pl.pallas_call(
        functools.partial(_rmsnorm_kernel, starts=starts),
        grid=(t // tm,),
        in_specs=specs + [pl.BlockSpec((1, d), lambda i: (0, 0))],
        out_specs=pl.BlockSpec((tm, d), lambda i: (i, 0)),
        out_shape=jax.ShapeDtypeStruct((t, d), BF16),
        compiler_params=_cparams("parallel"),
        name="rmsnorm",
    )(*parts, g.reshape(1, d).astype(F32))


EPI_COLS = 256


def _head_mats(n):
    k = lax.broadcasted_iota(jnp.int32, (n, n), 0)
    i = lax.broadcasted_iota(jnp.int32, (n, n), 1)
    shift = HEAD_DIM.bit_length() - 1
    same = jnp.right_shift(k, shift) == jnp.right_shift(i, shift)
    kl, il = jnp.bitwise_and(k, HEAD_DIM - 1), jnp.bitwise_and(i, HEAD_DIM - 1)
    half = ROPE_DIM // 2
    partner = jnp.where(il < half, il + half, jnp.where(il < 2 * half, il - half, -1))
    ones = jnp.where(same, 1.0, 0.0).astype(BF16)
    swap = jnp.where(jnp.logical_and(same, kl == partner), 1.0, 0.0).astype(BF16)
    return ones, swap


def _head_epilogue(acc, gain, rope, mats):
    ones, swap = mats
    y = acc * lax.rsqrt(_dot((acc * acc).astype(BF16), ones) * (1.0 / HEAD_DIM) + EPS) * gain
    if rope is not None:
        reps = acc.shape[1] // HEAD_DIM
        c, s = (jnp.concatenate([t] * reps, axis=1) for t in rope)
        y = y * c + _dot(y.astype(BF16), swap) * s
    return y


def _pick_part(refs, starts, i):
    val = refs[0][...]
    for ref, st in zip(refs[1:], starts[1:]):
        val = jnp.where(i >= st, ref[...], val)
    return val


def _mm_kernel(*refs, n_in, epi, res_starts):
    xs, ws, rest = refs[:n_in], refs[n_in:2 * n_in], refs[2 * n_in:]
    o_ref = rest[-1]
    if epi in ("headnorm", "headnorm_rope"):
        rope = (rest[1][...], rest[2][...]) if epi == "headnorm_rope" else None
        mats = _head_mats(EPI_COLS)
        x = xs[0][...]
        for c0 in range(0, o_ref.shape[1], 2 * EPI_COLS):
            acc = _dot(x, ws[0][:, c0:c0 + 2 * EPI_COLS])
            for h0 in range(0, acc.shape[1], EPI_COLS):
                cs = slice(c0 + h0, c0 + h0 + EPI_COLS)
                o_ref[:, cs] = _head_epilogue(acc[:, h0:h0 + EPI_COLS], rest[0][:, cs], rope, mats).astype(o_ref.dtype)
        return
    acc = _dot(xs[0][...], ws[0][...])
    for x_ref, w_ref in zip(xs[1:], ws[1:]):
        acc = acc + _dot(x_ref[...], w_ref[...])
    if epi == "plain":
        out = acc
    elif epi == "residual":
        out = acc + _pick_part(rest[:-1], res_starts, pl.program_id(0))
    else:
        raise ValueError(epi)
    o_ref[...] = out.astype(o_ref.dtype)


def _part_specs(parts, tm, block_cols, col_of):
    specs, starts, st = [], [], 0
    for p in parts:
        nb = p.shape[0] // tm
        specs.append(pl.BlockSpec((tm, block_cols), lambda i, *j, st=st, nb=nb: (jnp.clip(i - st, 0, nb - 1), col_of(*j))))
        starts.append(st)
        st += nb
    return specs, starts


def _matmul(xs, ws, out_dtype, epi="plain", extra=(), pos_blk=None, tm=1024, tn=1024):
    t = xs[0].shape[0]
    n = ws[0].shape[1]
    tm, tn = min(tm, t), min(tn, n)
    in_specs = [pl.BlockSpec((tm, x.shape[1]), lambda i, j: (i, 0)) for x in xs]
    in_specs += [pl.BlockSpec((w.shape[0], tn), lambda i, j: (0, j)) for w in ws]
    res_starts = None
    if epi == "residual":
        specs, res_starts = _part_specs(extra, tm, tn, lambda j: j)
        in_specs += specs
    elif epi in ("headnorm", "headnorm_rope"):
        in_specs.append(pl.BlockSpec((1, tn), lambda i, j: (0, j)))
        if epi == "headnorm_rope":
            in_specs += [pl.BlockSpec((tm, HEAD_DIM), lambda i, j: (pos_blk(i), 0))] * 2
    return pl.pallas_call(
        functools.partial(_mm_kernel, n_in=len(xs), epi=epi, res_starts=res_starts),
        grid=(t // tm, n // tn),
        in_specs=in_specs,
        out_specs=pl.BlockSpec((tm, tn), lambda i, j: (i, j)),
        out_shape=jax.ShapeDtypeStruct((t, n), out_dtype),
        compiler_params=_cparams("parallel", "parallel"),
        name="matmul_" + epi,
    )(*xs, *ws, *extra)


def _mlstm_kernel(qf_ref, kf_ref, vf_ref, gf_ref, qb_ref, kb_ref, vb_ref, gb_ref, bias_ref,
                  hf_ref, hb_ref, s_ref, m_ref, *, nh, nchunks, start_chunks, last_chunks):
    c = pl.program_id(0)
    cb = nchunks - 1 - c
    ch = MLSTM_CHUNK
    is_start = functools.reduce(jnp.logical_or, [c == s for s in start_chunks])
    is_last = functools.reduce(jnp.logical_or, [cb == e for e in last_chunks])

    def _reset(d):
        s_ref[d] = jnp.zeros((nh, HEAD_DIM, 2 * HEAD_DIM), F32)
        m_ref[d] = jnp.zeros((nh, 1, HEAD_DIM), F32)

    pl.when(is_start)(lambda: _reset(0))
    pl.when(is_last)(lambda: _reset(1))

    row = lax.broadcasted_iota(jnp.int32, (ch, ch), 0)
    col = lax.broadcasted_iota(jnp.int32, (ch, ch), 1)
    scale = HEAD_DIM ** -0.5
    heads = lambda ref: jnp.stack([ref[:, h * HEAD_DIM:(h + 1) * HEAD_DIM] for h in range(nh)])
    bdot = lambda a, b, ca, cb: lax.dot_general(a, b, (((ca,), (cb,)), ((0,), (0,))), preferred_element_type=F32)

    for d, (q_ref, k_ref, v_ref, g_ref, h_ref) in enumerate(
            ((qf_ref, kf_ref, vf_ref, gf_ref, hf_ref), (qb_ref, kb_ref, vb_ref, gb_ref, hb_ref))):
        mask = (row >= col) if d == 0 else (col >= row)
        icol0 = 2 * nh * d
        fcol0 = icol0 + nh
        tot_row = ch - 1 if d == 0 else 0
        g = g_ref[...] + bias_ref[...]
        lf = jnp.minimum(g, 0.0) - jnp.log1p(jnp.exp(-jnp.abs(g)))
        tri = jnp.where(mask, 1.0, 0.0).astype(BF16)
        hi, mid, lo = _split3(lf)
        bcum = _dot(tri, hi) + _dot(tri, mid) + _dot(tri, lo)
        a = g - pltpu.roll(bcum, HEAD_DIM - nh, 1)
        a_t = a.T
        bcol = jnp.stack([bcum[:, fcol0 + h:fcol0 + h + 1] for h in range(nh)])
        acol = jnp.stack([a[:, icol0 + h:icol0 + h + 1] for h in range(nh)])
        arow = jnp.stack([a_t[icol0 + h:icol0 + h + 1, :] for h in range(nh)])
        b_tot = bcol[:, tot_row:tot_row + 1, :]
        m_st = m_ref[d][:, :, 0:1]
        s_prev = s_ref[d]

        log_d = jnp.where(mask[None], bcol + arow, NEG)
        m_intra = jnp.max(log_d, axis=2, keepdims=True)
        log_inter = bcol + m_st
        m_row = jnp.maximum(log_inter, m_intra)
        dmat = jnp.exp(log_d - m_row)
        q = heads(q_ref)
        ks = (heads(k_ref).astype(F32) * scale).astype(BF16)
        v1 = jnp.concatenate([heads(v_ref), jnp.ones((nh, ch, HEAD_DIM), BF16)], axis=2)
        p = bdot(q, ks, 2, 2) * dmat
        w_inter = jnp.exp(log_inter - m_row)
        nd = bdot(p.astype(BF16), v1, 2, 1) + w_inter * bdot(q, s_prev.astype(BF16), 2, 1)
        h_out = nd[:, :, :HEAD_DIM] / jnp.maximum(jnp.abs(nd[:, :, HEAD_DIM:]), jnp.exp(-m_row))
        for h in range(nh):
            h_ref[:, h * HEAD_DIM:(h + 1) * HEAD_DIM] = h_out[h]

        m_kv = b_tot + jnp.max(arow, axis=2, keepdims=True)
        m_new = jnp.maximum(b_tot + m_st, m_kv)
        decay = jnp.exp(b_tot + m_st - m_new)
        wexp = jnp.exp(b_tot + acol - m_new)
        kw = ks.astype(F32) * wexp
        s_ref[d] = decay * s_prev + bdot(kw.astype(BF16), v1, 1, 1)
        m_ref[d] = jnp.broadcast_to(m_new, (nh, 1, HEAD_DIM))


def _mlstm(u_a, gates, gate_bias, seqs, nh):
    t = u_a.shape[0]
    ch = MLSTM_CHUNK
    nchunks = t // ch
    da = nh * HEAD_DIM
    start_chunks = [s // ch for s, _ in seqs]
    last_chunks = [(s + l) // ch - 1 for s, l in seqs]
    fwd = lambda j: pl.BlockSpec((ch, da), lambda c, j=j: (c, j))
    bwd = lambda j: pl.BlockSpec((ch, da), lambda c, j=j: (nchunks - 1 - c, j))
    gspec_f = pl.BlockSpec((ch, 128), lambda c: (c, 0))
    gspec_b = pl.BlockSpec((ch, 128), lambda c: (nchunks - 1 - c, 0))
    bias = jnp.zeros((1, 128), F32).at[0, :4 * nh].set(gate_bias.reshape(-1).astype(F32))
    return pl.pallas_call(
        functools.partial(_mlstm_kernel, nh=nh, nchunks=nchunks, start_chunks=start_chunks,
                          last_chunks=last_chunks),
        grid=(nchunks,),
        in_specs=[fwd(0), fwd(1), fwd(2), gspec_f, bwd(0), bwd(1), bwd(2), gspec_b,
                  pl.BlockSpec((1, 128), lambda c: (0, 0))],
        out_specs=[pl.BlockSpec((ch, da), lambda c: (c, 0)),
                   pl.BlockSpec((ch, da), lambda c: (nchunks - 1 - c, 0))],
        out_shape=[jax.ShapeDtypeStruct((t, da), F32)] * 2,
        scratch_shapes=[pltpu.VMEM((2, nh, HEAD_DIM, 2 * HEAD_DIM), F32),
                        pltpu.VMEM((2, nh, 1, HEAD_DIM), F32)],
        compiler_params=_cparams("arbitrary"),
        name="mlstm",
    )(u_a, u_a, u_a, gates, u_a, u_a, u_a, gates, bias)


def _mlstm_gate_kernel(hf_ref, hb_ref, o_ref, g_ref, y_ref):
    hsum = hf_ref[...] + hb_ref[...]
    o = o_ref[...].astype(F32)
    g = g_ref[...]
    outs = []
    for h in range(hsum.shape[1] // HEAD_DIM):
        hs = slice(h * HEAD_DIM, (h + 1) * HEAD_DIM)
        a = hsum[:, hs]
        y = a * lax.rsqrt(jnp.mean(a * a, axis=-1, keepdims=True) + EPS) * g[:, hs]
        outs.append(jax.nn.sigmoid(o[:, hs]) * y)
    y_ref[...] = jnp.concatenate(outs, axis=1).astype(y_ref.dtype)


def _mlstm_gate(h_f, h_b, u_a, hnorm_g, tm=512):
    t, da = h_f.shape
    return pl.pallas_call(
        _mlstm_gate_kernel,
        grid=(t // tm,),
        in_specs=[pl.BlockSpec((tm, da), lambda i: (i, 0)), pl.BlockSpec((tm, da), lambda i: (i, 0)),
                  pl.BlockSpec((tm, da), lambda i: (i, 3)), pl.BlockSpec((1, da), lambda i: (0, 0))],
        out_specs=pl.BlockSpec((tm, da), lambda i: (i, 0)),
        out_shape=jax.ShapeDtypeStruct((t, da), BF16),
        compiler_params=_cparams("parallel"),
        name="mlstm_gate",
    )(h_f, h_b, u_a, hnorm_g.reshape(1, da).astype(F32))


NA_QROWS = 4
NA_KROWS = NA_QROWS + NA_ROWS
NA_UNROLL = 2


def _na_bias_table(rel_bias):
    h = rel_bias.shape[0]
    nr, nc = 2 * NA_ROWS - 1, 2 * NA_COLS - 1
    c = np.arange(GRID_W)
    dc = np.clip(c[None, :] - c[:, None], -(NA_COLS - 1), NA_COLS - 1) + NA_COLS - 1
    c_start = np.clip(c - NA_COLS // 2, 0, GRID_W - NA_COLS)
    col_ok = (c[None, :] >= c_start[:, None]) & (c[None, :] < c_start[:, None] + NA_COLS)
    var = np.arange(3)[:, None, None]
    qi = np.arange(NA_QROWS)[None, :, None]
    kr = np.arange(NA_KROWS)[None, None, :]
    dr = kr - var * NA_QROWS - qi
    first = np.where(var == 0, 0, np.where(var == 1, qi, NA_QROWS))
    row_ok = (kr >= first) & (kr < first + NA_ROWS)
    dr_idx = np.clip(dr + NA_ROWS - 1, 0, nr - 1).reshape(-1)
    onehot = (dc.reshape(1, -1) == np.arange(nc)[:, None]).astype(np.float32)
    cols = jnp.dot(rel_bias.astype(F32).reshape(h * nr, nc), onehot, precision=lax.Precision.HIGHEST)
    cols = cols.reshape(h, nr, GRID_W, GRID_W)
    tbl = jnp.stack([cols[:, int(i)] for i in dr_idx], axis=1)
    tbl = tbl.reshape(h, 3, NA_QROWS, NA_KROWS, GRID_W, GRID_W)
    ok = row_ok[:, :, :, None, None] & col_ok[None, None, None, :, :]
    tbl = jnp.where(ok[None], tbl, NEG)
    return jnp.transpose(tbl, (0, 1, 2, 4, 3, 5)).reshape(h, 3, NA_QROWS * GRID_W, NA_KROWS * GRID_W)


def _na_kernel(q_ref, kp_ref, km_ref, kn_ref, vp_ref, vm_ref, vn_ref, bias_ref, o_ref, kcat, vcat,
               *, seg_rows, row_lo, row_hi):
    s = pl.program_id(1)
    halo = NA_ROWS * GRID_W
    seg = seg_rows * GRID_W
    kcat[0:halo] = kp_ref[...]
    kcat[halo:halo + seg] = km_ref[...]
    kcat[halo + seg:halo + seg + halo] = kn_ref[...]
    vcat[0:halo, 0:HEAD_DIM] = vp_ref[...]
    vcat[halo:halo + seg, 0:HEAD_DIM] = vm_ref[...]
    vcat[halo + seg:halo + seg + halo, 0:HEAD_DIM] = vn_ref[...]
    vcat[:, HEAD_DIM:] = jnp.ones((seg + 2 * halo, HEAD_DIM), BF16)
    r_lo = _seq_lookup(s, row_lo, 0)
    r_hi = _seq_lookup(s, row_hi, 0)
    scale = HEAD_DIM ** -0.5
    nq = NA_QROWS * GRID_W
    nk = NA_KROWS * GRID_W

    def one_group(gl):
        r0 = s * seg_rows + gl * NA_QROWS
        u = jnp.clip(r0 - NA_ROWS // 2, r_lo, r_hi - NA_KROWS)
        off = pl.multiple_of((u - s * seg_rows + NA_ROWS) * GRID_W, GRID_W)
        qoff = pl.multiple_of(gl * nq, nq)
        q = q_ref[pl.ds(qoff, nq), :]
        sc = _dot_nt(q, kcat[pl.ds(off, nk), :]) * scale + bias_ref[(r0 - u) // NA_QROWS]
        e = jnp.exp(sc - jnp.max(sc, axis=1, keepdims=True))
        nd = _dot(e.astype(BF16), vcat[pl.ds(off, nk), :])
        o_ref[pl.ds(qoff, nq), :] = (nd[:, :HEAD_DIM] / nd[:, HEAD_DIM:]).astype(o_ref.dtype)

    def body(it, carry):
        for k in range(NA_UNROLL):
            one_group(it * NA_UNROLL + k)
        return carry

    lax.fori_loop(0, seg_rows // (NA_QROWS * NA_UNROLL), body, 0)


def _neighbourhood_attention(qk, v, bias_tbl, seqs, seg):
    t, dh = v.shape
    nh = dh // HEAD_DIM
    nseg = t // seg
    seg_rows = seg // GRID_W
    assert all(l // GRID_W >= NA_KROWS and (s // GRID_W) % NA_QROWS == 0 and (l // GRID_W) % NA_QROWS == 0
               for s, l in seqs)
    halo = NA_ROWS * GRID_W
    hpb = seg // halo
    nhb = t // halo
    row_lo = [(s // seg, s // GRID_W) for s, _ in seqs]
    row_hi = [(s // seg, (s + l) // GRID_W) for s, l in seqs]
    main = lambda c0: pl.BlockSpec((seg, HEAD_DIM), lambda h, s: (s, c0 + h))
    prev = lambda c0: pl.BlockSpec((halo, HEAD_DIM), lambda h, s: (jnp.maximum(s * hpb - 1, 0), c0 + h))
    nxt = lambda c0: pl.BlockSpec((halo, HEAD_DIM), lambda h, s: (jnp.minimum((s + 1) * hpb, nhb - 1), c0 + h))
    return pl.pallas_call(
        functools.partial(_na_kernel, seg_rows=seg_rows, row_lo=row_lo, row_hi=row_hi),
        grid=(nh, nseg),
        in_specs=[main(0), prev(nh), main(nh), nxt(nh), prev(0), main(0), nxt(0),
                  pl.BlockSpec((None, 3, NA_QROWS * GRID_W, NA_KROWS * GRID_W), lambda h, s: (h, 0, 0, 0))],
        out_specs=main(0),
        out_shape=jax.ShapeDtypeStruct((t, dh), BF16),
        scratch_shapes=[pltpu.VMEM((seg + 2 * halo, HEAD_DIM), BF16), pltpu.VMEM((seg + 2 * halo, 2 * HEAD_DIM), BF16)],
        compiler_params=_cparams("parallel", "parallel"),
        name="natten",
    )(qk, qk, qk, qk, v, v, v, bias_tbl)


DIL_BLK = 128
DIL_RADIUS = 64
DIL_MAX = max(d for _, d in DIL_PAIRS)
DIL_QB = DIL_BLK * DIL_MAX
DIL_HALO = DIL_RADIUS * DIL_MAX


def _dilated_kernel(q_ref, kp_ref, kc_ref, kn_ref, vp_ref, vc_ref, vn_ref, y_ref, num_s, mx_s, den_s,
                    *, first_blocks, last_blocks):
    i = pl.program_id(1)
    has_prev = jnp.logical_not(functools.reduce(jnp.logical_or, [i == b for b in first_blocks]))
    has_next = jnp.logical_not(functools.reduce(jnp.logical_or, [i == b for b in last_blocks]))
    nk = DIL_BLK + 2 * DIL_RADIUS
    j_io = lax.broadcasted_iota(jnp.int32, (DIL_BLK, nk), 0)
    s_io = lax.broadcasted_iota(jnp.int32, (DIL_BLK, nk), 1)
    band = jnp.abs(s_io - DIL_RADIUS - j_io) <= DIL_RADIUS
    ok_prev = jnp.logical_or(s_io >= DIL_RADIUS, has_prev)
    ok_next = jnp.logical_or(s_io < DIL_BLK + DIL_RADIUS, has_next)
    madd = {}
    for up in (False, True):
        for un in (False, True):
            ok = band
            ok = jnp.logical_and(ok, ok_prev) if up else ok
            ok = jnp.logical_and(ok, ok_next) if un else ok
            madd[up, un] = jnp.where(ok, 0.0, NEG)
    scale = HEAD_DIM ** -0.5
    ones_v = jnp.ones((nk, HEAD_DIM), BF16)

    def rows(start, size, d):
        return pl.ds(start, size, stride=d) if d > 1 else pl.ds(start, size)

    for b, (_, d) in enumerate(DIL_PAIRS):
        per_class = DIL_QB // d
        nj = per_class // DIL_BLK
        for g in range(d):
            for j in range(nj):
                up, un = j == 0, j == nj - 1
                qrows = rows(g + d * DIL_BLK * j, DIL_BLK, d)
                m0 = max(DIL_BLK * j - DIL_RADIUS, 0)
                m1 = min(DIL_BLK * (j + 1) + DIL_RADIUS, per_class)
                kparts, vparts = [], []
                if up:
                    r = rows(DIL_HALO - DIL_RADIUS * d + g, DIL_RADIUS, d)
                    kparts.append(kp_ref[r, :])
                    vparts.append(vp_ref[r, :])
                r = rows(g + d * m0, m1 - m0, d)
                kparts.append(kc_ref[r, :])
                vparts.append(vc_ref[r, :])
                if un:
                    r = rows(g, DIL_RADIUS, d)
                    kparts.append(kn_ref[r, :])
                    vparts.append(vn_ref[r, :])
                kk = jnp.concatenate(kparts, axis=0).astype(BF16)
                vv = jnp.concatenate([jnp.concatenate(vparts, axis=0).astype(BF16), ones_v], axis=1)
                sc = _dot_nt(q_ref[qrows, :].astype(BF16), kk) * scale + madd[up, un]
                mx = jnp.max(sc, axis=1, keepdims=True)
                nd = _dot(jnp.exp(sc - mx).astype(BF16), vv)
                num_s[b, qrows, :] = nd[:, :HEAD_DIM]
                mx_s[b, qrows, :] = jnp.broadcast_to(mx, (DIL_BLK, HEAD_DIM))
                den_s[b, qrows, :] = nd[:, HEAD_DIM:]

    nb = len(DIL_PAIRS)
    mxs = [mx_s[b] for b in range(nb)]
    m_top = functools.reduce(jnp.maximum, mxs)
    w = [jnp.exp(m - m_top) for m in mxs]
    num = functools.reduce(jnp.add, [w[b] * num_s[b] for b in range(nb)])
    den = functools.reduce(jnp.add, [w[b] * den_s[b] for b in range(nb)])
    y_ref[...] = (num / den).astype(y_ref.dtype)


def _dilated_attention(qk, v, seqs):
    t, dh = v.shape
    nh = dh // HEAD_DIM
    assert all(window // (2 * dil) == DIL_RADIUS for window, dil in DIL_PAIRS)
    assert all(s % DIL_QB == 0 and l % DIL_QB == 0 for s, l in seqs)
    nq = t // DIL_QB
    hb = DIL_QB // DIL_HALO
    nhalo = t // DIL_HALO
    first_blocks = [s // DIL_QB for s, _ in seqs]
    last_blocks = [(s + l) // DIL_QB - 1 for s, l in seqs]
    cur = lambda c0: pl.BlockSpec((DIL_QB, HEAD_DIM), lambda h, i: (i, c0 + h))
    prev = lambda c0: pl.BlockSpec((DIL_HALO, HEAD_DIM), lambda h, i: (jnp.maximum(i * hb - 1, 0), c0 + h))
    nxt = lambda c0: pl.BlockSpec((DIL_HALO, HEAD_DIM), lambda h, i: (jnp.minimum((i + 1) * hb, nhalo - 1), c0 + h))
    nb = len(DIL_PAIRS)
    return pl.pallas_call(
        functools.partial(_dilated_kernel, first_blocks=first_blocks, last_blocks=last_blocks),
        grid=(nh, nq),
        in_specs=[cur(0), prev(nh), cur(nh), nxt(nh), prev(0), cur(0), nxt(0)],
        out_specs=cur(0),
        out_shape=jax.ShapeDtypeStruct((t, dh), BF16),
        scratch_shapes=[pltpu.VMEM((nb, DIL_QB, HEAD_DIM), F32)] * 3,
        compiler_params=_cparams("parallel", "parallel"),
        name="dilated",
    )(qk, qk, qk, qk, v, v, v)


def _rope_tables(max_len):
    half = ROPE_DIM // 2
    inv_freq = ROPE_THETA ** (-jnp.arange(half, dtype=F32) / half)
    ang = jnp.arange(max_len, dtype=F32)[:, None] * inv_freq[None, :]
    cos, sin = jnp.cos(ang), jnp.sin(ang)
    rest = HEAD_DIM - ROPE_DIM
    c = jnp.concatenate([cos, cos, jnp.ones((max_len, rest), F32)], axis=1)
    s = jnp.concatenate([-sin, sin, jnp.zeros((max_len, rest), F32)], axis=1)
    return c, s


def _router_kernel(x_ref, g_ref, wr_ref, rb_ref, xn_ref, e_ref, gt_ref, *, n_exp):
    x = x_ref[...]
    xn = x * lax.rsqrt(jnp.mean(x * x, axis=-1, keepdims=True) + EPS) * g_ref[...]
    xn_ref[...] = xn.astype(xn_ref.dtype)
    tm = x.shape[0]
    epg = n_exp // N_GROUPS
    xh, xm, _ = _split3(xn)
    wh, wm, _ = _split3(wr_ref[...])
    logits = _dot_nt(wh, xh) + _dot_nt(wh, xm) + _dot_nt(wm, xh)
    scores = jax.nn.sigmoid(logits)
    sel = (scores + rb_ref[:, 0:1]).reshape(N_GROUPS, epg, tm)
    sc3 = scores.reshape(N_GROUPS, epg, tm)
    io = lax.broadcasted_iota(jnp.int32, (N_GROUPS, epg, tm), 1)
    gio = lax.broadcasted_iota(jnp.int32, (N_GROUPS, 1, tm), 0)
    m1 = jnp.max(sel, axis=1, keepdims=True)
    i1 = jnp.min(jnp.where(sel == m1, io, epg), axis=1, keepdims=True)
    sel_b = jnp.where(io == i1, -jnp.inf, sel)
    m2 = jnp.max(sel_b, axis=1, keepdims=True)
    i2 = jnp.min(jnp.where(sel_b == m2, io, epg), axis=1, keepdims=True)
    gs = m1 + m2
    gbest = jnp.min(jnp.where(gs == jnp.max(gs, axis=0, keepdims=True), gio, N_GROUPS), axis=0, keepdims=True)
    in_g = gio == gbest
    outs_e, outs_w = [], []
    for ik in (i1, i2):
        pick = in_g & (io == ik)
        outs_e.append(jnp.sum(jnp.where(in_g, gbest * epg + ik, 0), axis=0))
        outs_w.append(jnp.sum(jnp.sum(jnp.where(pick, sc3, 0.0), axis=1, keepdims=True), axis=0))
    wsum = outs_w[0] + outs_w[1]
    e_ref[0:1, :] = outs_e[0]
    e_ref[1:2, :] = outs_e[1]
    gt_ref[0:1, :] = outs_w[0] / wsum
    gt_ref[1:2, :] = outs_w[1] / wsum


def _router(x, g, w_router, router_bias, tm=512):
    t, d = x.shape
    n_exp = w_router.shape[1]
    rb = jnp.broadcast_to(router_bias.astype(F32)[:, None], (n_exp, 128))
    return pl.pallas_call(
        functools.partial(_router_kernel, n_exp=n_exp),
        grid=(t // tm,),
        in_specs=[pl.BlockSpec((tm, d), lambda i: (i, 0)), pl.BlockSpec((1, d), lambda i: (0, 0)),
                  pl.BlockSpec((n_exp, d), lambda i: (0, 0)), pl.BlockSpec((n_exp, 128), lambda i: (0, 0))],
        out_specs=[pl.BlockSpec((tm, d), lambda i: (i, 0)), pl.BlockSpec((2, tm), lambda i: (0, i)),
                   pl.BlockSpec((2, tm), lambda i: (0, i))],
        out_shape=[jax.ShapeDtypeStruct((t, d), F32), jax.ShapeDtypeStruct((2, t), jnp.int32),
                   jax.ShapeDtypeStruct((2, t), F32)],
        compiler_params=_cparams("parallel"),
        name="router",
    )(x, g.reshape(1, d).astype(F32), w_router.T.astype(F32), rb)


def _rank_kernel(e_ref, rank_ref, cnt_ref, run_ref, *, n_exp):
    i = pl.program_id(0)
    tm = e_ref.shape[1]

    @pl.when(i == 0)
    def _():
        run_ref[...] = jnp.zeros_like(run_ref)

    eio = lax.broadcasted_iota(jnp.int32, (n_exp, tm), 0)
    oh = [(eio == e_ref[k:k + 1, :]) for k in range(TOP_K)]
    ohf = [jnp.where(o, 1.0, 0.0) for o in oh]
    both = ohf[0] + ohf[1]
    r_i = lax.broadcasted_iota(jnp.int32, (tm, tm), 0)
    c_i = lax.broadcasted_iota(jnp.int32, (tm, tm), 1)
    upper = jnp.where(r_i <= c_i, 1.0, 0.0).astype(BF16)
    cum = _dot(both.astype(BF16), upper)
    base = run_ref[:, 0:1] + cum - both
    for k in range(TOP_K):
        rank_ref[k:k + 1, :] = jnp.sum(jnp.where(oh[k], base, 0.0), axis=0, keepdims=True).astype(jnp.int32)
    run_new = run_ref[...] + cum[:, tm - 1:tm]
    run_ref[...] = run_new
    cnt_ref[...] = run_new.astype(jnp.int32)


def _rank(e, n_exp, tm=512):
    t = e.shape[1]
    return pl.pallas_call(
        functools.partial(_rank_kernel, n_exp=n_exp),
        grid=(t // tm,),
        in_specs=[pl.BlockSpec((2, tm), lambda i: (0, i))],
        out_specs=[pl.BlockSpec((2, tm), lambda i: (0, i)), pl.BlockSpec((n_exp, 128), lambda i: (0, 0))],
        out_shape=[jax.ShapeDtypeStruct((2, t), jnp.int32), jax.ShapeDtypeStruct((n_exp, 128), jnp.int32)],
        scratch_shapes=[pltpu.VMEM((n_exp, 128), F32)],
        compiler_params=_cparams("arbitrary"),
        name="moe_rank",
    )(e)


MOE_BLK = 512


CAST_ROWS = 256


ROW_UNROLL = 8


def _start_rows(n_rows, make_copy):
    def body(r, c):
        for k in range(TOP_K):
            make_copy(k, r).start(priority=k)
        return c
    lax.fori_loop(0, n_rows, body, 0, unroll=ROW_UNROLL)


def _wait_rows(n_rows, make_copy):
    def body(r, c):
        for k in range(TOP_K):
            make_copy(k, 0).wait()
        return c
    lax.fori_loop(0, n_rows, body, 0, unroll=ROW_UNROLL)


def _dispatch_kernel(fill_ref, ps_ref, e_ref, rank_ref, xn_ref, xs_out, zbuf, sem, zsem, *, tm, n_fill):
    i = pl.program_id(0)

    @pl.when(i == 0)
    def _():
        zbuf[...] = jnp.zeros_like(zbuf)

        def fill(j):
            rows = pl.ds(pl.multiple_of(fill_ref[j] * MOE_BLK, MOE_BLK), MOE_BLK)
            return pltpu.make_async_copy(zbuf, xs_out.at[rows, :], zsem)

        def start(j, c):
            pl.when(fill_ref[n_fill + j] > 0)(lambda: fill(j).start())
            return c

        def wait(j, c):
            pl.when(fill_ref[n_fill + j] > 0)(lambda: fill(j).wait())
            return c

        lax.fori_loop(0, n_fill, start, 0)
        lax.fori_loop(0, n_fill, wait, 0)

    def copy(k, r):
        slot = ps_ref[e_ref[k, r]] + rank_ref[k, r]
        return pltpu.make_async_copy(xn_ref.at[pl.ds(r, 1), :], xs_out.at[pl.ds(slot, 1), :], sem)

    _start_rows(tm, copy)
    _wait_rows(tm, copy)


def _dispatch(e, rank, pad_start, xn, fill, cap, tm=256):
    t, d = xn.shape
    smem = pl.BlockSpec((TOP_K, tm), lambda i, f, ps: (0, i), memory_space=pltpu.SMEM)
    return pl.pallas_call(
        functools.partial(_dispatch_kernel, tm=tm, n_fill=fill.shape[0] // 2),
        grid_spec=pltpu.PrefetchScalarGridSpec(
            num_scalar_prefetch=2, grid=(t // tm,),
            in_specs=[smem, smem, pl.BlockSpec((tm, d), lambda i, f, ps: (i, 0))],
            out_specs=pl.BlockSpec(memory_space=pl.ANY),
            scratch_shapes=[pltpu.VMEM((MOE_BLK, d), F32), pltpu.SemaphoreType.DMA, pltpu.SemaphoreType.DMA]),
        out_shape=jax.ShapeDtypeStruct((cap, d), F32),
        compiler_params=_cparams("arbitrary"),
        name="moe_dispatch",
    )(fill, pad_start, e, rank, xn)


def _dot_casting(x, w_refs, s_refs):
    k = x.shape[1]
    step = min(CAST_ROWS, k)
    accs = [None] * len(w_refs)
    for r0 in range(0, k, step):
        rows = slice(r0, r0 + step)
        for n, (w_ref, s_ref) in enumerate(zip(w_refs, s_refs)):
            wc = w_ref[rows, :].astype(BF16)
            s_ref[rows, :] = wc
            part = _dot(x[:, rows], wc)
            accs[n] = part if accs[n] is None else accs[n] + part
    return accs


def _expert_changed(b, be_ref):
    return jnp.logical_or(b == 0, be_ref[b] != be_ref[jnp.maximum(b - 1, 0)])


def _expert_mm_kernel(be_ref, nb_ref, nx_ref, x_ref, *rest, layer, n_w, finish):
    w_hbm, o_ref, (wf, ws, sem) = rest[:n_w], rest[n_w], rest[n_w + 1:]
    b = pl.program_id(0)
    live = b < nb_ref[0]
    first = _expert_changed(b, be_ref)

    def fetch(e):
        return [pltpu.make_async_copy(w.at[layer, e], wf.at[n], sem.at[n]) for n, w in enumerate(w_hbm)]

    @pl.when(b == 0)
    def _():
        for c in fetch(be_ref[0]):
            c.start()

    @pl.when(jnp.logical_and(live, first))
    def _():
        for c in fetch(be_ref[b]):
            c.wait()
        outs = _dot_casting(x_ref[...].astype(BF16), [wf.at[n] for n in range(n_w)], [ws.at[n] for n in range(n_w)])
        o_ref[...] = finish(outs).astype(o_ref.dtype)

        @pl.when(nx_ref[b] >= 0)
        def _():
            for c in fetch(nx_ref[b]):
                c.start()

    @pl.when(jnp.logical_and(live, jnp.logical_not(first)))
    def _():
        x = x_ref[...].astype(BF16)
        o_ref[...] = finish([_dot(x, ws[n]) for n in range(n_w)]).astype(o_ref.dtype)

    @pl.when(jnp.logical_not(live))
    def _():
        o_ref[...] = jnp.zeros_like(o_ref)


def _swiglu(outs):
    hg, hu = outs
    return (hg * jax.nn.sigmoid(hg)) * hu


def _expert_mm(xs, sched, layer, ws_hbm, finish, out_dtype, name):
    blk_expert, n_used, next_expert = sched
    cap, k = xs.shape
    n = ws_hbm[0].shape[3]
    n_w = len(ws_hbm)
    return pl.pallas_call(
        functools.partial(_expert_mm_kernel, layer=layer, n_w=n_w, finish=finish),
        grid_spec=pltpu.PrefetchScalarGridSpec(
            num_scalar_prefetch=3, grid=(cap // MOE_BLK,),
            in_specs=[pl.BlockSpec((MOE_BLK, k), lambda b, be, nb, nx: (jnp.minimum(b, nb[0] - 1), 0))]
            + [pl.BlockSpec(memory_space=pl.ANY)] * n_w,
            out_specs=pl.BlockSpec((MOE_BLK, n), lambda b, be, nb, nx: (b, 0)),
            scratch_shapes=[pltpu.VMEM((n_w, k, n), F32), pltpu.VMEM((n_w, k, n), BF16),
                            pltpu.SemaphoreType.DMA((n_w,))]),
        out_shape=jax.ShapeDtypeStruct((cap, n), out_dtype),
        compiler_params=_cparams("arbitrary"),
        name=name,
    )(blk_expert, n_used, next_expert, xs, *ws_hbm)


def _expert_ffn(xs, sched, layer, w_gate, w_up, w_down):
    h = _expert_mm(xs, sched, layer, (w_gate, w_up), _swiglu, BF16, "moe_ffn_up")
    return _expert_mm(h, sched, layer, (w_down,), lambda outs: outs[0], F32, "moe_ffn_down")


def _combine_kernel(ps_ref, e_ref, rank_ref, e_nx, rank_nx, x_ref, gt_ref, ys_hbm, *rest, tm, nsteps,
                    out_starts, out_nblk):
    o_refs, (buf, sem) = rest[:-2], rest[-2:]
    i = pl.program_id(0)
    cur = lax.rem(i, 2)

    def gather(eref, rref, b):
        def copy(k, r):
            slot = ps_ref[eref[k, r]] + rref[k, r]
            return pltpu.make_async_copy(ys_hbm.at[pl.ds(slot, 1), :], buf.at[b, k, pl.ds(r, 1), :], sem.at[b])
        return copy

    pl.when(i == 0)(lambda: _start_rows(tm, gather(e_ref, rank_ref, 0)))
    pl.when(i + 1 < nsteps)(lambda: _start_rows(tm, gather(e_nx, rank_nx, 1 - cur)))
    _wait_rows(tm, gather(e_ref, rank_ref, cur))
    out = x_ref[...] + (gt_ref[:, 0:1] * buf[cur, 0] + gt_ref[:, 1:2] * buf[cur, 1])
    for o_ref, st, nb in zip(o_refs, out_starts, out_nblk):
        @pl.when(jnp.logical_and(i >= st, i < st + nb))
        def _(o_ref=o_ref):
            o_ref[...] = out


def _combine(e, rank, pad_start, x, gt_cols, ys, out_rows, tm=256):
    t, d = x.shape
    nsteps = t // tm
    smem = lambda nxt: pl.BlockSpec((TOP_K, tm), lambda i, ps: (0, jnp.minimum(i + nxt, nsteps - 1)),
                                    memory_space=pltpu.SMEM)
    out_nblk = [r // tm for r in out_rows]
    out_starts = [sum(out_nblk[:n]) for n in range(len(out_nblk))]
    out_specs = [pl.BlockSpec((tm, d), lambda i, ps, st=st, nb=nb: (jnp.clip(i - st, 0, nb - 1), 0))
                 for st, nb in zip(out_starts, out_nblk)]
    return pl.pallas_call(
        functools.partial(_combine_kernel, tm=tm, nsteps=nsteps, out_starts=out_starts, out_nblk=out_nblk),
        grid_spec=pltpu.PrefetchScalarGridSpec(
            num_scalar_prefetch=1, grid=(nsteps,),
            in_specs=[smem(0), smem(0), smem(1), smem(1),
                      pl.BlockSpec((tm, d), lambda i, ps: (i, 0)), pl.BlockSpec((tm, TOP_K), lambda i, ps: (i, 0)),
                      pl.BlockSpec(memory_space=pl.ANY)],
            out_specs=out_specs,
            scratch_shapes=[pltpu.VMEM((2, TOP_K, tm, d), F32), pltpu.SemaphoreType.DMA((2,))]),
        out_shape=[jax.ShapeDtypeStruct((r, d), F32) for r in out_rows],
        compiler_params=_cparams("arbitrary"),
        name="moe_combine",
    )(pad_start, e, rank, e, rank, x, gt_cols, ys)


def _moe(x, ln_g, w_router, router_bias, layer, w_gate, w_up, w_down, out_rows):
    t, d = x.shape
    n_exp = w_router.shape[1]
    xn, e, gt = _router(x, ln_g, w_router, router_bias)
    rank, cnt = _rank(e, n_exp)
    counts = cnt[:, 0]
    padded = (counts + MOE_BLK - 1) // MOE_BLK * MOE_BLK
    pad_end = jnp.cumsum(padded)
    pad_start = (pad_end - padded).astype(jnp.int32)
    cap = t * TOP_K + n_exp * MOE_BLK
    nblk = cap // MOE_BLK
    blk_first = jnp.arange(nblk, dtype=jnp.int32) * MOE_BLK
    blk_expert = jnp.minimum(jnp.sum(pad_end[None, :] <= blk_first[:, None], axis=1), n_exp - 1).astype(jnp.int32)
    n_used = (pad_end[-1] // MOE_BLK).astype(jnp.int32).reshape(1)
    eid = jnp.arange(n_exp, dtype=jnp.int32)
    later = jnp.logical_and(padded[None, :] > 0, eid[None, :] > eid[:, None])
    next_of = jnp.min(jnp.where(later, eid[None, :], n_exp), axis=1)
    next_of = jnp.where(next_of < n_exp, next_of, -1)
    next_expert = jnp.sum(jnp.where(blk_expert[:, None] == eid[None, :], next_of[None, :], 0), axis=1).astype(jnp.int32)
    sched = (blk_expert, n_used, next_expert)
    trail = n_used[0] + jnp.arange(n_exp, dtype=jnp.int32)
    fill_blk = jnp.concatenate([pad_end // MOE_BLK - 1, jnp.minimum(trail, nblk - 1)])
    fill_ok = jnp.concatenate([padded > 0, trail < nblk])
    fill = jnp.concatenate([jnp.maximum(fill_blk, 0), fill_ok.astype(jnp.int32)]).astype(jnp.int32)
    xs = _dispatch(e, rank, pad_start, xn, fill, cap)
    ys = _expert_ffn(xs, sched, layer, w_gate, w_up, w_down)
    return tuple(_combine(e, rank, pad_start, x, gt.T, ys, out_rows))


def _ab_layer(xp, seqs, seg, ln_g, w_in, gate_bias, hnorm_g, qnorm_g, knorm_g, rel_bias, w_out):
    nh_a = gate_bias.shape[1]
    nh_b = rel_bias.shape[0]
    d_a, d_b = nh_a * HEAD_DIM, nh_b * HEAD_DIM
    xn = _rmsnorm(xp, ln_g)
    w = w_in.astype(BF16)
    g0 = 4 * d_a
    g1 = g0 + 4 * nh_a
    u_a = _matmul([xn], [w[:, :g0]], BF16)
    w_g = jnp.zeros((w.shape[0], 128), BF16).at[:, :4 * nh_a].set(w[:, g0:g1])
    gates = _matmul([xn], [w_g], F32)
    gain = jnp.concatenate([jnp.tile(qnorm_g, nh_b), jnp.tile(knorm_g, nh_b)]).reshape(1, -1).astype(F32)
    qk_b = _matmul([xn], [w[:, g1:g1 + 2 * d_b]], BF16, epi="headnorm", extra=(gain,), tn=1024)
    v_b = _matmul([xn], [w[:, g1 + 2 * d_b:]], BF16)
    h_f, h_b = _mlstm(u_a, gates, gate_bias, seqs, nh_a)
    y_a = _mlstm_gate(h_f, h_b, u_a, hnorm_g)
    y_b = _neighbourhood_attention(qk_b, v_b, _na_bias_table(rel_bias), seqs, seg)
    wo = w_out.astype(BF16)
    return _matmul([y_a, y_b], [wo[:d_a], wo[d_a:]], F32, epi="residual", extra=xp)


def _dilated_layer(xp, seqs, ln_g, w_in, qnorm_g, knorm_g, w_out):
    d_c = w_in.shape[1] // 3
    nh = d_c // HEAD_DIM
    xn = _rmsnorm(xp, ln_g)
    w = w_in.astype(BF16)
    tm = 1024
    max_len = max(l for _, l in seqs)
    tables = _rope_tables(max_len)
    pos_tbl = [(s // tm, s // tm) for s, _ in seqs]
    pos_blk = lambda i: i - _seq_lookup(i, pos_tbl, 0)
    gain = jnp.concatenate([jnp.tile(qnorm_g, nh), jnp.tile(knorm_g, nh)]).reshape(1, -1).astype(F32)
    qk = _matmul([xn], [w[:, :2 * d_c]], F32, epi="headnorm_rope", extra=(gain,) + tables,
                 pos_blk=pos_blk, tm=tm, tn=1024)
    v = _matmul([xn], [w[:, 2 * d_c:]], F32, tm=tm)
    y = _dilated_attention(qk, v, seqs)
    return _matmul([y], [w_out.astype(BF16)], F32, epi="residual", extra=xp)


def kernel(x_prompt, x_sample, ab_ln, ab_w_in, ab_gate_bias, ab_hnorm, ab_qnorm, ab_knorm, ab_relbias, ab_w_out,
           c_ln, c_w_in, c_qnorm, c_knorm, c_w_out, ffn_ln, w_router, router_bias, w_gate, w_up, w_down):
    bp, lp, d = x_prompt.shape
    bs, ls, _ = x_sample.shape
    tp = bp * lp
    seqs = [(b * lp, lp) for b in range(bp)] + [(tp + b * ls, ls) for b in range(bs)]
    seg = math.gcd(lp, ls)
    xp = (x_prompt.reshape(tp, d), x_sample.reshape(bs * ls, d))
    depth = ffn_ln.shape[0]
    for layer in range(depth):
        j = layer // 2
        if layer % 2 == 0:
            x = _ab_layer(xp, seqs, seg, ab_ln[j], ab_w_in[j], ab_gate_bias[j], ab_hnorm[j].reshape(-1),
                          ab_qnorm[j], ab_knorm[j], ab_relbias[j], ab_w_out[j])
        else:
            x = _dilated_layer(xp, seqs, c_ln[j], c_w_in[j], c_qnorm[j], c_knorm[j], c_w_out[j])
        xp = _moe(x, ffn_ln[layer], w_router, router_bias, layer, w_gate, w_up, w_down,
                  out_rows=(tp, bs * ls) if layer == depth - 1 else (tp + bs * ls,))
    return xp[0].reshape(bp, lp, d), xp[1].reshape(bs, ls, d)
```

```python
import functools
import math

import jax
import jax.numpy as jnp
import numpy as np
from jax import lax
from jax.experimental import pallas as pl
from jax.experimental.pallas import tpu as pltpu

HEAD_DIM = 128
MLSTM_CHUNK = 128
GRID_W = 64
NA_ROWS = 8
NA_COLS = 16
DIL_PAIRS = ((128, 1), (512, 4), (2048, 16))
ROPE_THETA = 500000.0
ROPE_DIM = HEAD_DIM // 4
N_GROUPS = 4
TOP_K = 2
EPS = 1e-6
NEG = -1e30
VMEM_LIMIT_BYTES = 56 * 1024 * 1024

F32 = jnp.float32
BF16 = jnp.bfloat16


def _cparams(*sem):
    return pltpu.CompilerParams(dimension_semantics=sem, vmem_limit_bytes=VMEM_LIMIT_BYTES)


def _dot(a, b):
    return jnp.dot(a, b, preferred_element_type=F32)


def _dot_nt(a, b):
    return lax.dot_general(a, b, (((1,), (1,)), ((), ())), preferred_element_type=F32)


def _dot_tn(a, b):
    return lax.dot_general(a, b, (((0,), (0,)), ((), ())), preferred_element_type=F32)


def _split3(x):
    hi = x.astype(BF16)
    r1 = x - hi.astype(F32)
    mid = r1.astype(BF16)
    lo = (r1 - mid.astype(F32)).astype(BF16)
    return hi, mid, lo


def _seq_lookup(u, table, default):
    out = default
    for thr, val in table:
        out = jnp.where(u >= thr, val, out)
    return out


def _rmsnorm_kernel(*refs, starts):
    g_ref, o_ref = refs[-2:]
    x = _pick_part(refs[:-2], starts, pl.program_id(0))
    ms = jnp.mean(x * x, axis=-1, keepdims=True)
    o_ref[...] = (x * lax.rsqrt(ms + EPS) * g_ref[...]).astype(o_ref.dtype)


def _rmsnorm(parts, g, tm=512):
    d = parts[0].shape[1]
    t = sum(p.shape[0] for p in parts)
    specs, starts = _part_specs(parts, tm, d, lambda: 0)
    return pl.pallas_call(
        functools.partial(_rmsnorm_kernel, starts=starts),
        grid=(t // tm,),
        in_specs=specs + [pl.BlockSpec((1, d), lambda i: (0, 0))],
        out_specs=pl.BlockSpec((tm, d), lambda i: (i, 0)),
        out_shape=jax.ShapeDtypeStruct((t, d), BF16),
        compiler_params=_cparams("parallel"),
        name="rmsnorm",
    )(*parts, g.reshape(1, d).astype(F32))


EPI_COLS = 256


def _head_mats(n):
    k = lax.broadcasted_iota(jnp.int32, (n, n), 0)
    i = lax.broadcasted_iota(jnp.int32, (n, n), 1)
    shift = HEAD_DIM.bit_length() - 1
    same = jnp.right_shift(k, shift) == jnp.right_shift(i, shift)
    kl, il = jnp.bitwise_and(k, HEAD_DIM - 1), jnp.bitwise_and(i, HEAD_DIM - 1)
    half = ROPE_DIM // 2
    partner = jnp.where(il < half, il + half, jnp.where(il < 2 * half, il - half, -1))
    ones = jnp.where(same, 1.0, 0.0).astype(BF16)
    swap = jnp.where(jnp.logical_and(same, kl == partner), 1.0, 0.0).astype(BF16)
    return ones, swap


def _head_epilogue(acc, gain, rope, mats):
    ones, swap = mats
    y = acc * lax.rsqrt(_dot((acc * acc).astype(BF16), ones) * (1.0 / HEAD_DIM) + EPS) * gain
    if rope is not None:
        reps = acc.shape[1] // HEAD_DIM
        c, s = (jnp.concatenate([t] * reps, axis=1) for t in rope)
        y = y * c + _dot(y.astype(BF16), swap) * s
    return y


def _pick_part(refs, starts, i):
    val = refs[0][...]
    for ref, st in zip(refs[1:], starts[1:]):
        val = jnp.where(i >= st, ref[...], val)
    return val


def _mm_kernel(*refs, n_in, epi, res_starts):
    xs, ws, rest = refs[:n_in], refs[n_in:2 * n_in], refs[2 * n_in:]
    o_ref = rest[-1]
    if epi in ("headnorm", "headnorm_rope"):
        rope = (rest[1][...], rest[2][...]) if epi == "headnorm_rope" else None
        mats = _head_mats(EPI_COLS)
        x = xs[0][...]
        for c0 in range(0, o_ref.shape[1], 2 * EPI_COLS):
            acc = _dot(x, ws[0][:, c0:c0 + 2 * EPI_COLS])
            for h0 in range(0, acc.shape[1], EPI_COLS):
                cs = slice(c0 + h0, c0 + h0 + EPI_COLS)
                o_ref[:, cs] = _head_epilogue(acc[:, h0:h0 + EPI_COLS], rest[0][:, cs], rope, mats).astype(o_ref.dtype)
        return
    acc = _dot(xs[0][...], ws[0][...])
    for x_ref, w_ref in zip(xs[1:], ws[1:]):
        acc = acc + _dot(x_ref[...], w_ref[...])
    if epi == "plain":
        out = acc
    elif epi == "residual":
        out = acc + _pick_part(rest[:-1], res_starts, pl.program_id(0))
    else:
        raise ValueError(epi)
    o_ref[...] = out.astype(o_ref.dtype)


def _part_specs(parts, tm, block_cols, col_of):
    specs, starts, st = [], [], 0
    for p in parts:
        nb = p.shape[0] // tm
        specs.append(pl.BlockSpec((tm, block_cols), lambda i, *j, st=st, nb=nb: (jnp.clip(i - st, 0, nb - 1), col_of(*j))))
        starts.append(st)
        st += nb
    return specs, starts


def _matmul(xs, ws, out_dtype, epi="plain", extra=(), pos_blk=None, tm=1024, tn=1024):
    t = xs[0].shape[0]
    n = ws[0].shape[1]
    tm, tn = min(tm, t), min(tn, n)
    in_specs = [pl.BlockSpec((tm, x.shape[1]), lambda i, j: (i, 0)) for x in xs]
    in_specs += [pl.BlockSpec((w.shape[0], tn), lambda i, j: (0, j)) for w in ws]
    res_starts = None
    if epi == "residual":
        specs, res_starts = _part_specs(extra, tm, tn, lambda j: j)
        in_specs += specs
    elif epi in ("headnorm", "headnorm_rope"):
        in_specs.append(pl.BlockSpec((1, tn), lambda i, j: (0, j)))
        if epi == "headnorm_rope":
            in_specs += [pl.BlockSpec((tm, HEAD_DIM), lambda i, j: (pos_blk(i), 0))] * 2
    return pl.pallas_call(
        functools.partial(_mm_kernel, n_in=len(xs), epi=epi, res_starts=res_starts),
        grid=(t // tm, n // tn),
        in_specs=in_specs,
        out_specs=pl.BlockSpec((tm, tn), lambda i, j: (i, j)),
        out_shape=jax.ShapeDtypeStruct((t, n), out_dtype),
        compiler_params=_cparams("parallel", "parallel"),
        name="matmul_" + epi,
    )(*xs, *ws, *extra)


def _mlstm_kernel(qf_ref, kf_ref, vf_ref, gf_ref, qb_ref, kb_ref, vb_ref, gb_ref, bias_ref,
                  hf_ref, hb_ref, s_ref, m_ref, *, nh, nchunks, start_chunks, last_chunks):
    c = pl.program_id(0)
    cb = nchunks - 1 - c
    ch = MLSTM_CHUNK
    is_start = functools.reduce(jnp.logical_or, [c == s for s in start_chunks])
    is_last = functools.reduce(jnp.logical_or, [cb == e for e in last_chunks])

    def _reset(d):
        s_ref[d] = jnp.zeros((nh, HEAD_DIM, 2 * HEAD_DIM), F32)
        m_ref[d] = jnp.zeros((nh, 1, HEAD_DIM), F32)

    pl.when(is_start)(lambda: _reset(0))
    pl.when(is_last)(lambda: _reset(1))

    row = lax.broadcasted_iota(jnp.int32, (ch, ch), 0)
    col = lax.broadcasted_iota(jnp.int32, (ch, ch), 1)
    scale = HEAD_DIM ** -0.5
    heads = lambda ref: jnp.stack([ref[:, h * HEAD_DIM:(h + 1) * HEAD_DIM] for h in range(nh)])
    bdot = lambda a, b, ca, cb: lax.dot_general(a, b, (((ca,), (cb,)), ((0,), (0,))), preferred_element_type=F32)

    for d, (q_ref, k_ref, v_ref, g_ref, h_ref) in enumerate(
            ((qf_ref, kf_ref, vf_ref, gf_ref, hf_ref), (qb_ref, kb_ref, vb_ref, gb_ref, hb_ref))):
        mask = (row >= col) if d == 0 else (col >= row)
        icol0 = 2 * nh * d
        fcol0 = icol0 + nh
        tot_row = ch - 1 if d == 0 else 0
        g = g_ref[...] + bias_ref[...]
        lf = jnp.minimum(g, 0.0) - jnp.log1p(jnp.exp(-jnp.abs(g)))
        tri = jnp.where(mask, 1.0, 0.0).astype(BF16)
        hi, mid, lo = _split3(lf)
        bcum = _dot(tri, hi) + _dot(tri, mid) + _dot(tri, lo)
        a = g - pltpu.roll(bcum, HEAD_DIM - nh, 1)
        a_t = a.T
        bcol = jnp.stack([bcum[:, fcol0 + h:fcol0 + h + 1] for h in range(nh)])
        acol = jnp.stack([a[:, icol0 + h:icol0 + h + 1] for h in range(nh)])
        arow = jnp.stack([a_t[icol0 + h:icol0 + h + 1, :] for h in range(nh)])
        b_tot = bcol[:, tot_row:tot_row + 1, :]
        m_st = m_ref[d][:, :, 0:1]
        s_prev = s_ref[d]

        log_d = jnp.where(mask[None], bcol + arow, NEG)
        m_intra = jnp.max(log_d, axis=2, keepdims=True)
        log_inter = bcol + m_st
        m_row = jnp.maximum(log_inter, m_intra)
        dmat = jnp.exp(log_d - m_row)
        q = heads(q_ref)
        ks = (heads(k_ref).astype(F32) * scale).astype(BF16)
        v1 = jnp.concatenate([heads(v_ref), jnp.ones((nh, ch, HEAD_DIM), BF16)], axis=2)
        p = bdot(q, ks, 2, 2) * dmat
        w_inter = jnp.exp(log_inter - m_row)
        nd = bdot(p.astype(BF16), v1, 2, 1) + w_inter * bdot(q, s_prev.astype(BF16), 2, 1)
        h_out = nd[:, :, :HEAD_DIM] / jnp.maximum(jnp.abs(nd[:, :, HEAD_DIM:]), jnp.exp(-m_row))
        for h in range(nh):
            h_ref[:, h * HEAD_DIM:(h + 1) * HEAD_DIM] = h_out[h]

        m_kv = b_tot + jnp.max(arow, axis=2, keepdims=True)
        m_new = jnp.maximum(b_tot + m_st, m_kv)
        decay = jnp.exp(b_tot + m_st - m_new)
        wexp = jnp.exp(b_tot + acol - m_new)
        kw = ks.astype(F32) * wexp
        s_ref[d] = decay * s_prev + bdot(kw.astype(BF16), v1, 1, 1)
        m_ref[d] = jnp.broadcast_to(m_new, (nh, 1, HEAD_DIM))


def _mlstm(u_a, gates, gate_bias, seqs, nh):
    t = u_a.shape[0]
    ch = MLSTM_CHUNK
    nchunks = t // ch
    da = nh * HEAD_DIM
    start_chunks = [s // ch for s, _ in seqs]
    last_chunks = [(s + l) // ch - 1 for s, l in seqs]
    fwd = lambda j: pl.BlockSpec((ch, da), lambda c, j=j: (c, j))
    bwd = lambda j: pl.BlockSpec((ch, da), lambda c, j=j: (nchunks - 1 - c, j))
    gspec_f = pl.BlockSpec((ch, 128), lambda c: (c, 0))
    gspec_b = pl.BlockSpec((ch, 128), lambda c: (nchunks - 1 - c, 0))
    bias = jnp.zeros((1, 128), F32).at[0, :4 * nh].set(gate_bias.reshape(-1).astype(F32))
    return pl.pallas_call(
        functools.partial(_mlstm_kernel, nh=nh, nchunks=nchunks, start_chunks=start_chunks,
                          last_chunks=last_chunks),
        grid=(nchunks,),
        in_specs=[fwd(0), fwd(1), fwd(2), gspec_f, bwd(0), bwd(1), bwd(2), gspec_b,
                  pl.BlockSpec((1, 128), lambda c: (0, 0))],
        out_specs=[pl.BlockSpec((ch, da), lambda c: (c, 0)),
                   pl.BlockSpec((ch, da), lambda c: (nchunks - 1 - c, 0))],
        out_shape=[jax.ShapeDtypeStruct((t, da), F32)] * 2,
        scratch_shapes=[pltpu.VMEM((2, nh, HEAD_DIM, 2 * HEAD_DIM), F32),
                        pltpu.VMEM((2, nh, 1, HEAD_DIM), F32)],
        compiler_params=_cparams("arbitrary"),
        name="mlstm",
    )(u_a, u_a, u_a, gates, u_a, u_a, u_a, gates, bias)


def _mlstm_gate_kernel(hf_ref, hb_ref, o_ref, g_ref, y_ref):
    hsum = hf_ref[...] + hb_ref[...]
    o = o_ref[...].astype(F32)
    g = g_ref[...]
    outs = []
    for h in range(hsum.shape[1] // HEAD_DIM):
        hs = slice(h * HEAD_DIM, (h + 1) * HEAD_DIM)
        a = hsum[:, hs]
        y = a * lax.rsqrt(jnp.mean(a * a, axis=-1, keepdims=True) + EPS) * g[:, hs]
        outs.append(jax.nn.sigmoid(o[:, hs]) * y)
    y_ref[...] = jnp.concatenate(outs, axis=1).astype(y_ref.dtype)


def _mlstm_gate(h_f, h_b, u_a, hnorm_g, tm=512):
    t, da = h_f.shape
    return pl.pallas_call(
        _mlstm_gate_kernel,
        grid=(t // tm,),
        in_specs=[pl.BlockSpec((tm, da), lambda i: (i, 0)), pl.BlockSpec((tm, da), lambda i: (i, 0)),
                  pl.BlockSpec((tm, da), lambda i: (i, 3)), pl.BlockSpec((1, da), lambda i: (0, 0))],
        out_specs=pl.BlockSpec((tm, da), lambda i: (i, 0)),
        out_shape=jax.ShapeDtypeStruct((t, da), BF16),
        compiler_params=_cparams("parallel"),
        name="mlstm_gate",
    )(h_f, h_b, u_a, hnorm_g.reshape(1, da).astype(F32))


NA_QROWS = 4
NA_KROWS = NA_QROWS + NA_ROWS
NA_UNROLL = 4


def _na_bias_table(rel_bias):
    h = rel_bias.shape[0]
    nr, nc = 2 * NA_ROWS - 1, 2 * NA_COLS - 1
    c = np.arange(GRID_W)
    dc = np.clip(c[None, :] - c[:, None], -(NA_COLS - 1), NA_COLS - 1) + NA_COLS - 1
    c_start = np.clip(c - NA_COLS // 2, 0, GRID_W - NA_COLS)
    col_ok = (c[None, :] >= c_start[:, None]) & (c[None, :] < c_start[:, None] + NA_COLS)
    var = np.arange(3)[:, None, None]
    qi = np.arange(NA_QROWS)[None, :, None]
    kr = np.arange(NA_KROWS)[None, None, :]
    dr = kr - var * NA_QROWS - qi
    first = np.where(var == 0, 0, np.where(var == 1, qi, NA_QROWS))
    row_ok = (kr >= first) & (kr < first + NA_ROWS)
    dr_idx = np.clip(dr + NA_ROWS - 1, 0, nr - 1).reshape(-1)
    onehot = (dc.reshape(1, -1) == np.arange(nc)[:, None]).astype(np.float32)
    cols = jnp.dot(rel_bias.astype(F32).reshape(h * nr, nc), onehot, precision=lax.Precision.HIGHEST)
    cols = cols.reshape(h, nr, GRID_W, GRID_W)
    tbl = jnp.stack([cols[:, int(i)] for i in dr_idx], axis=1)
    tbl = tbl.reshape(h, 3, NA_QROWS, NA_KROWS, GRID_W, GRID_W)
    ok = row_ok[:, :, :, None, None] & col_ok[None, None, None, :, :]
    tbl = jnp.where(ok[None], tbl, NEG)
    return jnp.transpose(tbl, (0, 1, 2, 4, 3, 5)).reshape(h, 3, NA_QROWS * GRID_W, NA_KROWS * GRID_W)


def _na_kernel(q_ref, kp_ref, km_ref, kn_ref, vp_ref, vm_ref, vn_ref, bias_ref, o_ref, kcat, vcat,
               *, seg_rows, row_lo, row_hi):
    s = pl.program_id(1)
    halo = NA_ROWS * GRID_W
    seg = seg_rows * GRID_W
    kcat[0:halo] = kp_ref[...]
    kcat[halo:halo + seg] = km_ref[...]
    kcat[halo + seg:halo + seg + halo] = kn_ref[...]
    vcat[0:halo, 0:HEAD_DIM] = vp_ref[...]
    vcat[halo:halo + seg, 0:HEAD_DIM] = vm_ref[...]
    vcat[halo + seg:halo + seg + halo, 0:HEAD_DIM] = vn_ref[...]
    vcat[:, HEAD_DIM:] = jnp.ones((seg + 2 * halo, HEAD_DIM), BF16)
    r_lo = _seq_lookup(s, row_lo, 0)
    r_hi = _seq_lookup(s, row_hi, 0)
    scale = HEAD_DIM ** -0.5
    nq = NA_QROWS * GRID_W
    nk = NA_KROWS * GRID_W

    def one_group(gl):
        r0 = s * seg_rows + gl * NA_QROWS
        u = jnp.clip(r0 - NA_ROWS // 2, r_lo, r_hi - NA_KROWS)
        off = pl.multiple_of((u - s * seg_rows + NA_ROWS) * GRID_W, GRID_W)
        qoff = pl.multiple_of(gl * nq, nq)
        q = q_ref[pl.ds(qoff, nq), :]
        sc = _dot_nt(q, kcat[pl.ds(off, nk), :]) * scale + bias_ref[(r0 - u) // NA_QROWS]
        e = jnp.exp(sc - jnp.max(sc, axis=1, keepdims=True))
        nd = _dot(e.astype(BF16), vcat[pl.ds(off, nk), :])
        o_ref[pl.ds(qoff, nq), :] = (nd[:, :HEAD_DIM] / nd[:, HEAD_DIM:]).astype(o_ref.dtype)

    def body(it, carry):
        for k in range(NA_UNROLL):
            one_group(it * NA_UNROLL + k)
        return carry

    lax.fori_loop(0, seg_rows // (NA_QROWS * NA_UNROLL), body, 0)


def _neighbourhood_attention(qk, v, bias_tbl, seqs, seg):
    t, dh = v.shape
    nh = dh // HEAD_DIM
    nseg = t // seg
    seg_rows = seg // GRID_W
    assert all(l // GRID_W >= NA_KROWS and (s // GRID_W) % NA_QROWS == 0 and (l // GRID_W) % NA_QROWS == 0
               for s, l in seqs)
    halo = NA_ROWS * GRID_W
    hpb = seg // halo
    nhb = t // halo
    row_lo = [(s // seg, s // GRID_W) for s, _ in seqs]
    row_hi = [(s // seg, (s + l) // GRID_W) for s, l in seqs]
    main = lambda c0: pl.BlockSpec((seg, HEAD_DIM), lambda h, s: (s, c0 + h))
    prev = lambda c0: pl.BlockSpec((halo, HEAD_DIM), lambda h, s: (jnp.maximum(s * hpb - 1, 0), c0 + h))
    nxt = lambda c0: pl.BlockSpec((halo, HEAD_DIM), lambda h, s: (jnp.minimum((s + 1) * hpb, nhb - 1), c0 + h))
    return pl.pallas_call(
        functools.partial(_na_kernel, seg_rows=seg_rows, row_lo=row_lo, row_hi=row_hi),
        grid=(nh, nseg),
        in_specs=[main(0), prev(nh), main(nh), nxt(nh), prev(0), main(0), nxt(0),
                  pl.BlockSpec((None, 3, NA_QROWS * GRID_W, NA_KROWS * GRID_W), lambda h, s: (h, 0, 0, 0))],
        out_specs=main(0),
        out_shape=jax.ShapeDtypeStruct((t, dh), BF16),
        scratch_shapes=[pltpu.VMEM((seg + 2 * halo, HEAD_DIM), BF16), pltpu.VMEM((seg + 2 * halo, 2 * HEAD_DIM), BF16)],
        compiler_params=_cparams("parallel", "parallel"),
        name="natten",
    )(qk, qk, qk, qk, v, v, v, bias_tbl)


DIL_BLK = 128
DIL_RADIUS = 64
DIL_MAX = max(d for _, d in DIL_PAIRS)
DIL_QB = DIL_BLK * DIL_MAX
DIL_HALO = DIL_RADIUS * DIL_MAX


def _dilated_kernel(q_ref, kp_ref, kc_ref, kn_ref, vp_ref, vc_ref, vn_ref, y_ref, num_s, mx_s, den_s,
                    *, first_blocks, last_blocks):
    i = pl.program_id(1)
    has_prev = jnp.logical_not(functools.reduce(jnp.logical_or, [i == b for b in first_blocks]))
    has_next = jnp.logical_not(functools.reduce(jnp.logical_or, [i == b for b in last_blocks]))
    nk = DIL_BLK + 2 * DIL_RADIUS
    j_io = lax.broadcasted_iota(jnp.int32, (DIL_BLK, nk), 0)
    s_io = lax.broadcasted_iota(jnp.int32, (DIL_BLK, nk), 1)
    band = jnp.abs(s_io - DIL_RADIUS - j_io) <= DIL_RADIUS
    ok_prev = jnp.logical_or(s_io >= DIL_RADIUS, has_prev)
    ok_next = jnp.logical_or(s_io < DIL_BLK + DIL_RADIUS, has_next)
    madd = {}
    for up in (False, True):
        for un in (False, True):
            ok = band
            ok = jnp.logical_and(ok, ok_prev) if up else ok
            ok = jnp.logical_and(ok, ok_next) if un else ok
            madd[up, un] = jnp.where(ok, 0.0, NEG)
    scale = HEAD_DIM ** -0.5
    ones_v = jnp.ones((nk, HEAD_DIM), BF16)

    def rows(start, size, d):
        return pl.ds(start, size, stride=d) if d > 1 else pl.ds(start, size)

    for b, (_, d) in enumerate(DIL_PAIRS):
        per_class = DIL_QB // d
        nj = per_class // DIL_BLK
        for g in range(d):
            for j in range(nj):
                up, un = j == 0, j == nj - 1
                qrows = rows(g + d * DIL_BLK * j, DIL_BLK, d)
                m0 = max(DIL_BLK * j - DIL_RADIUS, 0)
                m1 = min(DIL_BLK * (j + 1) + DIL_RADIUS, per_class)
                kparts, vparts = [], []
                if up:
                    r = rows(DIL_HALO - DIL_RADIUS * d + g, DIL_RADIUS, d)
                    kparts.append(kp_ref[r, :])
                    vparts.append(vp_ref[r, :])
                r = rows(g + d * m0, m1 - m0, d)
                kparts.append(kc_ref[r, :])
                vparts.append(vc_ref[r, :])
                if un:
                    r = rows(g, DIL_RADIUS, d)
                    kparts.append(kn_ref[r, :])
                    vparts.append(vn_ref[r, :])
                kk = jnp.concatenate(kparts, axis=0).astype(BF16)
                vv = jnp.concatenate([jnp.concatenate(vparts, axis=0).astype(BF16), ones_v], axis=1)
                sc = _dot_nt(q_ref[qrows, :].astype(BF16), kk) * scale + madd[up, un]
                mx = jnp.max(sc, axis=1, keepdims=True)
                nd = _dot(jnp.exp(sc - mx).astype(BF16), vv)
                num_s[b, qrows, :] = nd[:, :HEAD_DIM]
                mx_s[b, qrows, :] = jnp.broadcast_to(mx, (DIL_BLK, HEAD_DIM))
                den_s[b, qrows, :] = nd[:, HEAD_DIM:]

    nb = len(DIL_PAIRS)
    mxs = [mx_s[b] for b in range(nb)]
    m_top = functools.reduce(jnp.maximum, mxs)
    w = [jnp.exp(m - m_top) for m in mxs]
    num = functools.reduce(jnp.add, [w[b] * num_s[b] for b in range(nb)])
    den = functools.reduce(jnp.add, [w[b] * den_s[b] for b in range(nb)])
    y_ref[...] = (num / den).astype(y_ref.dtype)


def _dilated_attention(qk, v, seqs):
    t, dh = v.shape
    nh = dh // HEAD_DIM
    assert all(window // (2 * dil) == DIL_RADIUS for window, dil in DIL_PAIRS)
    assert all(s % DIL_QB == 0 and l % DIL_QB == 0 for s, l in seqs)
    nq = t // DIL_QB
    hb = DIL_QB // DIL_HALO
    nhalo = t // DIL_HALO
    first_blocks = [s // DIL_QB for s, _ in seqs]
    last_blocks = [(s + l) // DIL_QB - 1 for s, l in seqs]
    cur = lambda c0: pl.BlockSpec((DIL_QB, HEAD_DIM), lambda h, i: (i, c0 + h))
    prev = lambda c0: pl.BlockSpec((DIL_HALO, HEAD_DIM), lambda h, i: (jnp.maximum(i * hb - 1, 0), c0 + h))
    nxt = lambda c0: pl.BlockSpec((DIL_HALO, HEAD_DIM), lambda h, i: (jnp.minimum((i + 1) * hb, nhalo - 1), c0 + h))
    nb = len(DIL_PAIRS)
    return pl.pallas_call(
        functools.partial(_dilated_kernel, first_blocks=first_blocks, last_blocks=last_blocks),
        grid=(nh, nq),
        in_specs=[cur(0), prev(nh), cur(nh), nxt(nh), prev(0), cur(0), nxt(0)],
        out_specs=cur(0),
        out_shape=jax.ShapeDtypeStruct((t, dh), BF16),
        scratch_shapes=[pltpu.VMEM((nb, DIL_QB, HEAD_DIM), F32)] * 3,
        compiler_params=_cparams("parallel", "parallel"),
        name="dilated",
    )(qk, qk, qk, qk, v, v, v)


def _rope_tables(max_len):
    half = ROPE_DIM // 2
    inv_freq = ROPE_THETA ** (-jnp.arange(half, dtype=F32) / half)
    ang = jnp.arange(max_len, dtype=F32)[:, None] * inv_freq[None, :]
    cos, sin = jnp.cos(ang), jnp.sin(ang)
    rest = HEAD_DIM - ROPE_DIM
    c = jnp.concatenate([cos, cos, jnp.ones((max_len, rest), F32)], axis=1)
    s = jnp.concatenate([-sin, sin, jnp.zeros((max_len, rest), F32)], axis=1)
    return c, s


def _router_kernel(x_ref, g_ref, wr_ref, rb_ref, xn_ref, e_ref, gt_ref, *, n_exp):
    x = x_ref[...]
    xn = x * lax.rsqrt(jnp.mean(x * x, axis=-1, keepdims=True) + EPS) * g_ref[...]
    xn_ref[...] = xn.astype(xn_ref.dtype)
    tm = x.shape[0]
    epg = n_exp // N_GROUPS
    xh, xm, _ = _split3(xn)
    wh, wm, _ = _split3(wr_ref[...])
    logits = _dot_nt(wh, xh) + _dot_nt(wh, xm) + _dot_nt(wm, xh)
    scores = jax.nn.sigmoid(logits)
    sel = (scores + rb_ref[:, 0:1]).reshape(N_GROUPS, epg, tm)
    sc3 = scores.reshape(N_GROUPS, epg, tm)
    io = lax.broadcasted_iota(jnp.int32, (N_GROUPS, epg, tm), 1)
    gio = lax.broadcasted_iota(jnp.int32, (N_GROUPS, 1, tm), 0)
    m1 = jnp.max(sel, axis=1, keepdims=True)
    i1 = jnp.min(jnp.where(sel == m1, io, epg), axis=1, keepdims=True)
    sel_b = jnp.where(io == i1, -jnp.inf, sel)
    m2 = jnp.max(sel_b, axis=1, keepdims=True)
    i2 = jnp.min(jnp.where(sel_b == m2, io, epg), axis=1, keepdims=True)
    gs = m1 + m2
    gbest = jnp.min(jnp.where(gs == jnp.max(gs, axis=0, keepdims=True), gio, N_GROUPS), axis=0, keepdims=True)
    in_g = gio == gbest
    outs_e, outs_w = [], []
    for ik in (i1, i2):
        pick = in_g & (io == ik)
        outs_e.append(jnp.sum(jnp.where(in_g, gbest * epg + ik, 0), axis=0))
        outs_w.append(jnp.sum(jnp.sum(jnp.where(pick, sc3, 0.0), axis=1, keepdims=True), axis=0))
    wsum = outs_w[0] + outs_w[1]
    e_ref[0:1, :] = outs_e[0]
    e_ref[1:2, :] = outs_e[1]
    gt_ref[0:1, :] = outs_w[0] / wsum
    gt_ref[1:2, :] = outs_w[1] / wsum


def _router(x, g, w_router, router_bias, tm=512):
    t, d = x.shape
    n_exp = w_router.shape[1]
    rb = jnp.broadcast_to(router_bias.astype(F32)[:, None], (n_exp, 128))
    return pl.pallas_call(
        functools.partial(_router_kernel, n_exp=n_exp),
        grid=(t // tm,),
        in_specs=[pl.BlockSpec((tm, d), lambda i: (i, 0)), pl.BlockSpec((1, d), lambda i: (0, 0)),
                  pl.BlockSpec((n_exp, d), lambda i: (0, 0)), pl.BlockSpec((n_exp, 128), lambda i: (0, 0))],
        out_specs=[pl.BlockSpec((tm, d), lambda i: (i, 0)), pl.BlockSpec((2, tm), lambda i: (0, i)),
                   pl.BlockSpec((2, tm), lambda i: (0, i))],
        out_shape=[jax.ShapeDtypeStruct((t, d), F32), jax.ShapeDtypeStruct((2, t), jnp.int32),
                   jax.ShapeDtypeStruct((2, t), F32)],
        compiler_params=_cparams("parallel"),
        name="router",
    )(x, g.reshape(1, d).astype(F32), w_router.T.astype(F32), rb)


def _rank_kernel(e_ref, rank_ref, cnt_ref, run_ref, *, n_exp):
    i = pl.program_id(0)
    tm = e_ref.shape[1]

    @pl.when(i == 0)
    def _():
        run_ref[...] = jnp.zeros_like(run_ref)

    eio = lax.broadcasted_iota(jnp.int32, (n_exp, tm), 0)
    oh = [(eio == e_ref[k:k + 1, :]) for k in range(TOP_K)]
    ohf = [jnp.where(o, 1.0, 0.0) for o in oh]
    both = ohf[0] + ohf[1]
    r_i = lax.broadcasted_iota(jnp.int32, (tm, tm), 0)
    c_i = lax.broadcasted_iota(jnp.int32, (tm, tm), 1)
    upper = jnp.where(r_i <= c_i, 1.0, 0.0).astype(BF16)
    cum = _dot(both.astype(BF16), upper)
    base = run_ref[:, 0:1] + cum - both
    for k in range(TOP_K):
        rank_ref[k:k + 1, :] = jnp.sum(jnp.where(oh[k], base, 0.0), axis=0, keepdims=True).astype(jnp.int32)
    run_new = run_ref[...] + cum[:, tm - 1:tm]
    run_ref[...] = run_new
    cnt_ref[...] = run_new.astype(jnp.int32)


def _rank(e, n_exp, tm=512):
    t = e.shape[1]
    return pl.pallas_call(
        functools.partial(_rank_kernel, n_exp=n_exp),
        grid=(t // tm,),
        in_specs=[pl.BlockSpec((2, tm), lambda i: (0, i))],
        out_specs=[pl.BlockSpec((2, tm), lambda i: (0, i)), pl.BlockSpec((n_exp, 128), lambda i: (0, 0))],
        out_shape=[jax.ShapeDtypeStruct((2, t), jnp.int32), jax.ShapeDtypeStruct((n_exp, 128), jnp.int32)],
        scratch_shapes=[pltpu.VMEM((n_exp, 128), F32)],
        compiler_params=_cparams("arbitrary"),
        name="moe_rank",
    )(e)


def _slots_kernel(ps_ref, e_ref, rank_ref, slot_ref, *, n_exp):
    e = e_ref[...]
    start = jnp.zeros_like(e)
    for k in range(n_exp):
        start = jnp.where(e == k, ps_ref[k], start)
    slot_ref[...] = start + rank_ref[...]


def _slots(e, rank, pad_start, tm=2048):
    t = e.shape[1]
    tm = min(tm, t)
    spec = pl.BlockSpec((TOP_K, tm), lambda i, ps: (0, i))
    return pl.pallas_call(
        functools.partial(_slots_kernel, n_exp=pad_start.shape[0]),
        grid_spec=pltpu.PrefetchScalarGridSpec(num_scalar_prefetch=1, grid=(t // tm,), in_specs=[spec, spec],
                                               out_specs=spec),
        out_shape=jax.ShapeDtypeStruct((TOP_K, t), jnp.int32),
        compiler_params=_cparams("parallel"),
        name="moe_slots",
    )(pad_start, e, rank)


MOE_BLK = 512


CAST_ROWS = 256


ROW_UNROLL = 8


def _start_rows(n_rows, make_copy):
    def body(r, c):
        for k in range(TOP_K):
            make_copy(k, r).start(priority=k)
        return c
    lax.fori_loop(0, n_rows, body, 0, unroll=ROW_UNROLL)


def _wait_rows(n_rows, make_copy):
    def body(r, c):
        for k in range(TOP_K):
            make_copy(k, 0).wait()
        return c
    lax.fori_loop(0, n_rows, body, 0, unroll=ROW_UNROLL)


def _dispatch_kernel(fill_ref, *rest, tm, n_fill):
    slot_refs, (xn_ref, xs_out, zbuf, sem, zsem) = rest[:TOP_K], rest[TOP_K:]
    i = pl.program_id(0)

    @pl.when(i == 0)
    def _():
        zbuf[...] = jnp.zeros_like(zbuf)

        def fill(j):
            rows = pl.ds(pl.multiple_of(fill_ref[j] * MOE_BLK, MOE_BLK), MOE_BLK)
            return pltpu.make_async_copy(zbuf, xs_out.at[rows, :], zsem)

        def start(j, c):
            pl.when(fill_ref[n_fill + j] > 0)(lambda: fill(j).start())
            return c

        def wait(j, c):
            pl.when(fill_ref[n_fill + j] > 0)(lambda: fill(j).wait())
            return c

        lax.fori_loop(0, n_fill, start, 0)
        lax.fori_loop(0, n_fill, wait, 0)

    def copy(k, r):
        return pltpu.make_async_copy(xn_ref.at[pl.ds(r, 1), :], xs_out.at[pl.ds(slot_refs[k][r], 1), :], sem)

    _start_rows(tm, copy)
    _wait_rows(tm, copy)


def _dispatch(slots, xn, fill, cap, tm=256):
    t, d = xn.shape
    smem = pl.BlockSpec((tm,), lambda i, f: (i,), memory_space=pltpu.SMEM)
    return pl.pallas_call(
        functools.partial(_dispatch_kernel, tm=tm, n_fill=fill.shape[0] // 2),
        grid_spec=pltpu.PrefetchScalarGridSpec(
            num_scalar_prefetch=1, grid=(t // tm,),
            in_specs=[smem] * TOP_K + [pl.BlockSpec((tm, d), lambda i, f: (i, 0))],
            out_specs=pl.BlockSpec(memory_space=pl.ANY),
            scratch_shapes=[pltpu.VMEM((MOE_BLK, d), F32), pltpu.SemaphoreType.DMA, pltpu.SemaphoreType.DMA]),
        out_shape=jax.ShapeDtypeStruct((cap, d), F32),
        compiler_params=_cparams("arbitrary"),
        name="moe_dispatch",
    )(fill, *slots, xn)


def _dot_casting(x, w_refs, s_refs):
    k = x.shape[1]
    step = min(CAST_ROWS, k)
    accs = [None] * len(w_refs)
    for r0 in range(0, k, step):
        rows = slice(r0, r0 + step)
        for n, (w_ref, s_ref) in enumerate(zip(w_refs, s_refs)):
            wc = w_ref[rows, :].astype(BF16)
            s_ref[rows, :] = wc
            part = _dot(x[:, rows], wc)
            accs[n] = part if accs[n] is None else accs[n] + part
    return accs


def _expert_changed(b, be_ref):
    return jnp.logical_or(b == 0, be_ref[b] != be_ref[jnp.maximum(b - 1, 0)])


def _expert_mm_kernel(be_ref, nb_ref, nx_ref, x_ref, *rest, layer, n_w, finish):
    w_hbm, o_ref, (wf, ws, sem) = rest[:n_w], rest[n_w], rest[n_w + 1:]
    b = pl.program_id(0)
    live = b < nb_ref[0]
    first = _expert_changed(b, be_ref)

    def fetch(e):
        return [pltpu.make_async_copy(w.at[layer, e], wf.at[n], sem.at[n]) for n, w in enumerate(w_hbm)]

    @pl.when(b == 0)
    def _():
        for c in fetch(be_ref[0]):
            c.start()

    @pl.when(jnp.logical_and(live, first))
    def _():
        for c in fetch(be_ref[b]):
            c.wait()
        outs = _dot_casting(x_ref[...].astype(BF16), [wf.at[n] for n in range(n_w)], [ws.at[n] for n in range(n_w)])
        o_ref[...] = finish(outs).astype(o_ref.dtype)

        @pl.when(nx_ref[b] >= 0)
        def _():
            for c in fetch(nx_ref[b]):
                c.start()

    @pl.when(jnp.logical_and(live, jnp.logical_not(first)))
    def _():
        x = x_ref[...].astype(BF16)
        o_ref[...] = finish([_dot(x, ws[n]) for n in range(n_w)]).astype(o_ref.dtype)

    @pl.when(jnp.logical_not(live))
    def _():
        o_ref[...] = jnp.zeros_like(o_ref)


def _swiglu(outs):
    hg, hu = outs
    return (hg * jax.nn.sigmoid(hg)) * hu


def _expert_mm(xs, sched, layer, ws_hbm, finish, out_dtype, name):
    blk_expert, n_used, next_expert = sched
    cap, k = xs.shape
    n = ws_hbm[0].shape[3]
    n_w = len(ws_hbm)
    return pl.pallas_call(
        functools.partial(_expert_mm_kernel, layer=layer, n_w=n_w, finish=finish),
        grid_spec=pltpu.PrefetchScalarGridSpec(
            num_scalar_prefetch=3, grid=(cap // MOE_BLK,),
            in_specs=[pl.BlockSpec((MOE_BLK, k), lambda b, be, nb, nx: (jnp.minimum(b, nb[0] - 1), 0))]
            + [pl.BlockSpec(memory_space=pl.ANY)] * n_w,
            out_specs=pl.BlockSpec((MOE_BLK, n), lambda b, be, nb, nx: (b, 0)),
            scratch_shapes=[pltpu.VMEM((n_w, k, n), F32), pltpu.VMEM((n_w, k, n), BF16),
                            pltpu.SemaphoreType.DMA((n_w,))]),
        out_shape=jax.ShapeDtypeStruct((cap, n), out_dtype),
        compiler_params=_cparams("arbitrary"),
        name=name,
    )(blk_expert, n_used, next_expert, xs, *ws_hbm)


def _expert_ffn(xs, sched, layer, w_gate, w_up, w_down):
    h = _expert_mm(xs, sched, layer, (w_gate, w_up), _swiglu, BF16, "moe_ffn_up")
    return _expert_mm(h, sched, layer, (w_down,), lambda outs: outs[0], F32, "moe_ffn_down")


def _combine_kernel(*refs, tm, nsteps, out_starts, out_nblk):
    slot_cur, slot_nxt = refs[:TOP_K], refs[TOP_K:2 * TOP_K]
    x_ref, gt_ref, ys_hbm = refs[2 * TOP_K:2 * TOP_K + 3]
    o_refs, (buf, sem) = refs[2 * TOP_K + 3:-2], refs[-2:]
    i = pl.program_id(0)
    cur = lax.rem(i, 2)

    def gather(slot_refs, b):
        def copy(k, r):
            return pltpu.make_async_copy(ys_hbm.at[pl.ds(slot_refs[k][r], 1), :], buf.at[b, k, pl.ds(r, 1), :],
                                         sem.at[b])
        return copy

    pl.when(i == 0)(lambda: _start_rows(tm, gather(slot_cur, 0)))
    pl.when(i + 1 < nsteps)(lambda: _start_rows(tm, gather(slot_nxt, 1 - cur)))
    _wait_rows(tm, gather(slot_cur, cur))
    out = x_ref[...] + (gt_ref[:, 0:1] * buf[cur, 0] + gt_ref[:, 1:2] * buf[cur, 1])
    for o_ref, st, nb in zip(o_refs, out_starts, out_nblk):
        @pl.when(jnp.logical_and(i >= st, i < st + nb))
        def _(o_ref=o_ref):
            o_ref[...] = out


def _combine(slots, x, gt_cols, ys, out_rows, tm=256):
    t, d = x.shape
    nsteps = t // tm
    smem = lambda nxt: pl.BlockSpec((tm,), lambda i: (jnp.minimum(i + nxt, nsteps - 1),), memory_space=pltpu.SMEM)
    out_nblk = [r // tm for r in out_rows]
    out_starts = [sum(out_nblk[:n]) for n in range(len(out_nblk))]
    out_specs = [pl.BlockSpec((tm, d), lambda i, st=st, nb=nb: (jnp.clip(i - st, 0, nb - 1), 0))
                 for st, nb in zip(out_starts, out_nblk)]
    return pl.pallas_call(
        functools.partial(_combine_kernel, tm=tm, nsteps=nsteps, out_starts=out_starts, out_nblk=out_nblk),
        grid=(nsteps,),
        in_specs=[smem(0)] * TOP_K + [smem(1)] * TOP_K
        + [pl.BlockSpec((tm, d), lambda i: (i, 0)), pl.BlockSpec((tm, TOP_K), lambda i: (i, 0)),
           pl.BlockSpec(memory_space=pl.ANY)],
        out_specs=out_specs,
        scratch_shapes=[pltpu.VMEM((2, TOP_K, tm, d), F32), pltpu.SemaphoreType.DMA((2,))],
        out_shape=[jax.ShapeDtypeStruct((r, d), F32) for r in out_rows],
        compiler_params=_cparams("arbitrary"),
        name="moe_combine",
    )(*slots, *slots, x, gt_cols, ys)


def _moe(x, ln_g, w_router, router_bias, layer, w_gate, w_up, w_down, out_rows):
    t, d = x.shape
    n_exp = w_router.shape[1]
    xn, e, gt = _router(x, ln_g, w_router, router_bias)
    rank, cnt = _rank(e, n_exp)
    counts = cnt[:, 0]
    padded = (counts + MOE_BLK - 1) // MOE_BLK * MOE_BLK
    pad_end = jnp.cumsum(padded)
    pad_start = (pad_end - padded).astype(jnp.int32)
    cap = t * TOP_K + n_exp * MOE_BLK
    nblk = cap // MOE_BLK
    blk_first = jnp.arange(nblk, dtype=jnp.int32) * MOE_BLK
    blk_expert = jnp.minimum(jnp.sum(pad_end[None, :] <= blk_first[:, None], axis=1), n_exp - 1).astype(jnp.int32)
    n_used = (pad_end[-1] // MOE_BLK).astype(jnp.int32).reshape(1)
    eid = jnp.arange(n_exp, dtype=jnp.int32)
    later = jnp.logical_and(padded[None, :] > 0, eid[None, :] > eid[:, None])
    next_of = jnp.min(jnp.where(later, eid[None, :], n_exp), axis=1)
    next_of = jnp.where(next_of < n_exp, next_of, -1)
    next_expert = jnp.sum(jnp.where(blk_expert[:, None] == eid[None, :], next_of[None, :], 0), axis=1).astype(jnp.int32)
    sched = (blk_expert, n_used, next_expert)
    trail = n_used[0] + jnp.arange(n_exp, dtype=jnp.int32)
    fill_blk = jnp.concatenate([pad_end // MOE_BLK - 1, jnp.minimum(trail, nblk - 1)])
    fill_ok = jnp.concatenate([padded > 0, trail < nblk])
    fill = jnp.concatenate([jnp.maximum(fill_blk, 0), fill_ok.astype(jnp.int32)]).astype(jnp.int32)
    slot = _slots(e, rank, pad_start)
    slots = tuple(slot[k] for k in range(TOP_K))
    xs = _dispatch(slots, xn, fill, cap)
    ys = _expert_ffn(xs, sched, layer, w_gate, w_up, w_down)
    return tuple(_combine(slots, x, gt.T, ys, out_rows))


def _ab_layer(xp, seqs, seg, ln_g, w_in, gate_bias, hnorm_g, qnorm_g, knorm_g, rel_bias, w_out):
    nh_a = gate_bias.shape[1]
    nh_b = rel_bias.shape[0]
    d_a, d_b = nh_a * HEAD_DIM, nh_b * HEAD_DIM
    xn = _rmsnorm(xp, ln_g)
    w = w_in.astype(BF16)
    g0 = 4 * d_a
    g1 = g0 + 4 * nh_a
    u_a = _matmul([xn], [w[:, :g0]], BF16)
    w_g = jnp.zeros((w.shape[0], 128), BF16).at[:, :4 * nh_a].set(w[:, g0:g1])
    gates = _matmul([xn], [w_g], F32)
    gain = jnp.concatenate([jnp.tile(qnorm_g, nh_b), jnp.tile(knorm_g, nh_b)]).reshape(1, -1).astype(F32)
    qk_b = _matmul([xn], [w[:, g1:g1 + 2 * d_b]], BF16, epi="headnorm", extra=(gain,), tn=1024)
    v_b = _matmul([xn], [w[:, g1 + 2 * d_b:]], BF16)
    h_f, h_b = _mlstm(u_a, gates, gate_bias, seqs, nh_a)
    y_a = _mlstm_gate(h_f, h_b, u_a, hnorm_g)
    y_b = _neighbourhood_attention(qk_b, v_b, _na_bias_table(rel_bias), seqs, seg)
    wo = w_out.astype(BF16)
    return _matmul([y_a, y_b], [wo[:d_a], wo[d_a:]], F32, epi="residual", extra=xp)


def _dilated_layer(xp, seqs, ln_g, w_in, qnorm_g, knorm_g, w_out):
    d_c = w_in.shape[1] // 3
    nh = d_c // HEAD_DIM
    xn = _rmsnorm(xp, ln_g)
    w = w_in.astype(BF16)
    tm = 1024
    max_len = max(l for _, l in seqs)
    tables = _rope_tables(max_len)
    pos_tbl = [(s // tm, s // tm) for s, _ in seqs]
    pos_blk = lambda i: i - _seq_lookup(i, pos_tbl, 0)
    gain = jnp.concatenate([jnp.tile(qnorm_g, nh), jnp.tile(knorm_g, nh)]).reshape(1, -1).astype(F32)
    qk = _matmul([xn], [w[:, :2 * d_c]], F32, epi="headnorm_rope", extra=(gain,) + tables,
                 pos_blk=pos_blk, tm=tm, tn=1024)
    v = _matmul([xn], [w[:, 2 * d_c:]], F32, tm=tm)
    y = _dilated_attention(qk, v, seqs)
    return _matmul([y], [w_out.astype(BF16)], F32, epi="residual", extra=xp)


def kernel(x_prompt, x_sample, ab_ln, ab_w_in, ab_gate_bias, ab_hnorm, ab_qnorm, ab_knorm, ab_relbias, ab_w_out,
           c_ln, c_w_in, c_qnorm, c_knorm, c_w_out, ffn_ln, w_router, router_bias, w_gate, w_up, w_down):
    bp, lp, d = x_prompt.shape
    bs, ls, _ = x_sample.shape
    tp = bp * lp
    seqs = [(b * lp, lp) for b in range(bp)] + [(tp + b * ls, ls) for b in range(bs)]
    seg = math.gcd(lp, ls)
    xp = (x_prompt.reshape(tp, d), x_sample.reshape(bs * ls, d))
    depth = ffn_ln.shape[0]
    for layer in range(depth):
        j = layer // 2
        if layer % 2 == 0:
            x = _ab_layer(xp, seqs, seg, ab_ln[j], ab_w_in[j], ab_gate_bias[j], ab_hnorm[j].reshape(-1),
                          ab_qnorm[j], ab_knorm[j], ab_relbias[j], ab_w_out[j])
        else:
            x = _dilated_layer(xp, seqs, c_ln[j], c_w_in[j], c_qnorm[j], c_knorm[j], c_w_out[j])
        xp = _moe(x, ffn_ln[layer], w_router, router_bias, layer, w_gate, w_up, w_down,
                  out_rows=(tp, bs * ls) if layer == depth - 1 else (tp + bs * ls,))
    return xp[0].reshape(bp, lp, d), xp[1].reshape(bs, ls, d)
```

```python
import functools
import math

import jax
import jax.numpy as jnp
import numpy as np
from jax import lax
from jax.experimental import pallas as pl
from jax.experimental.pallas import tpu as pltpu

HEAD_DIM = 128
MLSTM_CHUNK = 128
GRID_W = 64
NA_ROWS = 8
NA_COLS = 16
DIL_PAIRS = ((128, 1), (512, 4), (2048, 16))
ROPE_THETA = 500000.0
ROPE_DIM = HEAD_DIM // 4
N_GROUPS = 4
TOP_K = 2
EPS = 1e-6
NEG = -1e30
VMEM_LIMIT_BYTES = 56 * 1024 * 1024

F32 = jnp.float32
BF16 = jnp.bfloat16


def _cparams(*sem):
    return pltpu.CompilerParams(dimension_semantics=sem, vmem_limit_bytes=VMEM_LIMIT_BYTES)


def _dot(a, b):
    return jnp.dot(a, b, preferred_element_type=F32)


def _dot_nt(a, b):
    return lax.dot_general(a, b, (((1,), (1,)), ((), ())), preferred_element_type=F32)


def _dot_tn(a, b):
    return lax.dot_general(a, b, (((0,), (0,)), ((), ())), preferred_element_type=F32)


def _split3(x):
    hi = x.astype(BF16)
    r1 = x - hi.astype(F32)
    mid = r1.astype(BF16)
    lo = (r1 - mid.astype(F32)).astype(BF16)
    return hi, mid, lo


def _seq_lookup(u, table, default):
    out = default
    for thr, val in table:
        out = jnp.where(u >= thr, val, out)
    return out


def _rmsnorm_kernel(*refs, starts):
    g_ref, o_ref = refs[-2:]
    x = _pick_part(refs[:-2], starts, pl.program_id(0))
    ms = jnp.mean(x * x, axis=-1, keepdims=True)
    o_ref[...] = (x * lax.rsqrt(ms + EPS) * g_ref[...]).astype(o_ref.dtype)


def _rmsnorm(parts, g, tm=512):
    d = parts[0].shape[1]
    t = sum(p.shape[0] for p in parts)
    specs, starts = _part_specs(parts, tm, d, lambda: 0)
    return pl.pallas_call(
        functools.partial(_rmsnorm_kernel, starts=starts),
        grid=(t // tm,),
        in_specs=specs + [pl.BlockSpec((1, d), lambda i: (0, 0))],
        out_specs=pl.BlockSpec((tm, d), lambda i: (i, 0)),
        out_shape=jax.ShapeDtypeStruct((t, d), BF16),
        compiler_params=_cparams("parallel"),
        name="rmsnorm",
    )(*parts, g.reshape(1, d).astype(F32))


EPI_COLS = 256


def _head_mats(n):
    k = lax.broadcasted_iota(jnp.int32, (n, n), 0)
    i = lax.broadcasted_iota(jnp.int32, (n, n), 1)
    shift = HEAD_DIM.bit_length() - 1
    same = jnp.right_shift(k, shift) == jnp.right_shift(i, shift)
    kl, il = jnp.bitwise_and(k, HEAD_DIM - 1), jnp.bitwise_and(i, HEAD_DIM - 1)
    half = ROPE_DIM // 2
    partner = jnp.where(il < half, il + half, jnp.where(il < 2 * half, il - half, -1))
    ones = jnp.where(same, 1.0, 0.0).astype(BF16)
    swap = jnp.where(jnp.logical_and(same, kl == partner), 1.0, 0.0).astype(BF16)
    return ones, swap


def _head_epilogue(acc, gain, rope, mats):
    ones, swap = mats
    y = acc * lax.rsqrt(_dot((acc * acc).astype(BF16), ones) * (1.0 / HEAD_DIM) + EPS) * gain
    if rope is not None:
        reps = acc.shape[1] // HEAD_DIM
        c, s = (jnp.concatenate([t] * reps, axis=1) for t in rope)
        y = y * c + _dot(y.astype(BF16), swap) * s
    return y


def _pick_part(refs, starts, i):
    val = refs[0][...]
    for ref, st in zip(refs[1:], starts[1:]):
        val = jnp.where(i >= st, ref[...], val)
    return val


def _mm_kernel(*refs, n_in, epi, res_starts):
    xs, ws, rest = refs[:n_in], refs[n_in:2 * n_in], refs[2 * n_in:]
    o_ref = rest[-1]
    if epi in ("headnorm", "headnorm_rope"):
        rope = (rest[1][...], rest[2][...]) if epi == "headnorm_rope" else None
        mats = _head_mats(EPI_COLS)
        x = xs[0][...]
        for c0 in range(0, o_ref.shape[1], 2 * EPI_COLS):
            acc = _dot(x, ws[0][:, c0:c0 + 2 * EPI_COLS])
            for h0 in range(0, acc.shape[1], EPI_COLS):
                cs = slice(c0 + h0, c0 + h0 + EPI_COLS)
                o_ref[:, cs] = _head_epilogue(acc[:, h0:h0 + EPI_COLS], rest[0][:, cs], rope, mats).astype(o_ref.dtype)
        return
    acc = _dot(xs[0][...], ws[0][...])
    for x_ref, w_ref in zip(xs[1:], ws[1:]):
        acc = acc + _dot(x_ref[...], w_ref[...])
    if epi == "plain":
        o_ref[...] = acc.astype(o_ref.dtype)
    elif epi == "residual":
        i = pl.program_id(0)
        ends = res_starts[1:] + [None]
        for ref, st, en in zip(rest[:-1], res_starts, ends):
            @pl.when(i >= st if en is None else jnp.logical_and(i >= st, i < en))
            def _(ref=ref):
                o_ref[...] = (acc + ref[...]).astype(o_ref.dtype)
    else:
        raise ValueError(epi)


def _part_specs(parts, tm, block_cols, col_of):
    specs, starts, st = [], [], 0
    for p in parts:
        nb = p.shape[0] // tm
        specs.append(pl.BlockSpec((tm, block_cols), lambda i, *j, st=st, nb=nb: (jnp.clip(i - st, 0, nb - 1), col_of(*j))))
        starts.append(st)
        st += nb
    return specs, starts


def _matmul(xs, ws, out_dtype, epi="plain", extra=(), pos_blk=None, tm=1024, tn=1024):
    t = xs[0].shape[0]
    n = ws[0].shape[1]
    tm, tn = min(tm, t), min(tn, n)
    in_specs = [pl.BlockSpec((tm, x.shape[1]), lambda i, j: (i, 0)) for x in xs]
    in_specs += [pl.BlockSpec((w.shape[0], tn), lambda i, j: (0, j)) for w in ws]
    res_starts = None
    if epi == "residual":
        specs, res_starts = _part_specs(extra, tm, tn, lambda j: j)
        in_specs += specs
    elif epi in ("headnorm", "headnorm_rope"):
        in_specs.append(pl.BlockSpec((1, tn), lambda i, j: (0, j)))
        if epi == "headnorm_rope":
            in_specs += [pl.BlockSpec((tm, HEAD_DIM), lambda i, j: (pos_blk(i), 0))] * 2
    return pl.pallas_call(
        functools.partial(_mm_kernel, n_in=len(xs), epi=epi, res_starts=res_starts),
        grid=(t // tm, n // tn),
        in_specs=in_specs,
        out_specs=pl.BlockSpec((tm, tn), lambda i, j: (i, j)),
        out_shape=jax.ShapeDtypeStruct((t, n), out_dtype),
        compiler_params=_cparams("parallel", "parallel"),
        name="matmul_" + epi,
    )(*xs, *ws, *extra)


def _mlstm_kernel(qf_ref, kf_ref, vf_ref, gf_ref, qb_ref, kb_ref, vb_ref, gb_ref, bias_ref,
                  hf_ref, hb_ref, s_ref, m_ref, *, nh, nchunks, start_chunks, last_chunks):
    c = pl.program_id(0)
    cb = nchunks - 1 - c
    ch = MLSTM_CHUNK
    is_start = functools.reduce(jnp.logical_or, [c == s for s in start_chunks])
    is_last = functools.reduce(jnp.logical_or, [cb == e for e in last_chunks])

    def _reset(d):
        s_ref[d] = jnp.zeros((nh, HEAD_DIM, 2 * HEAD_DIM), F32)
        m_ref[d] = jnp.zeros((nh, 1, HEAD_DIM), F32)

    pl.when(is_start)(lambda: _reset(0))
    pl.when(is_last)(lambda: _reset(1))

    row = lax.broadcasted_iota(jnp.int32, (ch, ch), 0)
    col = lax.broadcasted_iota(jnp.int32, (ch, ch), 1)
    scale = HEAD_DIM ** -0.5
    heads = lambda ref: jnp.stack([ref[:, h * HEAD_DIM:(h + 1) * HEAD_DIM] for h in range(nh)])
    bdot = lambda a, b, ca, cb: lax.dot_general(a, b, (((ca,), (cb,)), ((0,), (0,))), preferred_element_type=F32)

    for d, (q_ref, k_ref, v_ref, g_ref, h_ref) in enumerate(
            ((qf_ref, kf_ref, vf_ref, gf_ref, hf_ref), (qb_ref, kb_ref, vb_ref, gb_ref, hb_ref))):
        mask = (row >= col) if d == 0 else (col >= row)
        icol0 = 2 * nh * d
        fcol0 = icol0 + nh
        tot_row = ch - 1 if d == 0 else 0
        g = g_ref[...] + bias_ref[...]
        lf = jnp.minimum(g, 0.0) - jnp.log1p(jnp.exp(-jnp.abs(g)))
        tri = jnp.where(mask, 1.0, 0.0).astype(BF16)
        hi, mid, lo = _split3(lf)
        bcum = _dot(tri, hi) + _dot(tri, mid) + _dot(tri, lo)
        a = g - pltpu.roll(bcum, HEAD_DIM - nh, 1)
        a_t = a.T
        bcol = jnp.stack([bcum[:, fcol0 + h:fcol0 + h + 1] for h in range(nh)])
        acol = jnp.stack([a[:, icol0 + h:icol0 + h + 1] for h in range(nh)])
        arow = jnp.stack([a_t[icol0 + h:icol0 + h + 1, :] for h in range(nh)])
        b_tot = bcol[:, tot_row:tot_row + 1, :]
        m_st = m_ref[d][:, :, 0:1]
        s_prev = s_ref[d]

        log_d = jnp.where(mask[None], bcol + arow, NEG)
        m_intra = jnp.max(log_d, axis=2, keepdims=True)
        log_inter = bcol + m_st
        m_row = jnp.maximum(log_inter, m_intra)
        dmat = jnp.exp(log_d - m_row)
        q = heads(q_ref)
        ks = (heads(k_ref).astype(F32) * scale).astype(BF16)
        v1 = jnp.concatenate([heads(v_ref), jnp.ones((nh, ch, HEAD_DIM), BF16)], axis=2)
        p = bdot(q, ks, 2, 2) * dmat
        w_inter = jnp.exp(log_inter - m_row)
        nd = bdot(p.astype(BF16), v1, 2, 1) + w_inter * bdot(q, s_prev.astype(BF16), 2, 1)
        h_out = nd[:, :, :HEAD_DIM] / jnp.maximum(jnp.abs(nd[:, :, HEAD_DIM:]), jnp.exp(-m_row))
        for h in range(nh):
            h_ref[:, h * HEAD_DIM:(h + 1) * HEAD_DIM] = h_out[h]

        m_kv = b_tot + jnp.max(arow, axis=2, keepdims=True)
        m_new = jnp.maximum(b_tot + m_st, m_kv)
        decay = jnp.exp(b_tot + m_st - m_new)
        wexp = jnp.exp(b_tot + acol - m_new)
        kw = ks.astype(F32) * wexp
        s_ref[d] = decay * s_prev + bdot(kw.astype(BF16), v1, 1, 1)
        m_ref[d] = jnp.broadcast_to(m_new, (nh, 1, HEAD_DIM))


def _mlstm(u_a, gates, gate_bias, seqs, nh):
    t = u_a.shape[0]
    ch = MLSTM_CHUNK
    nchunks = t // ch
    da = nh * HEAD_DIM
    start_chunks = [s // ch for s, _ in seqs]
    last_chunks = [(s + l) // ch - 1 for s, l in seqs]
    fwd = lambda j: pl.BlockSpec((ch, da), lambda c, j=j: (c, j))
    bwd = lambda j: pl.BlockSpec((ch, da), lambda c, j=j: (nchunks - 1 - c, j))
    gspec_f = pl.BlockSpec((ch, 128), lambda c: (c, 0))
    gspec_b = pl.BlockSpec((ch, 128), lambda c: (nchunks - 1 - c, 0))
    bias = jnp.zeros((1, 128), F32).at[0, :4 * nh].set(gate_bias.reshape(-1).astype(F32))
    return pl.pallas_call(
        functools.partial(_mlstm_kernel, nh=nh, nchunks=nchunks, start_chunks=start_chunks,
                          last_chunks=last_chunks),
        grid=(nchunks,),
        in_specs=[fwd(0), fwd(1), fwd(2), gspec_f, bwd(0), bwd(1), bwd(2), gspec_b,
                  pl.BlockSpec((1, 128), lambda c: (0, 0))],
        out_specs=[pl.BlockSpec((ch, da), lambda c: (c, 0)),
                   pl.BlockSpec((ch, da), lambda c: (nchunks - 1 - c, 0))],
        out_shape=[jax.ShapeDtypeStruct((t, da), F32)] * 2,
        scratch_shapes=[pltpu.VMEM((2, nh, HEAD_DIM, 2 * HEAD_DIM), F32),
                        pltpu.VMEM((2, nh, 1, HEAD_DIM), F32)],
        compiler_params=_cparams("arbitrary"),
        name="mlstm",
    )(u_a, u_a, u_a, gates, u_a, u_a, u_a, gates, bias)


def _mlstm_gate_kernel(hf_ref, hb_ref, o_ref, g_ref, y_ref):
    hsum = hf_ref[...] + hb_ref[...]
    o = o_ref[...].astype(F32)
    g = g_ref[...]
    outs = []
    for h in range(hsum.shape[1] // HEAD_DIM):
        hs = slice(h * HEAD_DIM, (h + 1) * HEAD_DIM)
        a = hsum[:, hs]
        y = a * lax.rsqrt(jnp.mean(a * a, axis=-1, keepdims=True) + EPS) * g[:, hs]
        outs.append(jax.nn.sigmoid(o[:, hs]) * y)
    y_ref[...] = jnp.concatenate(outs, axis=1).astype(y_ref.dtype)


def _mlstm_gate(h_f, h_b, u_a, hnorm_g, tm=512):
    t, da = h_f.shape
    return pl.pallas_call(
        _mlstm_gate_kernel,
        grid=(t // tm,),
        in_specs=[pl.BlockSpec((tm, da), lambda i: (i, 0)), pl.BlockSpec((tm, da), lambda i: (i, 0)),
                  pl.BlockSpec((tm, da), lambda i: (i, 3)), pl.BlockSpec((1, da), lambda i: (0, 0))],
        out_specs=pl.BlockSpec((tm, da), lambda i: (i, 0)),
        out_shape=jax.ShapeDtypeStruct((t, da), BF16),
        compiler_params=_cparams("parallel"),
        name="mlstm_gate",
    )(h_f, h_b, u_a, hnorm_g.reshape(1, da).astype(F32))


NA_QROWS = 4
NA_KROWS = NA_QROWS + NA_ROWS
NA_UNROLL = 4


def _na_bias_table(rel_bias):
    h = rel_bias.shape[0]
    nr, nc = 2 * NA_ROWS - 1, 2 * NA_COLS - 1
    c = np.arange(GRID_W)
    dc = np.clip(c[None, :] - c[:, None], -(NA_COLS - 1), NA_COLS - 1) + NA_COLS - 1
    c_start = np.clip(c - NA_COLS // 2, 0, GRID_W - NA_COLS)
    col_ok = (c[None, :] >= c_start[:, None]) & (c[None, :] < c_start[:, None] + NA_COLS)
    var = np.arange(3)[:, None, None]
    qi = np.arange(NA_QROWS)[None, :, None]
    kr = np.arange(NA_KROWS)[None, None, :]
    dr = kr - var * NA_QROWS - qi
    first = np.where(var == 0, 0, np.where(var == 1, qi, NA_QROWS))
    row_ok = (kr >= first) & (kr < first + NA_ROWS)
    dr_idx = np.clip(dr + NA_ROWS - 1, 0, nr - 1).reshape(-1)
    onehot = (dc.reshape(1, -1) == np.arange(nc)[:, None]).astype(np.float32)
    cols = jnp.dot(rel_bias.astype(F32).reshape(h * nr, nc), onehot, precision=lax.Precision.HIGHEST)
    cols = cols.reshape(h, nr, GRID_W, GRID_W)
    tbl = jnp.stack([cols[:, int(i)] for i in dr_idx], axis=1)
    tbl = tbl.reshape(h, 3, NA_QROWS, NA_KROWS, GRID_W, GRID_W)
    ok = row_ok[:, :, :, None, None] & col_ok[None, None, None, :, :]
    tbl = jnp.where(ok[None], tbl, NEG)
    return jnp.transpose(tbl, (0, 1, 2, 4, 3, 5)).reshape(h, 3, NA_QROWS * GRID_W, NA_KROWS * GRID_W)


def _na_kernel(q_ref, kp_ref, km_ref, kn_ref, vp_ref, vm_ref, vn_ref, bias_ref, o_ref, kcat, vcat,
               *, seg_rows, row_lo, row_hi):
    s = pl.program_id(1)
    halo = NA_ROWS * GRID_W
    seg = seg_rows * GRID_W
    kcat[0:halo] = kp_ref[...]
    kcat[halo:halo + seg] = km_ref[...]
    kcat[halo + seg:halo + seg + halo] = kn_ref[...]
    vcat[0:halo, 0:HEAD_DIM] = vp_ref[...]
    vcat[halo:halo + seg, 0:HEAD_DIM] = vm_ref[...]
    vcat[halo + seg:halo + seg + halo, 0:HEAD_DIM] = vn_ref[...]
    vcat[:, HEAD_DIM:] = jnp.ones((seg + 2 * halo, HEAD_DIM), BF16)
    r_lo = _seq_lookup(s, row_lo, 0)
    r_hi = _seq_lookup(s, row_hi, 0)
    scale = HEAD_DIM ** -0.5
    nq = NA_QROWS * GRID_W
    nk = NA_KROWS * GRID_W

    def one_group(gl):
        r0 = s * seg_rows + gl * NA_QROWS
        u = jnp.clip(r0 - NA_ROWS // 2, r_lo, r_hi - NA_KROWS)
        off = pl.multiple_of((u - s * seg_rows + NA_ROWS) * GRID_W, GRID_W)
        qoff = pl.multiple_of(gl * nq, nq)
        q = q_ref[pl.ds(qoff, nq), :]
        sc = _dot_nt(q, kcat[pl.ds(off, nk), :]) * scale + bias_ref[(r0 - u) // NA_QROWS]
        e = jnp.exp(sc - jnp.max(sc, axis=1, keepdims=True))
        nd = _dot(e.astype(BF16), vcat[pl.ds(off, nk), :])
        o_ref[pl.ds(qoff, nq), :] = (nd[:, :HEAD_DIM] / nd[:, HEAD_DIM:]).astype(o_ref.dtype)

    def body(it, carry):
        for k in range(NA_UNROLL):
            one_group(it * NA_UNROLL + k)
        return carry

    lax.fori_loop(0, seg_rows // (NA_QROWS * NA_UNROLL), body, 0)


def _neighbourhood_attention(qk, v, bias_tbl, seqs, seg):
    t, dh = v.shape
    nh = dh // HEAD_DIM
    nseg = t // seg
    seg_rows = seg // GRID_W
    assert all(l // GRID_W >= NA_KROWS and (s // GRID_W) % NA_QROWS == 0 and (l // GRID_W) % NA_QROWS == 0
               for s, l in seqs)
    halo = NA_ROWS * GRID_W
    hpb = seg // halo
    nhb = t // halo
    row_lo = [(s // seg, s // GRID_W) for s, _ in seqs]
    row_hi = [(s // seg, (s + l) // GRID_W) for s, l in seqs]
    main = lambda c0: pl.BlockSpec((seg, HEAD_DIM), lambda h, s: (s, c0 + h))
    prev = lambda c0: pl.BlockSpec((halo, HEAD_DIM), lambda h, s: (jnp.maximum(s * hpb - 1, 0), c0 + h))
    nxt = lambda c0: pl.BlockSpec((halo, HEAD_DIM), lambda h, s: (jnp.minimum((s + 1) * hpb, nhb - 1), c0 + h))
    return pl.pallas_call(
        functools.partial(_na_kernel, seg_rows=seg_rows, row_lo=row_lo, row_hi=row_hi),
        grid=(nh, nseg),
        in_specs=[main(0), prev(nh), main(nh), nxt(nh), prev(0), main(0), nxt(0),
                  pl.BlockSpec((None, 3, NA_QROWS * GRID_W, NA_KROWS * GRID_W), lambda h, s: (h, 0, 0, 0))],
        out_specs=main(0),
        out_shape=jax.ShapeDtypeStruct((t, dh), BF16),
        scratch_shapes=[pltpu.VMEM((seg + 2 * halo, HEAD_DIM), BF16), pltpu.VMEM((seg + 2 * halo, 2 * HEAD_DIM), BF16)],
        compiler_params=_cparams("parallel", "parallel"),
        name="natten",
    )(qk, qk, qk, qk, v, v, v, bias_tbl)


DIL_BLK = 128
DIL_RADIUS = 64
DIL_MAX = max(d for _, d in DIL_PAIRS)
DIL_QB = DIL_BLK * DIL_MAX
DIL_HALO = DIL_RADIUS * DIL_MAX
DIL_NAT_MAX = 1
DIL_NAT_HALO = DIL_RADIUS * max(d for _, d in DIL_PAIRS if d <= DIL_NAT_MAX)


def _dilated_natural(b, d, q_ref, kcat, vcat, has_prev, has_next, num_s, mx_s, den_s):
    halo = DIL_RADIUS * d
    nk = DIL_BLK + 2 * halo
    i_io = lax.broadcasted_iota(jnp.int32, (DIL_BLK, nk), 0)
    j_io = lax.broadcasted_iota(jnp.int32, (DIL_BLK, nk), 1)
    rel = j_io - halo - i_io
    ok = jnp.logical_and(jnp.abs(rel) <= halo, jnp.bitwise_and(rel, d - 1) == 0)
    base = jnp.where(ok, 0.0, NEG)
    scale = HEAD_DIM ** -0.5
    for t in range(DIL_QB // DIL_BLK):
        k0 = DIL_NAT_HALO + DIL_BLK * t - halo
        n_prev = max(DIL_NAT_HALO - k0, 0)
        n_next = max(k0 + nk - (DIL_NAT_HALO + DIL_QB), 0)
        madd = base
        if n_prev:
            madd = jnp.where(jnp.logical_or(j_io >= n_prev, has_prev), madd, NEG)
        if n_next:
            madd = jnp.where(jnp.logical_or(j_io < nk - n_next, has_next), madd, NEG)
        qrows = pl.ds(DIL_BLK * t, DIL_BLK)
        sc = _dot_nt(q_ref[qrows, :].astype(BF16), kcat[k0:k0 + nk, :]) * scale + madd
        mx = jnp.max(sc, axis=1, keepdims=True)
        nd = _dot(jnp.exp(sc - mx).astype(BF16), vcat[k0:k0 + nk, :])
        num_s[b, qrows, :] = nd[:, :HEAD_DIM]
        mx_s[b, qrows, :] = jnp.broadcast_to(mx, (DIL_BLK, HEAD_DIM))
        den_s[b, qrows, :] = nd[:, HEAD_DIM:]


def _dilated_classes(b, d, q_ref, k_refs, v_refs, has_prev, has_next, num_s, mx_s, den_s):
    per_class = DIL_QB // d
    halo = DIL_RADIUS * d
    to_classes = lambda x: pltpu.einshape("mgd->gmd", x.reshape(x.shape[0] // d, d, HEAD_DIM))
    to_natural = lambda x: pltpu.einshape("gmd->mgd", x).reshape(DIL_QB, HEAD_DIM)
    bdot = lambda a, c, ca, cc: lax.dot_general(a, c, (((ca,), (cc,)), ((0,), (0,))), preferred_element_type=F32)
    nk = DIL_BLK + 2 * DIL_RADIUS

    def with_halo(refs):
        prv, cur, nxt = refs
        parts = [to_classes(prv[DIL_HALO - halo:DIL_HALO, :]), to_classes(cur[...]), to_classes(nxt[0:halo, :])]
        return jnp.concatenate(parts, axis=1).astype(BF16)

    qd = to_classes(q_ref[...]).astype(BF16)
    kd = with_halo(k_refs)
    vd = with_halo(v_refs)
    vd = jnp.concatenate([vd, jnp.ones(vd.shape, BF16)], axis=2)
    i_io = lax.broadcasted_iota(jnp.int32, (DIL_BLK, nk), 0)
    j_io = lax.broadcasted_iota(jnp.int32, (DIL_BLK, nk), 1)
    band = jnp.abs(j_io - DIL_RADIUS - i_io) <= DIL_RADIUS
    nj = per_class // DIL_BLK
    nums, mxs, dens = [], [], []
    for j in range(nj):
        ok = band
        if j == 0:
            ok = jnp.logical_and(ok, jnp.logical_or(j_io >= DIL_RADIUS, has_prev))
        if j == nj - 1:
            ok = jnp.logical_and(ok, jnp.logical_or(j_io < DIL_BLK + DIL_RADIUS, has_next))
        qs = slice(DIL_BLK * j, DIL_BLK * (j + 1))
        ks = slice(DIL_BLK * j, DIL_BLK * j + nk)
        sc = bdot(qd[:, qs], kd[:, ks], 2, 2) * (HEAD_DIM ** -0.5) + jnp.where(ok, 0.0, NEG)[None]
        mx = jnp.max(sc, axis=2, keepdims=True)
        nd = bdot(jnp.exp(sc - mx).astype(BF16), vd[:, ks], 2, 1)
        nums.append(nd[:, :, :HEAD_DIM])
        mxs.append(jnp.broadcast_to(mx, (d, DIL_BLK, HEAD_DIM)))
        dens.append(nd[:, :, HEAD_DIM:])
    cat = lambda xs: xs[0] if len(xs) == 1 else jnp.concatenate(xs, axis=1)
    num_s[b] = to_natural(cat(nums))
    mx_s[b] = to_natural(cat(mxs))
    den_s[b] = to_natural(cat(dens))


def _dilated_kernel(q_ref, kp_ref, kc_ref, kn_ref, vp_ref, vc_ref, vn_ref, y_ref, num_s, mx_s, den_s, kcat, vcat,
                    *, first_blocks, last_blocks):
    i = pl.program_id(1)
    has_prev = jnp.logical_not(functools.reduce(jnp.logical_or, [i == b for b in first_blocks]))
    has_next = jnp.logical_not(functools.reduce(jnp.logical_or, [i == b for b in last_blocks]))
    h = DIL_NAT_HALO
    kcat[0:h] = kp_ref[DIL_HALO - h:DIL_HALO, :].astype(BF16)
    kcat[h:h + DIL_QB] = kc_ref[...].astype(BF16)
    kcat[h + DIL_QB:h + DIL_QB + h] = kn_ref[0:h, :].astype(BF16)
    vcat[0:h, 0:HEAD_DIM] = vp_ref[DIL_HALO - h:DIL_HALO, :].astype(BF16)
    vcat[h:h + DIL_QB, 0:HEAD_DIM] = vc_ref[...].astype(BF16)
    vcat[h + DIL_QB:h + DIL_QB + h, 0:HEAD_DIM] = vn_ref[0:h, :].astype(BF16)
    vcat[:, HEAD_DIM:] = jnp.ones((DIL_QB + 2 * h, HEAD_DIM), BF16)
    nk = DIL_BLK + 2 * DIL_RADIUS
    j_io = lax.broadcasted_iota(jnp.int32, (DIL_BLK, nk), 0)
    s_io = lax.broadcasted_iota(jnp.int32, (DIL_BLK, nk), 1)
    band = jnp.abs(s_io - DIL_RADIUS - j_io) <= DIL_RADIUS
    ok_prev = jnp.logical_or(s_io >= DIL_RADIUS, has_prev)
    ok_next = jnp.logical_or(s_io < DIL_BLK + DIL_RADIUS, has_next)
    madd = {}
    for up in (False, True):
        for un in (False, True):
            ok = band
            ok = jnp.logical_and(ok, ok_prev) if up else ok
            ok = jnp.logical_and(ok, ok_next) if un else ok
            madd[up, un] = jnp.where(ok, 0.0, NEG)
    scale = HEAD_DIM ** -0.5
    ones_v = jnp.ones((nk, HEAD_DIM), BF16)

    def rows(start, size, d):
        return pl.ds(start, size, stride=d) if d > 1 else pl.ds(start, size)

    for b, (_, d) in enumerate(DIL_PAIRS):
        if d <= DIL_NAT_MAX:
            _dilated_natural(b, d, q_ref, kcat, vcat, has_prev, has_next, num_s, mx_s, den_s)
            continue
        if d % 8 == 0:
            _dilated_classes(b, d, q_ref, (kp_ref, kc_ref, kn_ref), (vp_ref, vc_ref, vn_ref), has_prev, has_next,
                             num_s, mx_s, den_s)
            continue
        per_class = DIL_QB // d
        nj = per_class // DIL_BLK
        for g in range(d):
            for j in range(nj):
                up, un = j == 0, j == nj - 1
                qrows = rows(g + d * DIL_BLK * j, DIL_BLK, d)
                m0 = max(DIL_BLK * j - DIL_RADIUS, 0)
                m1 = min(DIL_BLK * (j + 1) + DIL_RADIUS, per_class)
                kparts, vparts = [], []
                if up:
                    r = rows(DIL_HALO - DIL_RADIUS * d + g, DIL_RADIUS, d)
                    kparts.append(kp_ref[r, :])
                    vparts.append(vp_ref[r, :])
                r = rows(g + d * m0, m1 - m0, d)
                kparts.append(kc_ref[r, :])
                vparts.append(vc_ref[r, :])
                if un:
                    r = rows(g, DIL_RADIUS, d)
                    kparts.append(kn_ref[r, :])
                    vparts.append(vn_ref[r, :])
                kk = jnp.concatenate(kparts, axis=0).astype(BF16)
                vv = jnp.concatenate([jnp.concatenate(vparts, axis=0).astype(BF16), ones_v], axis=1)
                sc = _dot_nt(q_ref[qrows, :].astype(BF16), kk) * scale + madd[up, un]
                mx = jnp.max(sc, axis=1, keepdims=True)
                nd = _dot(jnp.exp(sc - mx).astype(BF16), vv)
                num_s[b, qrows, :] = nd[:, :HEAD_DIM]
                mx_s[b, qrows, :] = jnp.broadcast_to(mx, (DIL_BLK, HEAD_DIM))
                den_s[b, qrows, :] = nd[:, HEAD_DIM:]

    nb = len(DIL_PAIRS)
    mxs = [mx_s[b] for b in range(nb)]
    m_top = functools.reduce(jnp.maximum, mxs)
    w = [jnp.exp(m - m_top) for m in mxs]
    num = functools.reduce(jnp.add, [w[b] * num_s[b] for b in range(nb)])
    den = functools.reduce(jnp.add, [w[b] * den_s[b] for b in range(nb)])
    y_ref[...] = (num / den).astype(y_ref.dtype)


def _dilated_attention(qk, v, seqs):
    t, dh = v.shape
    nh = dh // HEAD_DIM
    assert all(window // (2 * dil) == DIL_RADIUS for window, dil in DIL_PAIRS)
    assert all(s % DIL_QB == 0 and l % DIL_QB == 0 for s, l in seqs)
    nq = t // DIL_QB
    hb = DIL_QB // DIL_HALO
    nhalo = t // DIL_HALO
    first_blocks = [s // DIL_QB for s, _ in seqs]
    last_blocks = [(s + l) // DIL_QB - 1 for s, l in seqs]
    cur = lambda c0: pl.BlockSpec((DIL_QB, HEAD_DIM), lambda h, i: (i, c0 + h))
    prev = lambda c0: pl.BlockSpec((DIL_HALO, HEAD_DIM), lambda h, i: (jnp.maximum(i * hb - 1, 0), c0 + h))
    nxt = lambda c0: pl.BlockSpec((DIL_HALO, HEAD_DIM), lambda h, i: (jnp.minimum((i + 1) * hb, nhalo - 1), c0 + h))
    nb = len(DIL_PAIRS)
    return pl.pallas_call(
        functools.partial(_dilated_kernel, first_blocks=first_blocks, last_blocks=last_blocks),
        grid=(nh, nq),
        in_specs=[cur(0), prev(nh), cur(nh), nxt(nh), prev(0), cur(0), nxt(0)],
        out_specs=cur(0),
        out_shape=jax.ShapeDtypeStruct((t, dh), BF16),
        scratch_shapes=[pltpu.VMEM((nb, DIL_QB, HEAD_DIM), F32)] * 3
        + [pltpu.VMEM((DIL_QB + 2 * DIL_NAT_HALO, HEAD_DIM), BF16),
           pltpu.VMEM((DIL_QB + 2 * DIL_NAT_HALO, 2 * HEAD_DIM), BF16)],
        compiler_params=_cparams("parallel", "parallel"),
        name="dilated",
    )(qk, qk, qk, qk, v, v, v)


def _rope_tables(max_len):
    half = ROPE_DIM // 2
    inv_freq = ROPE_THETA ** (-jnp.arange(half, dtype=F32) / half)
    ang = jnp.arange(max_len, dtype=F32)[:, None] * inv_freq[None, :]
    cos, sin = jnp.cos(ang), jnp.sin(ang)
    rest = HEAD_DIM - ROPE_DIM
    c = jnp.concatenate([cos, cos, jnp.ones((max_len, rest), F32)], axis=1)
    s = jnp.concatenate([-sin, sin, jnp.zeros((max_len, rest), F32)], axis=1)
    return c, s


def _router_kernel(x_ref, g_ref, wr_ref, rb_ref, xn_ref, e_ref, gt_ref, *, n_exp):
    x = x_ref[...]
    xn = x * lax.rsqrt(jnp.mean(x * x, axis=-1, keepdims=True) + EPS) * g_ref[...]
    xn_ref[...] = xn.astype(xn_ref.dtype)
    tm = x.shape[0]
    epg = n_exp // N_GROUPS
    xh, xm, _ = _split3(xn)
    wh, wm, _ = _split3(wr_ref[...])
    logits = _dot_nt(wh, xh) + _dot_nt(wh, xm) + _dot_nt(wm, xh)
    scores = jax.nn.sigmoid(logits)
    sel = (scores + rb_ref[:, 0:1]).reshape(N_GROUPS, epg, tm)
    sc3 = scores.reshape(N_GROUPS, epg, tm)
    io = lax.broadcasted_iota(jnp.int32, (N_GROUPS, epg, tm), 1)
    gio = lax.broadcasted_iota(jnp.int32, (N_GROUPS, 1, tm), 0)
    m1 = jnp.max(sel, axis=1, keepdims=True)
    i1 = jnp.min(jnp.where(sel == m1, io, epg), axis=1, keepdims=True)
    sel_b = jnp.where(io == i1, -jnp.inf, sel)
    m2 = jnp.max(sel_b, axis=1, keepdims=True)
    i2 = jnp.min(jnp.where(sel_b == m2, io, epg), axis=1, keepdims=True)
    gs = m1 + m2
    gbest = jnp.min(jnp.where(gs == jnp.max(gs, axis=0, keepdims=True), gio, N_GROUPS), axis=0, keepdims=True)
    in_g = gio == gbest
    outs_e, outs_w = [], []
    for ik in (i1, i2):
        pick = in_g & (io == ik)
        outs_e.append(jnp.sum(jnp.where(in_g, gbest * epg + ik, 0), axis=0))
        outs_w.append(jnp.sum(jnp.sum(jnp.where(pick, sc3, 0.0), axis=1, keepdims=True), axis=0))
    wsum = outs_w[0] + outs_w[1]
    e_ref[0:1, :] = outs_e[0]
    e_ref[1:2, :] = outs_e[1]
    gt_ref[0:1, :] = outs_w[0] / wsum
    gt_ref[1:2, :] = outs_w[1] / wsum


def _router(x, g, w_router, router_bias, tm=512):
    t, d = x.shape
    n_exp = w_router.shape[1]
    rb = jnp.broadcast_to(router_bias.astype(F32)[:, None], (n_exp, 128))
    return pl.pallas_call(
        functools.partial(_router_kernel, n_exp=n_exp),
        grid=(t // tm,),
        in_specs=[pl.BlockSpec((tm, d), lambda i: (i, 0)), pl.BlockSpec((1, d), lambda i: (0, 0)),
                  pl.BlockSpec((n_exp, d), lambda i: (0, 0)), pl.BlockSpec((n_exp, 128), lambda i: (0, 0))],
        out_specs=[pl.BlockSpec((tm, d), lambda i: (i, 0)), pl.BlockSpec((2, tm), lambda i: (0, i)),
                   pl.BlockSpec((2, tm), lambda i: (0, i))],
        out_shape=[jax.ShapeDtypeStruct((t, d), F32), jax.ShapeDtypeStruct((2, t), jnp.int32),
                   jax.ShapeDtypeStruct((2, t), F32)],
        compiler_params=_cparams("parallel"),
        name="router",
    )(x, g.reshape(1, d).astype(F32), w_router.T.astype(F32), rb)


def _rank_kernel(e_ref, rank_ref, cnt_ref, run_ref, *, n_exp):
    i = pl.program_id(0)
    tm = e_ref.shape[1]

    @pl.when(i == 0)
    def _():
        run_ref[...] = jnp.zeros_like(run_ref)

    eio = lax.broadcasted_iota(jnp.int32, (n_exp, tm), 0)
    oh = [(eio == e_ref[k:k + 1, :]) for k in range(TOP_K)]
    ohf = [jnp.where(o, 1.0, 0.0) for o in oh]
    both = ohf[0] + ohf[1]
    r_i = lax.broadcasted_iota(jnp.int32, (tm, tm), 0)
    c_i = lax.broadcasted_iota(jnp.int32, (tm, tm), 1)
    upper = jnp.where(r_i <= c_i, 1.0, 0.0).astype(BF16)
    cum = _dot(both.astype(BF16), upper)
    base = run_ref[:, 0:1] + cum - both
    for k in range(TOP_K):
        rank_ref[k:k + 1, :] = jnp.sum(jnp.where(oh[k], base, 0.0), axis=0, keepdims=True).astype(jnp.int32)
    run_new = run_ref[...] + cum[:, tm - 1:tm]
    run_ref[...] = run_new
    cnt_ref[...] = run_new.astype(jnp.int32)


def _rank(e, n_exp, tm=512):
    t = e.shape[1]
    return pl.pallas_call(
        functools.partial(_rank_kernel, n_exp=n_exp),
        grid=(t // tm,),
        in_specs=[pl.BlockSpec((2, tm), lambda i: (0, i))],
        out_specs=[pl.BlockSpec((2, tm), lambda i: (0, i)), pl.BlockSpec((n_exp, 128), lambda i: (0, 0))],
        out_shape=[jax.ShapeDtypeStruct((2, t), jnp.int32), jax.ShapeDtypeStruct((n_exp, 128), jnp.int32)],
        scratch_shapes=[pltpu.VMEM((n_exp, 128), F32)],
        compiler_params=_cparams("arbitrary"),
        name="moe_rank",
    )(e)


def _slots_kernel(ps_ref, e_ref, rank_ref, slot_ref, *, n_exp):
    e = e_ref[...]
    start = jnp.zeros_like(e)
    for k in range(n_exp):
        start = jnp.where(e == k, ps_ref[k], start)
    slot_ref[...] = start + rank_ref[...]


def _slots(e, rank, pad_start, tm=2048):
    t = e.shape[1]
    tm = min(tm, t)
    spec = pl.BlockSpec((TOP_K, tm), lambda i, ps: (0, i))
    return pl.pallas_call(
        functools.partial(_slots_kernel, n_exp=pad_start.shape[0]),
        grid_spec=pltpu.PrefetchScalarGridSpec(num_scalar_prefetch=1, grid=(t // tm,), in_specs=[spec, spec],
                                               out_specs=spec),
        out_shape=jax.ShapeDtypeStruct((TOP_K, t), jnp.int32),
        compiler_params=_cparams("parallel"),
        name="moe_slots",
    )(pad_start, e, rank)


MOE_BLK = 512


CAST_ROWS = 256


ROW_UNROLL = 8


def _start_rows(n_rows, make_copy):
    def body(r, c):
        for k in range(TOP_K):
            make_copy(k, r).start(priority=k)
        return c
    lax.fori_loop(0, n_rows, body, 0, unroll=ROW_UNROLL)


def _wait_rows(n_rows, make_copy):
    def body(r, c):
        for k in range(TOP_K):
            make_copy(k, 0).wait()
        return c
    lax.fori_loop(0, n_rows, body, 0, unroll=ROW_UNROLL)


def _dispatch_kernel(fill_ref, *rest, tm, n_fill):
    slot_refs, (xn_ref, xs_out, zbuf, sem, zsem) = rest[:TOP_K], rest[TOP_K:]
    i = pl.program_id(0)

    @pl.when(i == 0)
    def _():
        zbuf[...] = jnp.zeros_like(zbuf)

        def fill(j):
            rows = pl.ds(pl.multiple_of(fill_ref[j] * MOE_BLK, MOE_BLK), MOE_BLK)
            return pltpu.make_async_copy(zbuf, xs_out.at[rows, :], zsem)

        def start(j, c):
            pl.when(fill_ref[n_fill + j] > 0)(lambda: fill(j).start())
            return c

        def wait(j, c):
            pl.when(fill_ref[n_fill + j] > 0)(lambda: fill(j).wait())
            return c

        lax.fori_loop(0, n_fill, start, 0)
        lax.fori_loop(0, n_fill, wait, 0)

    def copy(k, r):
        return pltpu.make_async_copy(xn_ref.at[pl.ds(r, 1), :], xs_out.at[pl.ds(slot_refs[k][r], 1), :], sem)

    _start_rows(tm, copy)
    _wait_rows(tm, copy)


def _dispatch(slots, xn, fill, cap, tm=256):
    t, d = xn.shape
    smem = pl.BlockSpec((tm,), lambda i, f: (i,), memory_space=pltpu.SMEM)
    return pl.pallas_call(
        functools.partial(_dispatch_kernel, tm=tm, n_fill=fill.shape[0] // 2),
        grid_spec=pltpu.PrefetchScalarGridSpec(
            num_scalar_prefetch=1, grid=(t // tm,),
            in_specs=[smem] * TOP_K + [pl.BlockSpec((tm, d), lambda i, f: (i, 0))],
            out_specs=pl.BlockSpec(memory_space=pl.ANY),
            scratch_shapes=[pltpu.VMEM((MOE_BLK, d), F32), pltpu.SemaphoreType.DMA, pltpu.SemaphoreType.DMA]),
        out_shape=jax.ShapeDtypeStruct((cap, d), F32),
        compiler_params=_cparams("arbitrary"),
        name="moe_dispatch",
    )(fill, *slots, xn)


def _dot_casting(x, w_refs, s_refs):
    k = x.shape[1]
    step = min(CAST_ROWS, k)
    accs = [None] * len(w_refs)
    for r0 in range(0, k, step):
        rows = slice(r0, r0 + step)
        for n, (w_ref, s_ref) in enumerate(zip(w_refs, s_refs)):
            wc = w_ref[rows, :].astype(BF16)
            s_ref[rows, :] = wc
            part = _dot(x[:, rows], wc)
            accs[n] = part if accs[n] is None else accs[n] + part
    return accs


def _expert_changed(b, be_ref):
    return jnp.logical_or(b == 0, be_ref[b] != be_ref[jnp.maximum(b - 1, 0)])


def _expert_mm_kernel(be_ref, nb_ref, nx_ref, x_ref, *rest, layer, n_w, finish):
    w_hbm, o_ref, (wf, ws, sem) = rest[:n_w], rest[n_w], rest[n_w + 1:]
    b = pl.program_id(0)
    live = b < nb_ref[0]
    first = _expert_changed(b, be_ref)

    def fetch(e):
        return [pltpu.make_async_copy(w.at[layer, e], wf.at[n], sem.at[n]) for n, w in enumerate(w_hbm)]

    @pl.when(b == 0)
    def _():
        for c in fetch(be_ref[0]):
            c.start()

    @pl.when(jnp.logical_and(live, first))
    def _():
        for c in fetch(be_ref[b]):
            c.wait()
        outs = _dot_casting(x_ref[...].astype(BF16), [wf.at[n] for n in range(n_w)], [ws.at[n] for n in range(n_w)])
        o_ref[...] = finish(outs).astype(o_ref.dtype)

        @pl.when(nx_ref[b] >= 0)
        def _():
            for c in fetch(nx_ref[b]):
                c.start()

    @pl.when(jnp.logical_and(live, jnp.logical_not(first)))
    def _():
        x = x_ref[...].astype(BF16)
        o_ref[...] = finish([_dot(x, ws[n]) for n in range(n_w)]).astype(o_ref.dtype)

    @pl.when(jnp.logical_not(live))
    def _():
        o_ref[...] = jnp.zeros_like(o_ref)


def _swiglu(outs):
    hg, hu = outs
    return (hg * jax.nn.sigmoid(hg)) * hu


def _expert_mm(xs, sched, layer, ws_hbm, finish, out_dtype, name):
    blk_expert, n_used, next_expert = sched
    cap, k = xs.shape
    n = ws_hbm[0].shape[3]
    n_w = len(ws_hbm)
    return pl.pallas_call(
        functools.partial(_expert_mm_kernel, layer=layer, n_w=n_w, finish=finish),
        grid_spec=pltpu.PrefetchScalarGridSpec(
            num_scalar_prefetch=3, grid=(cap // MOE_BLK,),
            in_specs=[pl.BlockSpec((MOE_BLK, k), lambda b, be, nb, nx: (jnp.minimum(b, nb[0] - 1), 0))]
            + [pl.BlockSpec(memory_space=pl.ANY)] * n_w,
            out_specs=pl.BlockSpec((MOE_BLK, n), lambda b, be, nb, nx: (b, 0)),
            scratch_shapes=[pltpu.VMEM((n_w, k, n), F32), pltpu.VMEM((n_w, k, n), BF16),
                            pltpu.SemaphoreType.DMA((n_w,))]),
        out_shape=jax.ShapeDtypeStruct((cap, n), out_dtype),
        compiler_params=_cparams("arbitrary"),
        name=name,
    )(blk_expert, n_used, next_expert, xs, *ws_hbm)


def _expert_ffn(xs, sched, layer, w_gate, w_up, w_down):
    h = _expert_mm(xs, sched, layer, (w_gate, w_up), _swiglu, BF16, "moe_ffn_up")
    return _expert_mm(h, sched, layer, (w_down,), lambda outs: outs[0], F32, "moe_ffn_down")


def _combine_kernel(*refs, tm, nsteps, out_starts, out_nblk):
    slot_cur, slot_nxt = refs[:TOP_K], refs[TOP_K:2 * TOP_K]
    x_ref, gt_ref, ys_hbm = refs[2 * TOP_K:2 * TOP_K + 3]
    o_refs, (buf, sem) = refs[2 * TOP_K + 3:-2], refs[-2:]
    i = pl.program_id(0)
    cur = lax.rem(i, 2)

    def gather(slot_refs, b):
        def copy(k, r):
            return pltpu.make_async_copy(ys_hbm.at[pl.ds(slot_refs[k][r], 1), :], buf.at[b, k, pl.ds(r, 1), :],
                                         sem.at[b])
        return copy

    pl.when(i == 0)(lambda: _start_rows(tm, gather(slot_cur, 0)))
    pl.when(i + 1 < nsteps)(lambda: _start_rows(tm, gather(slot_nxt, 1 - cur)))
    _wait_rows(tm, gather(slot_cur, cur))
    out = x_ref[...] + (gt_ref[:, 0:1] * buf[cur, 0] + gt_ref[:, 1:2] * buf[cur, 1])
    for o_ref, st, nb in zip(o_refs, out_starts, out_nblk):
        @pl.when(jnp.logical_and(i >= st, i < st + nb))
        def _(o_ref=o_ref):
            o_ref[...] = out


def _combine(slots, x, gt_cols, ys, out_rows, tm=256):
    t, d = x.shape
    nsteps = t // tm
    smem = lambda nxt: pl.BlockSpec((tm,), lambda i: (jnp.minimum(i + nxt, nsteps - 1),), memory_space=pltpu.SMEM)
    out_nblk = [r // tm for r in out_rows]
    out_starts = [sum(out_nblk[:n]) for n in range(len(out_nblk))]
    out_specs = [pl.BlockSpec((tm, d), lambda i, st=st, nb=nb: (jnp.clip(i - st, 0, nb - 1), 0))
                 for st, nb in zip(out_starts, out_nblk)]
    return pl.pallas_call(
        functools.partial(_combine_kernel, tm=tm, nsteps=nsteps, out_starts=out_starts, out_nblk=out_nblk),
        grid=(nsteps,),
        in_specs=[smem(0)] * TOP_K + [smem(1)] * TOP_K
        + [pl.BlockSpec((tm, d), lambda i: (i, 0)), pl.BlockSpec((tm, TOP_K), lambda i: (i, 0)),
           pl.BlockSpec(memory_space=pl.ANY)],
        out_specs=out_specs,
        scratch_shapes=[pltpu.VMEM((2, TOP_K, tm, d), F32), pltpu.SemaphoreType.DMA((2,))],
        out_shape=[jax.ShapeDtypeStruct((r, d), F32) for r in out_rows],
        compiler_params=_cparams("arbitrary"),
        name="moe_combine",
    )(*slots, *slots, x, gt_cols, ys)


def _moe(x, ln_g, w_router, router_bias, layer, w_gate, w_up, w_down, out_rows):
    t, d = x.shape
    n_exp = w_router.shape[1]
    xn, e, gt = _router(x, ln_g, w_router, router_bias)
    rank, cnt = _rank(e, n_exp)
    counts = cnt[:, 0]
    padded = (counts + MOE_BLK - 1) // MOE_BLK * MOE_BLK
    pad_end = jnp.cumsum(padded)
    pad_start = (pad_end - padded).astype(jnp.int32)
    cap = t * TOP_K + n_exp * MOE_BLK
    nblk = cap // MOE_BLK
    blk_first = jnp.arange(nblk, dtype=jnp.int32) * MOE_BLK
    blk_expert = jnp.minimum(jnp.sum(pad_end[None, :] <= blk_first[:, None], axis=1), n_exp - 1).astype(jnp.int32)
    n_used = (pad_end[-1] // MOE_BLK).astype(jnp.int32).reshape(1)
    eid = jnp.arange(n_exp, dtype=jnp.int32)
    later = jnp.logical_and(padded[None, :] > 0, eid[None, :] > eid[:, None])
    next_of = jnp.min(jnp.where(later, eid[None, :], n_exp), axis=1)
    next_of = jnp.where(next_of < n_exp, next_of, -1)
    next_expert = jnp.sum(jnp.where(blk_expert[:, None] == eid[None, :], next_of[None, :], 0), axis=1).astype(jnp.int32)
    sched = (blk_expert, n_used, next_expert)
    trail = n_used[0] + jnp.arange(n_exp, dtype=jnp.int32)
    fill_blk = jnp.concatenate([pad_end // MOE_BLK - 1, jnp.minimum(trail, nblk - 1)])
    fill_ok = jnp.concatenate([padded > 0, trail < nblk])
    fill = jnp.concatenate([jnp.maximum(fill_blk, 0), fill_ok.astype(jnp.int32)]).astype(jnp.int32)
    slot = _slots(e, rank, pad_start)
    slots = tuple(slot[k] for k in range(TOP_K))
    xs = _dispatch(slots, xn, fill, cap)
    ys = _expert_ffn(xs, sched, layer, w_gate, w_up, w_down)
    return tuple(_combine(slots, x, gt.T, ys, out_rows))


def _ab_layer(xp, seqs, seg, ln_g, w_in, gate_bias, hnorm_g, qnorm_g, knorm_g, rel_bias, w_out):
    nh_a = gate_bias.shape[1]
    nh_b = rel_bias.shape[0]
    d_a, d_b = nh_a * HEAD_DIM, nh_b * HEAD_DIM
    xn = _rmsnorm(xp, ln_g)
    w = w_in.astype(BF16)
    g0 = 4 * d_a
    g1 = g0 + 4 * nh_a
    u_a = _matmul([xn], [w[:, :g0]], BF16)
    w_g = jnp.zeros((w.shape[0], 128), BF16).at[:, :4 * nh_a].set(w[:, g0:g1])
    gates = _matmul([xn], [w_g], F32)
    gain = jnp.concatenate([jnp.tile(qnorm_g, nh_b), jnp.tile(knorm_g, nh_b)]).reshape(1, -1).astype(F32)
    qk_b = _matmul([xn], [w[:, g1:g1 + 2 * d_b]], BF16, epi="headnorm", extra=(gain,), tn=1024)
    v_b = _matmul([xn], [w[:, g1 + 2 * d_b:]], BF16)
    h_f, h_b = _mlstm(u_a, gates, gate_bias, seqs, nh_a)
    y_a = _mlstm_gate(h_f, h_b, u_a, hnorm_g)
    y_b = _neighbourhood_attention(qk_b, v_b, _na_bias_table(rel_bias), seqs, seg)
    wo = w_out.astype(BF16)
    return _matmul([y_a, y_b], [wo[:d_a], wo[d_a:]], F32, epi="residual", extra=xp)


def _dilated_layer(xp, seqs, ln_g, w_in, qnorm_g, knorm_g, w_out):
    d_c = w_in.shape[1] // 3
    nh = d_c // HEAD_DIM
    xn = _rmsnorm(xp, ln_g)
    w = w_in.astype(BF16)
    tm = 1024
    max_len = max(l for _, l in seqs)
    tables = _rope_tables(max_len)
    pos_tbl = [(s // tm, s // tm) for s, _ in seqs]
    pos_blk = lambda i: i - _seq_lookup(i, pos_tbl, 0)
    gain = jnp.concatenate([jnp.tile(qnorm_g, nh), jnp.tile(knorm_g, nh)]).reshape(1, -1).astype(F32)
    qk = _matmul([xn], [w[:, :2 * d_c]], F32, epi="headnorm_rope", extra=(gain,) + tables,
                 pos_blk=pos_blk, tm=tm, tn=1024)
    v = _matmul([xn], [w[:, 2 * d_c:]], F32, tm=tm)
    y = _dilated_attention(qk, v, seqs)
    return _matmul([y], [w_out.astype(BF16)], F32, epi="residual", extra=xp)


def kernel(x_prompt, x_sample, ab_ln, ab_w_in, ab_gate_bias, ab_hnorm, ab_qnorm, ab_knorm, ab_relbias, ab_w_out,
           c_ln, c_w_in, c_qnorm, c_knorm, c_w_out, ffn_ln, w_router, router_bias, w_gate, w_up, w_down):
    bp, lp, d = x_prompt.shape
    bs, ls, _ = x_sample.shape
    tp = bp * lp
    seqs = [(b * lp, lp) for b in range(bp)] + [(tp + b * ls, ls) for b in range(bs)]
    seg = math.gcd(lp, ls)
    xp = (x_prompt.reshape(tp, d), x_sample.reshape(bs * ls, d))
    depth = ffn_ln.shape[0]
    for layer in range(depth):
        j = layer // 2
        if layer % 2 == 0:
            x = _ab_layer(xp, seqs, seg, ab_ln[j], ab_w_in[j], ab_gate_bias[j], ab_hnorm[j].reshape(-1),
                          ab_qnorm[j], ab_knorm[j], ab_relbias[j], ab_w_out[j])
        else:
            x = _dilated_layer(xp, seqs, c_ln[j], c_w_in[j], c_qnorm[j], c_knorm[j], c_w_out[j])
        xp = _moe(x, ffn_ln[layer], w_router, router_bias, layer, w_gate, w_up, w_down,
                  out_rows=(tp, bs * ls) if layer == depth - 1 else (tp + bs * ls,))
    return xp[0].reshape(bp, lp, d), xp[1].reshape(bs, ls, d)
```

```python
import functools
import math

import jax
import jax.numpy as jnp
import numpy as np
from jax import lax
from jax.experimental import pallas as pl
from jax.experimental.pallas import tpu as pltpu

HEAD_DIM = 128
MLSTM_CHUNK = 128
GRID_W = 64
NA_ROWS = 8
NA_COLS = 16
DIL_PAIRS = ((128, 1), (512, 4), (2048, 16))
ROPE_THETA = 500000.0
ROPE_DIM = HEAD_DIM // 4
N_GROUPS = 4
TOP_K = 2
EPS = 1e-6
NEG = -1e30
VMEM_LIMIT_BYTES = 56 * 1024 * 1024

F32 = jnp.float32
BF16 = jnp.bfloat16


def _cparams(*sem):
    return pltpu.CompilerParams(dimension_semantics=sem, vmem_limit_bytes=VMEM_LIMIT_BYTES)


def _dot(a, b):
    return jnp.dot(a, b, preferred_element_type=F32)


def _dot_nt(a, b):
    return lax.dot_general(a, b, (((1,), (1,)), ((), ())), preferred_element_type=F32)


def _dot_tn(a, b):
    return lax.dot_general(a, b, (((0,), (0,)), ((), ())), preferred_element_type=F32)


def _split3(x):
    hi = x.astype(BF16)
    r1 = x - hi.astype(F32)
    mid = r1.astype(BF16)
    lo = (r1 - mid.astype(F32)).astype(BF16)
    return hi, mid, lo


def _seq_lookup(u, table, default):
    out = default
    for thr, val in table:
        out = jnp.where(u >= thr, val, out)
    return out


def _rmsnorm_kernel(*refs, starts):
    g_ref, o_ref = refs[-2:]
    x = _pick_part(refs[:-2], starts, pl.program_id(0))
    ms = jnp.mean(x * x, axis=-1, keepdims=True)
    o_ref[...] = (x * lax.rsqrt(ms + EPS) * g_ref[...]).astype(o_ref.dtype)


def _rmsnorm(parts, g, tm=512):
    d = parts[0].shape[1]
    t = sum(p.shape[0] for p in parts)
    specs, starts = _part_specs(parts, tm, d, lambda: 0)
    return pl.pallas_call(
        functools.partial(_rmsnorm_kernel, starts=starts),
        grid=(t // tm,),
        in_specs=specs + [pl.BlockSpec((1, d), lambda i: (0, 0))],
        out_specs=pl.BlockSpec((tm, d), lambda i: (i, 0)),
        out_shape=jax.ShapeDtypeStruct((t, d), BF16),
        compiler_params=_cparams("parallel"),
        name="rmsnorm",
    )(*parts, g.reshape(1, d).astype(F32))


EPI_COLS = 256


def _head_mats(n):
    k = lax.broadcasted_iota(jnp.int32, (n, n), 0)
    i = lax.broadcasted_iota(jnp.int32, (n, n), 1)
    shift = HEAD_DIM.bit_length() - 1
    same = jnp.right_shift(k, shift) == jnp.right_shift(i, shift)
    kl, il = jnp.bitwise_and(k, HEAD_DIM - 1), jnp.bitwise_and(i, HEAD_DIM - 1)
    half = ROPE_DIM // 2
    partner = jnp.where(il < half, il + half, jnp.where(il < 2 * half, il - half, -1))
    ones = jnp.where(same, 1.0, 0.0).astype(BF16)
    swap = jnp.where(jnp.logical_and(same, kl == partner), 1.0, 0.0).astype(BF16)
    return ones, swap


def _head_epilogue(acc, gain, rope, mats):
    ones, swap = mats
    y = acc * lax.rsqrt(_dot((acc * acc).astype(BF16), ones) * (1.0 / HEAD_DIM) + EPS) * gain
    if rope is not None:
        reps = acc.shape[1] // HEAD_DIM
        c, s = (jnp.concatenate([t] * reps, axis=1) for t in rope)
        y = y * c + _dot(y.astype(BF16), swap) * s
    return y


def _pick_part(refs, starts, i):
    val = refs[0][...]
    for ref, st in zip(refs[1:], starts[1:]):
        val = jnp.where(i >= st, ref[...], val)
    return val


def _mm_kernel(*refs, n_in, epi, res_starts):
    xs, ws, rest = refs[:n_in], refs[n_in:2 * n_in], refs[2 * n_in:]
    o_ref = rest[-1]
    if epi in ("headnorm", "headnorm_rope"):
        rope = (rest[1][...], rest[2][...]) if epi == "headnorm_rope" else None
        mats = _head_mats(EPI_COLS)
        x = xs[0][...]
        for c0 in range(0, o_ref.shape[1], 2 * EPI_COLS):
            acc = _dot(x, ws[0][:, c0:c0 + 2 * EPI_COLS])
            for h0 in range(0, acc.shape[1], EPI_COLS):
                cs = slice(c0 + h0, c0 + h0 + EPI_COLS)
                o_ref[:, cs] = _head_epilogue(acc[:, h0:h0 + EPI_COLS], rest[0][:, cs], rope, mats).astype(o_ref.dtype)
        return
    acc = _dot(xs[0][...], ws[0][...])
    for x_ref, w_ref in zip(xs[1:], ws[1:]):
        acc = acc + _dot(x_ref[...], w_ref[...])
    if epi == "plain":
        o_ref[...] = acc.astype(o_ref.dtype)
    elif epi == "residual":
        i = pl.program_id(0)
        ends = res_starts[1:] + [None]
        for ref, st, en in zip(rest[:-1], res_starts, ends):
            @pl.when(i >= st if en is None else jnp.logical_and(i >= st, i < en))
            def _(ref=ref):
                o_ref[...] = (acc + ref[...]).astype(o_ref.dtype)
    else:
        raise ValueError(epi)


def _part_specs(parts, tm, block_cols, col_of):
    specs, starts, st = [], [], 0
    for p in parts:
        nb = p.shape[0] // tm
        specs.append(pl.BlockSpec((tm, block_cols), lambda i, *j, st=st, nb=nb: (jnp.clip(i - st, 0, nb - 1), col_of(*j))))
        starts.append(st)
        st += nb
    return specs, starts


def _matmul(xs, ws, out_dtype, epi="plain", extra=(), pos_blk=None, tm=1024, tn=1024):
    t = xs[0].shape[0]
    n = ws[0].shape[1]
    tm, tn = min(tm, t), min(tn, n)
    in_specs = [pl.BlockSpec((tm, x.shape[1]), lambda i, j: (i, 0)) for x in xs]
    in_specs += [pl.BlockSpec((w.shape[0], tn), lambda i, j: (0, j)) for w in ws]
    res_starts = None
    if epi == "residual":
        specs, res_starts = _part_specs(extra, tm, tn, lambda j: j)
        in_specs += specs
    elif epi in ("headnorm", "headnorm_rope"):
        in_specs.append(pl.BlockSpec((1, tn), lambda i, j: (0, j)))
        if epi == "headnorm_rope":
            in_specs += [pl.BlockSpec((tm, HEAD_DIM), lambda i, j: (pos_blk(i), 0))] * 2
    return pl.pallas_call(
        functools.partial(_mm_kernel, n_in=len(xs), epi=epi, res_starts=res_starts),
        grid=(t // tm, n // tn),
        in_specs=in_specs,
        out_specs=pl.BlockSpec((tm, tn), lambda i, j: (i, j)),
        out_shape=jax.ShapeDtypeStruct((t, n), out_dtype),
        compiler_params=_cparams("parallel", "parallel"),
        name="matmul_" + epi,
    )(*xs, *ws, *extra)


def _mlstm_kernel(qf_ref, kf_ref, vf_ref, gf_ref, qb_ref, kb_ref, vb_ref, gb_ref, bias_ref,
                  hf_ref, hb_ref, s_ref, m_ref, *, nh, nchunks, start_chunks, last_chunks):
    c = pl.program_id(0)
    cb = nchunks - 1 - c
    ch = MLSTM_CHUNK
    is_start = functools.reduce(jnp.logical_or, [c == s for s in start_chunks])
    is_last = functools.reduce(jnp.logical_or, [cb == e for e in last_chunks])

    def _reset(d):
        s_ref[d] = jnp.zeros((nh, HEAD_DIM, 2 * HEAD_DIM), F32)
        m_ref[d] = jnp.zeros((nh, 1, HEAD_DIM), F32)

    pl.when(is_start)(lambda: _reset(0))
    pl.when(is_last)(lambda: _reset(1))

    row = lax.broadcasted_iota(jnp.int32, (ch, ch), 0)
    col = lax.broadcasted_iota(jnp.int32, (ch, ch), 1)
    scale = HEAD_DIM ** -0.5
    heads = lambda ref: jnp.stack([ref[:, h * HEAD_DIM:(h + 1) * HEAD_DIM] for h in range(nh)])
    bdot = lambda a, b, ca, cb: lax.dot_general(a, b, (((ca,), (cb,)), ((0,), (0,))), preferred_element_type=F32)

    for d, (q_ref, k_ref, v_ref, g_ref, h_ref) in enumerate(
            ((qf_ref, kf_ref, vf_ref, gf_ref, hf_ref), (qb_ref, kb_ref, vb_ref, gb_ref, hb_ref))):
        mask = (row >= col) if d == 0 else (col >= row)
        icol0 = 2 * nh * d
        fcol0 = icol0 + nh
        tot_row = ch - 1 if d == 0 else 0
        g = g_ref[...] + bias_ref[...]
        lf = jnp.minimum(g, 0.0) - jnp.log1p(jnp.exp(-jnp.abs(g)))
        tri = jnp.where(mask, 1.0, 0.0).astype(BF16)
        hi, mid, lo = _split3(lf)
        bcum = _dot(tri, hi) + _dot(tri, mid) + _dot(tri, lo)
        a = g - pltpu.roll(bcum, HEAD_DIM - nh, 1)
        a_t = a.T
        bcol = jnp.stack([bcum[:, fcol0 + h:fcol0 + h + 1] for h in range(nh)])
        acol = jnp.stack([a[:, icol0 + h:icol0 + h + 1] for h in range(nh)])
        arow = jnp.stack([a_t[icol0 + h:icol0 + h + 1, :] for h in range(nh)])
        b_tot = bcol[:, tot_row:tot_row + 1, :]
        m_st = m_ref[d][:, :, 0:1]
        s_prev = s_ref[d]

        log_d = jnp.where(mask[None], bcol + arow, NEG)
        m_intra = jnp.max(log_d, axis=2, keepdims=True)
        log_inter = bcol + m_st
        m_row = jnp.maximum(log_inter, m_intra)
        dmat = jnp.exp(log_d - m_row)
        q = heads(q_ref)
        ks = (heads(k_ref).astype(F32) * scale).astype(BF16)
        v1 = jnp.concatenate([heads(v_ref), jnp.ones((nh, ch, HEAD_DIM), BF16)], axis=2)
        p = bdot(q, ks, 2, 2) * dmat
        w_inter = jnp.exp(log_inter - m_row)
        nd = bdot(p.astype(BF16), v1, 2, 1) + w_inter * bdot(q, s_prev.astype(BF16), 2, 1)
        h_out = nd[:, :, :HEAD_DIM] / jnp.maximum(jnp.abs(nd[:, :, HEAD_DIM:]), jnp.exp(-m_row))
        for h in range(nh):
            h_ref[:, h * HEAD_DIM:(h + 1) * HEAD_DIM] = h_out[h]

        m_kv = b_tot + jnp.max(arow, axis=2, keepdims=True)
        m_new = jnp.maximum(b_tot + m_st, m_kv)
        decay = jnp.exp(b_tot + m_st - m_new)
        wexp = jnp.exp(b_tot + acol - m_new)
        kw = ks.astype(F32) * wexp
        s_ref[d] = decay * s_prev + bdot(kw.astype(BF16), v1, 1, 1)
        m_ref[d] = jnp.broadcast_to(m_new, (nh, 1, HEAD_DIM))


def _mlstm(u_a, gates, gate_bias, seqs, nh):
    t = u_a.shape[0]
    ch = MLSTM_CHUNK
    nchunks = t // ch
    da = nh * HEAD_DIM
    start_chunks = [s // ch for s, _ in seqs]
    last_chunks = [(s + l) // ch - 1 for s, l in seqs]
    fwd = lambda j: pl.BlockSpec((ch, da), lambda c, j=j: (c, j))
    bwd = lambda j: pl.BlockSpec((ch, da), lambda c, j=j: (nchunks - 1 - c, j))
    gspec_f = pl.BlockSpec((ch, 128), lambda c: (c, 0))
    gspec_b = pl.BlockSpec((ch, 128), lambda c: (nchunks - 1 - c, 0))
    bias = jnp.zeros((1, 128), F32).at[0, :4 * nh].set(gate_bias.reshape(-1).astype(F32))
    return pl.pallas_call(
        functools.partial(_mlstm_kernel, nh=nh, nchunks=nchunks, start_chunks=start_chunks,
                          last_chunks=last_chunks),
        grid=(nchunks,),
        in_specs=[fwd(0), fwd(1), fwd(2), gspec_f, bwd(0), bwd(1), bwd(2), gspec_b,
                  pl.BlockSpec((1, 128), lambda c: (0, 0))],
        out_specs=[pl.BlockSpec((ch, da), lambda c: (c, 0)),
                   pl.BlockSpec((ch, da), lambda c: (nchunks - 1 - c, 0))],
        out_shape=[jax.ShapeDtypeStruct((t, da), F32)] * 2,
        scratch_shapes=[pltpu.VMEM((2, nh, HEAD_DIM, 2 * HEAD_DIM), F32),
                        pltpu.VMEM((2, nh, 1, HEAD_DIM), F32)],
        compiler_params=_cparams("arbitrary"),
        name="mlstm",
    )(u_a, u_a, u_a, gates, u_a, u_a, u_a, gates, bias)


def _mlstm_gate_kernel(hf_ref, hb_ref, o_ref, g_ref, y_ref):
    hsum = hf_ref[...] + hb_ref[...]
    o = o_ref[...].astype(F32)
    g = g_ref[...]
    outs = []
    for h in range(hsum.shape[1] // HEAD_DIM):
        hs = slice(h * HEAD_DIM, (h + 1) * HEAD_DIM)
        a = hsum[:, hs]
        y = a * lax.rsqrt(jnp.mean(a * a, axis=-1, keepdims=True) + EPS) * g[:, hs]
        outs.append(jax.nn.sigmoid(o[:, hs]) * y)
    y_ref[...] = jnp.concatenate(outs, axis=1).astype(y_ref.dtype)


def _mlstm_gate(h_f, h_b, u_a, hnorm_g, tm=512):
    t, da = h_f.shape
    return pl.pallas_call(
        _mlstm_gate_kernel,
        grid=(t // tm,),
        in_specs=[pl.BlockSpec((tm, da), lambda i: (i, 0)), pl.BlockSpec((tm, da), lambda i: (i, 0)),
                  pl.BlockSpec((tm, da), lambda i: (i, 3)), pl.BlockSpec((1, da), lambda i: (0, 0))],
        out_specs=pl.BlockSpec((tm, da), lambda i: (i, 0)),
        out_shape=jax.ShapeDtypeStruct((t, da), BF16),
        compiler_params=_cparams("parallel"),
        name="mlstm_gate",
    )(h_f, h_b, u_a, hnorm_g.reshape(1, da).astype(F32))


NA_QROWS = 4
NA_KROWS = NA_QROWS + NA_ROWS
NA_UNROLL = 8


def _na_bias_table(rel_bias):
    h = rel_bias.shape[0]
    nr, nc = 2 * NA_ROWS - 1, 2 * NA_COLS - 1
    c = np.arange(GRID_W)
    dc = np.clip(c[None, :] - c[:, None], -(NA_COLS - 1), NA_COLS - 1) + NA_COLS - 1
    c_start = np.clip(c - NA_COLS // 2, 0, GRID_W - NA_COLS)
    col_ok = (c[None, :] >= c_start[:, None]) & (c[None, :] < c_start[:, None] + NA_COLS)
    var = np.arange(3)[:, None, None]
    qi = np.arange(NA_QROWS)[None, :, None]
    kr = np.arange(NA_KROWS)[None, None, :]
    dr = kr - var * NA_QROWS - qi
    first = np.where(var == 0, 0, np.where(var == 1, qi, NA_QROWS))
    row_ok = (kr >= first) & (kr < first + NA_ROWS)
    dr_idx = np.clip(dr + NA_ROWS - 1, 0, nr - 1).reshape(-1)
    onehot = (dc.reshape(1, -1) == np.arange(nc)[:, None]).astype(np.float32)
    cols = jnp.dot(rel_bias.astype(F32).reshape(h * nr, nc), onehot, precision=lax.Precision.HIGHEST)
    cols = cols.reshape(h, nr, GRID_W, GRID_W)
    tbl = jnp.stack([cols[:, int(i)] for i in dr_idx], axis=1)
    tbl = tbl.reshape(h, 3, NA_QROWS, NA_KROWS, GRID_W, GRID_W)
    ok = row_ok[:, :, :, None, None] & col_ok[None, None, None, :, :]
    tbl = jnp.where(ok[None], tbl, NEG)
    return jnp.transpose(tbl, (0, 1, 2, 4, 3, 5)).reshape(h, 3, NA_QROWS * GRID_W, NA_KROWS * GRID_W)


def _na_kernel(q_ref, kp_ref, km_ref, kn_ref, vp_ref, vm_ref, vn_ref, bias_ref, o_ref, kcat, vcat,
               *, seg_rows, row_lo, row_hi):
    s = pl.program_id(1)
    halo = NA_ROWS * GRID_W
    seg = seg_rows * GRID_W
    kcat[0:halo] = kp_ref[...]
    kcat[halo:halo + seg] = km_ref[...]
    kcat[halo + seg:halo + seg + halo] = kn_ref[...]
    vcat[0:halo, 0:HEAD_DIM] = vp_ref[...]
    vcat[halo:halo + seg, 0:HEAD_DIM] = vm_ref[...]
    vcat[halo + seg:halo + seg + halo, 0:HEAD_DIM] = vn_ref[...]
    vcat[:, HEAD_DIM:] = jnp.ones((seg + 2 * halo, HEAD_DIM), BF16)
    r_lo = _seq_lookup(s, row_lo, 0)
    r_hi = _seq_lookup(s, row_hi, 0)
    scale = HEAD_DIM ** -0.5
    nq = NA_QROWS * GRID_W
    nk = NA_KROWS * GRID_W

    def one_group(gl):
        r0 = s * seg_rows + gl * NA_QROWS
        u = jnp.clip(r0 - NA_ROWS // 2, r_lo, r_hi - NA_KROWS)
        off = pl.multiple_of((u - s * seg_rows + NA_ROWS) * GRID_W, GRID_W)
        qoff = pl.multiple_of(gl * nq, nq)
        q = q_ref[pl.ds(qoff, nq), :]
        sc = _dot_nt(q, kcat[pl.ds(off, nk), :]) * scale + bias_ref[(r0 - u) // NA_QROWS]
        e = jnp.exp(sc - jnp.max(sc, axis=1, keepdims=True))
        nd = _dot(e.astype(BF16), vcat[pl.ds(off, nk), :])
        o_ref[pl.ds(qoff, nq), :] = (nd[:, :HEAD_DIM] / nd[:, HEAD_DIM:]).astype(o_ref.dtype)

    def body(it, carry):
        for k in range(NA_UNROLL):
            one_group(it * NA_UNROLL + k)
        return carry

    lax.fori_loop(0, seg_rows // (NA_QROWS * NA_UNROLL), body, 0)


def _neighbourhood_attention(qk, v, bias_tbl, seqs, seg):
    t, dh = v.shape
    nh = dh // HEAD_DIM
    nseg = t // seg
    seg_rows = seg // GRID_W
    assert all(l // GRID_W >= NA_KROWS and (s // GRID_W) % NA_QROWS == 0 and (l // GRID_W) % NA_QROWS == 0
               for s, l in seqs)
    halo = NA_ROWS * GRID_W
    hpb = seg // halo
    nhb = t // halo
    row_lo = [(s // seg, s // GRID_W) for s, _ in seqs]
    row_hi = [(s // seg, (s + l) // GRID_W) for s, l in seqs]
    main = lambda c0: pl.BlockSpec((seg, HEAD_DIM), lambda h, s: (s, c0 + h))
    prev = lambda c0: pl.BlockSpec((halo, HEAD_DIM), lambda h, s: (jnp.maximum(s * hpb - 1, 0), c0 + h))
    nxt = lambda c0: pl.BlockSpec((halo, HEAD_DIM), lambda h, s: (jnp.minimum((s + 1) * hpb, nhb - 1), c0 + h))
    return pl.pallas_call(
        functools.partial(_na_kernel, seg_rows=seg_rows, row_lo=row_lo, row_hi=row_hi),
        grid=(nh, nseg),
        in_specs=[main(0), prev(nh), main(nh), nxt(nh), prev(0), main(0), nxt(0),
                  pl.BlockSpec((None, 3, NA_QROWS * GRID_W, NA_KROWS * GRID_W), lambda h, s: (h, 0, 0, 0))],
        out_specs=main(0),
        out_shape=jax.ShapeDtypeStruct((t, dh), BF16),
        scratch_shapes=[pltpu.VMEM((seg + 2 * halo, HEAD_DIM), BF16), pltpu.VMEM((seg + 2 * halo, 2 * HEAD_DIM), BF16)],
        compiler_params=_cparams("parallel", "parallel"),
        name="natten",
    )(qk, qk, qk, qk, v, v, v, bias_tbl)


DIL_BLK = 128
DIL_RADIUS = 64
DIL_MAX = max(d for _, d in DIL_PAIRS)
DIL_QB = DIL_BLK * DIL_MAX
DIL_HALO = DIL_RADIUS * DIL_MAX
DIL_NAT_MAX = 1
DIL_NAT_HALO = DIL_RADIUS * max(d for _, d in DIL_PAIRS if d <= DIL_NAT_MAX)


def _dilated_natural(b, d, q_ref, kcat, vcat, has_prev, has_next, num_s, mx_s, den_s):
    halo = DIL_RADIUS * d
    nk = DIL_BLK + 2 * halo
    i_io = lax.broadcasted_iota(jnp.int32, (DIL_BLK, nk), 0)
    j_io = lax.broadcasted_iota(jnp.int32, (DIL_BLK, nk), 1)
    rel = j_io - halo - i_io
    ok = jnp.logical_and(jnp.abs(rel) <= halo, jnp.bitwise_and(rel, d - 1) == 0)
    base = jnp.where(ok, 0.0, NEG)
    scale = HEAD_DIM ** -0.5
    for t in range(DIL_QB // DIL_BLK):
        k0 = DIL_NAT_HALO + DIL_BLK * t - halo
        n_prev = max(DIL_NAT_HALO - k0, 0)
        n_next = max(k0 + nk - (DIL_NAT_HALO + DIL_QB), 0)
        madd = base
        if n_prev:
            madd = jnp.where(jnp.logical_or(j_io >= n_prev, has_prev), madd, NEG)
        if n_next:
            madd = jnp.where(jnp.logical_or(j_io < nk - n_next, has_next), madd, NEG)
        qrows = pl.ds(DIL_BLK * t, DIL_BLK)
        sc = _dot_nt(q_ref[qrows, :].astype(BF16), kcat[k0:k0 + nk, :]) * scale + madd
        mx = jnp.max(sc, axis=1, keepdims=True)
        nd = _dot(jnp.exp(sc - mx).astype(BF16), vcat[k0:k0 + nk, :])
        num_s[b, qrows, :] = nd[:, :HEAD_DIM]
        mx_s[b, qrows, :] = jnp.broadcast_to(mx, (DIL_BLK, HEAD_DIM))
        den_s[b, qrows, :] = nd[:, HEAD_DIM:]


def _dilated_classes(b, d, q_ref, k_refs, v_refs, has_prev, has_next, num_s, mx_s, den_s):
    per_class = DIL_QB // d
    halo = DIL_RADIUS * d
    to_classes = lambda x: pltpu.einshape("mgd->gmd", x.reshape(x.shape[0] // d, d, HEAD_DIM))
    to_natural = lambda x: pltpu.einshape("gmd->mgd", x).reshape(DIL_QB, HEAD_DIM)
    bdot = lambda a, c, ca, cc: lax.dot_general(a, c, (((ca,), (cc,)), ((0,), (0,))), preferred_element_type=F32)
    nk = DIL_BLK + 2 * DIL_RADIUS

    def with_halo(refs):
        prv, cur, nxt = refs
        parts = [to_classes(prv[DIL_HALO - halo:DIL_HALO, :]), to_classes(cur[...]), to_classes(nxt[0:halo, :])]
        return jnp.concatenate(parts, axis=1).astype(BF16)

    qd = to_classes(q_ref[...]).astype(BF16)
    kd = with_halo(k_refs)
    vd = with_halo(v_refs)
    vd = jnp.concatenate([vd, jnp.ones(vd.shape, BF16)], axis=2)
    i_io = lax.broadcasted_iota(jnp.int32, (DIL_BLK, nk), 0)
    j_io = lax.broadcasted_iota(jnp.int32, (DIL_BLK, nk), 1)
    band = jnp.abs(j_io - DIL_RADIUS - i_io) <= DIL_RADIUS
    nj = per_class // DIL_BLK
    nums, mxs, dens = [], [], []
    for j in range(nj):
        ok = band
        if j == 0:
            ok = jnp.logical_and(ok, jnp.logical_or(j_io >= DIL_RADIUS, has_prev))
        if j == nj - 1:
            ok = jnp.logical_and(ok, jnp.logical_or(j_io < DIL_BLK + DIL_RADIUS, has_next))
        qs = slice(DIL_BLK * j, DIL_BLK * (j + 1))
        ks = slice(DIL_BLK * j, DIL_BLK * j + nk)
        sc = bdot(qd[:, qs], kd[:, ks], 2, 2) * (HEAD_DIM ** -0.5) + jnp.where(ok, 0.0, NEG)[None]
        mx = jnp.max(sc, axis=2, keepdims=True)
        nd = bdot(jnp.exp(sc - mx).astype(BF16), vd[:, ks], 2, 1)
        nums.append(nd[:, :, :HEAD_DIM])
        mxs.append(jnp.broadcast_to(mx, (d, DIL_BLK, HEAD_DIM)))
        dens.append(nd[:, :, HEAD_DIM:])
    cat = lambda xs: xs[0] if len(xs) == 1 else jnp.concatenate(xs, axis=1)
    num_s[b] = to_natural(cat(nums))
    mx_s[b] = to_natural(cat(mxs))
    den_s[b] = to_natural(cat(dens))


def _dilated_kernel(q_ref, kp_ref, kc_ref, kn_ref, vp_ref, vc_ref, vn_ref, y_ref, num_s, mx_s, den_s, kcat, vcat,
                    *, first_blocks, last_blocks):
    i = pl.program_id(1)
    has_prev = jnp.logical_not(functools.reduce(jnp.logical_or, [i == b for b in first_blocks]))
    has_next = jnp.logical_not(functools.reduce(jnp.logical_or, [i == b for b in last_blocks]))
    h = DIL_NAT_HALO
    kcat[0:h] = kp_ref[DIL_HALO - h:DIL_HALO, :].astype(BF16)
    kcat[h:h + DIL_QB] = kc_ref[...].astype(BF16)
    kcat[h + DIL_QB:h + DIL_QB + h] = kn_ref[0:h, :].astype(BF16)
    vcat[0:h, 0:HEAD_DIM] = vp_ref[DIL_HALO - h:DIL_HALO, :].astype(BF16)
    vcat[h:h + DIL_QB, 0:HEAD_DIM] = vc_ref[...].astype(BF16)
    vcat[h + DIL_QB:h + DIL_QB + h, 0:HEAD_DIM] = vn_ref[0:h, :].astype(BF16)
    vcat[:, HEAD_DIM:] = jnp.ones((DIL_QB + 2 * h, HEAD_DIM), BF16)
    nk = DIL_BLK + 2 * DIL_RADIUS
    j_io = lax.broadcasted_iota(jnp.int32, (DIL_BLK, nk), 0)
    s_io = lax.broadcasted_iota(jnp.int32, (DIL_BLK, nk), 1)
    band = jnp.abs(s_io - DIL_RADIUS - j_io) <= DIL_RADIUS
    ok_prev = jnp.logical_or(s_io >= DIL_RADIUS, has_prev)
    ok_next = jnp.logical_or(s_io < DIL_BLK + DIL_RADIUS, has_next)
    madd = {}
    for up in (False, True):
        for un in (False, True):
            ok = band
            ok = jnp.logical_and(ok, ok_prev) if up else ok
            ok = jnp.logical_and(ok, ok_next) if un else ok
            madd[up, un] = jnp.where(ok, 0.0, NEG)
    scale = HEAD_DIM ** -0.5
    ones_v = jnp.ones((nk, HEAD_DIM), BF16)

    def rows(start, size, d):
        return pl.ds(start, size, stride=d) if d > 1 else pl.ds(start, size)

    for b, (_, d) in enumerate(DIL_PAIRS):
        if d <= DIL_NAT_MAX:
            _dilated_natural(b, d, q_ref, kcat, vcat, has_prev, has_next, num_s, mx_s, den_s)
            continue
        if d % 8 == 0:
            _dilated_classes(b, d, q_ref, (kp_ref, kc_ref, kn_ref), (vp_ref, vc_ref, vn_ref), has_prev, has_next,
                             num_s, mx_s, den_s)
            continue
        per_class = DIL_QB // d
        nj = per_class // DIL_BLK
        for g in range(d):
            for j in range(nj):
                up, un = j == 0, j == nj - 1
                qrows = rows(g + d * DIL_BLK * j, DIL_BLK, d)
                m0 = max(DIL_BLK * j - DIL_RADIUS, 0)
                m1 = min(DIL_BLK * (j + 1) + DIL_RADIUS, per_class)
                kparts, vparts = [], []
                if up:
                    r = rows(DIL_HALO - DIL_RADIUS * d + g, DIL_RADIUS, d)
                    kparts.append(kp_ref[r, :])
                    vparts.append(vp_ref[r, :])
                r = rows(g + d * m0, m1 - m0, d)
                kparts.append(kc_ref[r, :])
                vparts.append(vc_ref[r, :])
                if un:
                    r = rows(g, DIL_RADIUS, d)
                    kparts.append(kn_ref[r, :])
                    vparts.append(vn_ref[r, :])
                kk = jnp.concatenate(kparts, axis=0).astype(BF16)
                vv = jnp.concatenate([jnp.concatenate(vparts, axis=0).astype(BF16), ones_v], axis=1)
                sc = _dot_nt(q_ref[qrows, :].astype(BF16), kk) * scale + madd[up, un]
                mx = jnp.max(sc, axis=1, keepdims=True)
                nd = _dot(jnp.exp(sc - mx).astype(BF16), vv)
                num_s[b, qrows, :] = nd[:, :HEAD_DIM]
                mx_s[b, qrows, :] = jnp.broadcast_to(mx, (DIL_BLK, HEAD_DIM))
                den_s[b, qrows, :] = nd[:, HEAD_DIM:]

    nb = len(DIL_PAIRS)
    mxs = [mx_s[b] for b in range(nb)]
    m_top = functools.reduce(jnp.maximum, mxs)
    w = [jnp.exp(m - m_top) for m in mxs]
    num = functools.reduce(jnp.add, [w[b] * num_s[b] for b in range(nb)])
    den = functools.reduce(jnp.add, [w[b] * den_s[b] for b in range(nb)])
    y_ref[...] = (num / den).astype(y_ref.dtype)


def _dilated_attention(qk, v, seqs):
    t, dh = v.shape
    nh = dh // HEAD_DIM
    assert all(window // (2 * dil) == DIL_RADIUS for window, dil in DIL_PAIRS)
    assert all(s % DIL_QB == 0 and l % DIL_QB == 0 for s, l in seqs)
    nq = t // DIL_QB
    hb = DIL_QB // DIL_HALO
    nhalo = t // DIL_HALO
    first_blocks = [s // DIL_QB for s, _ in seqs]
    last_blocks = [(s + l) // DIL_QB - 1 for s, l in seqs]
    cur = lambda c0: pl.BlockSpec((DIL_QB, HEAD_DIM), lambda h, i: (i, c0 + h))
    prev = lambda c0: pl.BlockSpec((DIL_HALO, HEAD_DIM), lambda h, i: (jnp.maximum(i * hb - 1, 0), c0 + h))
    nxt = lambda c0: pl.BlockSpec((DIL_HALO, HEAD_DIM), lambda h, i: (jnp.minimum((i + 1) * hb, nhalo - 1), c0 + h))
    nb = len(DIL_PAIRS)
    return pl.pallas_call(
        functools.partial(_dilated_kernel, first_blocks=first_blocks, last_blocks=last_blocks),
        grid=(nh, nq),
        in_specs=[cur(0), prev(nh), cur(nh), nxt(nh), prev(0), cur(0), nxt(0)],
        out_specs=cur(0),
        out_shape=jax.ShapeDtypeStruct((t, dh), BF16),
        scratch_shapes=[pltpu.VMEM((nb, DIL_QB, HEAD_DIM), F32)] * 3
        + [pltpu.VMEM((DIL_QB + 2 * DIL_NAT_HALO, HEAD_DIM), BF16),
           pltpu.VMEM((DIL_QB + 2 * DIL_NAT_HALO, 2 * HEAD_DIM), BF16)],
        compiler_params=_cparams("parallel", "parallel"),
        name="dilated",
    )(qk, qk, qk, qk, v, v, v)


def _rope_tables(max_len):
    half = ROPE_DIM // 2
    inv_freq = ROPE_THETA ** (-jnp.arange(half, dtype=F32) / half)
    ang = jnp.arange(max_len, dtype=F32)[:, None] * inv_freq[None, :]
    cos, sin = jnp.cos(ang), jnp.sin(ang)
    rest = HEAD_DIM - ROPE_DIM
    c = jnp.concatenate([cos, cos, jnp.ones((max_len, rest), F32)], axis=1)
    s = jnp.concatenate([-sin, sin, jnp.zeros((max_len, rest), F32)], axis=1)
    return c, s


def _router_kernel(x_ref, g_ref, wr_ref, rb_ref, xn_ref, e_ref, gt_ref, *, n_exp):
    x = x_ref[...]
    xn = x * lax.rsqrt(jnp.mean(x * x, axis=-1, keepdims=True) + EPS) * g_ref[...]
    xn_ref[...] = xn.astype(xn_ref.dtype)
    tm = x.shape[0]
    epg = n_exp // N_GROUPS
    xh, xm, _ = _split3(xn)
    wh, wm, _ = _split3(wr_ref[...])
    logits = _dot_nt(wh, xh) + _dot_nt(wh, xm) + _dot_nt(wm, xh)
    scores = jax.nn.sigmoid(logits)
    sel = (scores + rb_ref[:, 0:1]).reshape(N_GROUPS, epg, tm)
    sc3 = scores.reshape(N_GROUPS, epg, tm)
    io = lax.broadcasted_iota(jnp.int32, (N_GROUPS, epg, tm), 1)
    gio = lax.broadcasted_iota(jnp.int32, (N_GROUPS, 1, tm), 0)
    m1 = jnp.max(sel, axis=1, keepdims=True)
    i1 = jnp.min(jnp.where(sel == m1, io, epg), axis=1, keepdims=True)
    sel_b = jnp.where(io == i1, -jnp.inf, sel)
    m2 = jnp.max(sel_b, axis=1, keepdims=True)
    i2 = jnp.min(jnp.where(sel_b == m2, io, epg), axis=1, keepdims=True)
    gs = m1 + m2
    gbest = jnp.min(jnp.where(gs == jnp.max(gs, axis=0, keepdims=True), gio, N_GROUPS), axis=0, keepdims=True)
    in_g = gio == gbest
    outs_e, outs_w = [], []
    for ik in (i1, i2):
        pick = in_g & (io == ik)
        outs_e.append(jnp.sum(jnp.where(in_g, gbest * epg + ik, 0), axis=0))
        outs_w.append(jnp.sum(jnp.sum(jnp.where(pick, sc3, 0.0), axis=1, keepdims=True), axis=0))
    wsum = outs_w[0] + outs_w[1]
    e_ref[0:1, :] = outs_e[0]
    e_ref[1:2, :] = outs_e[1]
    gt_ref[0:1, :] = outs_w[0] / wsum
    gt_ref[1:2, :] = outs_w[1] / wsum


def _router(x, g, w_router, router_bias, tm=512):
    t, d = x.shape
    n_exp = w_router.shape[1]
    rb = jnp.broadcast_to(router_bias.astype(F32)[:, None], (n_exp, 128))
    return pl.pallas_call(
        functools.partial(_router_kernel, n_exp=n_exp),
        grid=(t // tm,),
        in_specs=[pl.BlockSpec((tm, d), lambda i: (i, 0)), pl.BlockSpec((1, d), lambda i: (0, 0)),
                  pl.BlockSpec((n_exp, d), lambda i: (0, 0)), pl.BlockSpec((n_exp, 128), lambda i: (0, 0))],
        out_specs=[pl.BlockSpec((tm, d), lambda i: (i, 0)), pl.BlockSpec((2, tm), lambda i: (0, i)),
                   pl.BlockSpec((2, tm), lambda i: (0, i))],
        out_shape=[jax.ShapeDtypeStruct((t, d), F32), jax.ShapeDtypeStruct((2, t), jnp.int32),
                   jax.ShapeDtypeStruct((2, t), F32)],
        compiler_params=_cparams("parallel"),
        name="router",
    )(x, g.reshape(1, d).astype(F32), w_router.T.astype(F32), rb)


def _rank_kernel(e_ref, rank_ref, cnt_ref, run_ref, *, n_exp):
    i = pl.program_id(0)
    tm = e_ref.shape[1]

    @pl.when(i == 0)
    def _():
        run_ref[...] = jnp.zeros_like(run_ref)

    eio = lax.broadcasted_iota(jnp.int32, (n_exp, tm), 0)
    oh = [(eio == e_ref[k:k + 1, :]) for k in range(TOP_K)]
    ohf = [jnp.where(o, 1.0, 0.0) for o in oh]
    both = ohf[0] + ohf[1]
    r_i = lax.broadcasted_iota(jnp.int32, (tm, tm), 0)
    c_i = lax.broadcasted_iota(jnp.int32, (tm, tm), 1)
    upper = jnp.where(r_i <= c_i, 1.0, 0.0).astype(BF16)
    cum = _dot(both.astype(BF16), upper)
    base = run_ref[:, 0:1] + cum - both
    for k in range(TOP_K):
        rank_ref[k:k + 1, :] = jnp.sum(jnp.where(oh[k], base, 0.0), axis=0, keepdims=True).astype(jnp.int32)
    run_new = run_ref[...] + cum[:, tm - 1:tm]
    run_ref[...] = run_new
    cnt_ref[...] = run_new.astype(jnp.int32)


def _rank(e, n_exp, tm=512):
    t = e.shape[1]
    return pl.pallas_call(
        functools.partial(_rank_kernel, n_exp=n_exp),
        grid=(t // tm,),
        in_specs=[pl.BlockSpec((2, tm), lambda i: (0, i))],
        out_specs=[pl.BlockSpec((2, tm), lambda i: (0, i)), pl.BlockSpec((n_exp, 128), lambda i: (0, 0))],
        out_shape=[jax.ShapeDtypeStruct((2, t), jnp.int32), jax.ShapeDtypeStruct((n_exp, 128), jnp.int32)],
        scratch_shapes=[pltpu.VMEM((n_exp, 128), F32)],
        compiler_params=_cparams("arbitrary"),
        name="moe_rank",
    )(e)


def _slots_kernel(ps_ref, e_ref, rank_ref, slot_ref, *, n_exp):
    e = e_ref[...]
    start = jnp.zeros_like(e)
    for k in range(n_exp):
        start = jnp.where(e == k, ps_ref[k], start)
    slot_ref[...] = start + rank_ref[...]


def _slots(e, rank, pad_start, tm=2048):
    t = e.shape[1]
    tm = min(tm, t)
    spec = pl.BlockSpec((TOP_K, tm), lambda i, ps: (0, i))
    return pl.pallas_call(
        functools.partial(_slots_kernel, n_exp=pad_start.shape[0]),
        grid_spec=pltpu.PrefetchScalarGridSpec(num_scalar_prefetch=1, grid=(t // tm,), in_specs=[spec, spec],
                                               out_specs=spec),
        out_shape=jax.ShapeDtypeStruct((TOP_K, t), jnp.int32),
        compiler_params=_cparams("parallel"),
        name="moe_slots",
    )(pad_start, e, rank)


MOE_BLK = 512


CAST_ROWS = 256


ROW_UNROLL = 8


def _start_rows(n_rows, make_copy):
    def body(r, c):
        for k in range(TOP_K):
            make_copy(k, r).start(priority=k)
        return c
    lax.fori_loop(0, n_rows, body, 0, unroll=ROW_UNROLL)


def _wait_rows(n_rows, make_copy):
    def body(r, c):
        for k in range(TOP_K):
            make_copy(k, 0).wait()
        return c
    lax.fori_loop(0, n_rows, body, 0, unroll=ROW_UNROLL)


def _dispatch_kernel(fill_ref, *rest, tm, n_fill):
    slot_refs, (xn_ref, xs_out, zbuf, sem, zsem) = rest[:TOP_K], rest[TOP_K:]
    i = pl.program_id(0)

    @pl.when(i == 0)
    def _():
        zbuf[...] = jnp.zeros_like(zbuf)

        def fill(j):
            rows = pl.ds(pl.multiple_of(fill_ref[j] * MOE_BLK, MOE_BLK), MOE_BLK)
            return pltpu.make_async_copy(zbuf, xs_out.at[rows, :], zsem)

        def start(j, c):
            pl.when(fill_ref[n_fill + j] > 0)(lambda: fill(j).start())
            return c

        def wait(j, c):
            pl.when(fill_ref[n_fill + j] > 0)(lambda: fill(j).wait())
            return c

        lax.fori_loop(0, n_fill, start, 0)
        lax.fori_loop(0, n_fill, wait, 0)

    def copy(k, r):
        return pltpu.make_async_copy(xn_ref.at[pl.ds(r, 1), :], xs_out.at[pl.ds(slot_refs[k][r], 1), :], sem)

    _start_rows(tm, copy)
    _wait_rows(tm, copy)


def _dispatch(slots, xn, fill, cap, tm=256):
    t, d = xn.shape
    smem = pl.BlockSpec((tm,), lambda i, f: (i,), memory_space=pltpu.SMEM)
    return pl.pallas_call(
        functools.partial(_dispatch_kernel, tm=tm, n_fill=fill.shape[0] // 2),
        grid_spec=pltpu.PrefetchScalarGridSpec(
            num_scalar_prefetch=1, grid=(t // tm,),
            in_specs=[smem] * TOP_K + [pl.BlockSpec((tm, d), lambda i, f: (i, 0))],
            out_specs=pl.BlockSpec(memory_space=pl.ANY),
            scratch_shapes=[pltpu.VMEM((MOE_BLK, d), F32), pltpu.SemaphoreType.DMA, pltpu.SemaphoreType.DMA]),
        out_shape=jax.ShapeDtypeStruct((cap, d), F32),
        compiler_params=_cparams("arbitrary"),
        name="moe_dispatch",
    )(fill, *slots, xn)


def _dot_casting(x, w_refs, s_refs):
    k = x.shape[1]
    step = min(CAST_ROWS, k)
    accs = [None] * len(w_refs)
    for r0 in range(0, k, step):
        rows = slice(r0, r0 + step)
        for n, (w_ref, s_ref) in enumerate(zip(w_refs, s_refs)):
            wc = w_ref[rows, :].astype(BF16)
            s_ref[rows, :] = wc
            part = _dot(x[:, rows], wc)
            accs[n] = part if accs[n] is None else accs[n] + part
    return accs


def _expert_changed(b, be_ref):
    return jnp.logical_or(b == 0, be_ref[b] != be_ref[jnp.maximum(b - 1, 0)])


def _expert_mm_kernel(be_ref, nb_ref, nx_ref, x_ref, *rest, layer, n_w, finish):
    w_hbm, o_ref, (wf, ws, sem) = rest[:n_w], rest[n_w], rest[n_w + 1:]
    b = pl.program_id(0)
    live = b < nb_ref[0]
    first = _expert_changed(b, be_ref)

    def fetch(e):
        return [pltpu.make_async_copy(w.at[layer, e], wf.at[n], sem.at[n]) for n, w in enumerate(w_hbm)]

    @pl.when(b == 0)
    def _():
        for c in fetch(be_ref[0]):
            c.start()

    @pl.when(jnp.logical_and(live, first))
    def _():
        for c in fetch(be_ref[b]):
            c.wait()
        outs = _dot_casting(x_ref[...].astype(BF16), [wf.at[n] for n in range(n_w)], [ws.at[n] for n in range(n_w)])
        o_ref[...] = finish(outs).astype(o_ref.dtype)

        @pl.when(nx_ref[b] >= 0)
        def _():
            for c in fetch(nx_ref[b]):
                c.start()

    @pl.when(jnp.logical_and(live, jnp.logical_not(first)))
    def _():
        x = x_ref[...].astype(BF16)
        o_ref[...] = finish([_dot(x, ws[n]) for n in range(n_w)]).astype(o_ref.dtype)

    @pl.when(jnp.logical_not(live))
    def _():
        o_ref[...] = jnp.zeros_like(o_ref)


def _swiglu(outs):
    hg, hu = outs
    return (hg * jax.nn.sigmoid(hg)) * hu


def _expert_mm(xs, sched, layer, ws_hbm, finish, out_dtype, name):
    blk_expert, n_used, next_expert = sched
    cap, k = xs.shape
    n = ws_hbm[0].shape[3]
    n_w = len(ws_hbm)
    return pl.pallas_call(
        functools.partial(_expert_mm_kernel, layer=layer, n_w=n_w, finish=finish),
        grid_spec=pltpu.PrefetchScalarGridSpec(
            num_scalar_prefetch=3, grid=(cap // MOE_BLK,),
            in_specs=[pl.BlockSpec((MOE_BLK, k), lambda b, be, nb, nx: (jnp.minimum(b, nb[0] - 1), 0))]
            + [pl.BlockSpec(memory_space=pl.ANY)] * n_w,
            out_specs=pl.BlockSpec((MOE_BLK, n), lambda b, be, nb, nx: (b, 0)),
            scratch_shapes=[pltpu.VMEM((n_w, k, n), F32), pltpu.VMEM((n_w, k, n), BF16),
                            pltpu.SemaphoreType.DMA((n_w,))]),
        out_shape=jax.ShapeDtypeStruct((cap, n), out_dtype),
        compiler_params=_cparams("arbitrary"),
        name=name,
    )(blk_expert, n_used, next_expert, xs, *ws_hbm)


def _expert_ffn(xs, sched, layer, w_gate, w_up, w_down):
    h = _expert_mm(xs, sched, layer, (w_gate, w_up), _swiglu, BF16, "moe_ffn_up")
    return _expert_mm(h, sched, layer, (w_down,), lambda outs: outs[0], F32, "moe_ffn_down")


def _combine_kernel(*refs, tm, nsteps, out_starts, out_nblk, with_norm):
    slot_cur, slot_nxt = refs[:TOP_K], refs[TOP_K:2 * TOP_K]
    x_ref, gt_ref, ys_hbm = refs[2 * TOP_K:2 * TOP_K + 3]
    rest, (buf, sem) = refs[2 * TOP_K + 3:-2], refs[-2:]
    g_ref, rest = (rest[0], rest[1:]) if with_norm else (None, rest)
    o_refs, xn_ref = (rest[:-1], rest[-1]) if with_norm else (rest, None)
    i = pl.program_id(0)
    cur = lax.rem(i, 2)

    def gather(slot_refs, b):
        def copy(k, r):
            return pltpu.make_async_copy(ys_hbm.at[pl.ds(slot_refs[k][r], 1), :], buf.at[b, k, pl.ds(r, 1), :],
                                         sem.at[b])
        return copy

    pl.when(i == 0)(lambda: _start_rows(tm, gather(slot_cur, 0)))
    pl.when(i + 1 < nsteps)(lambda: _start_rows(tm, gather(slot_nxt, 1 - cur)))
    _wait_rows(tm, gather(slot_cur, cur))
    out = x_ref[...] + (gt_ref[:, 0:1] * buf[cur, 0] + gt_ref[:, 1:2] * buf[cur, 1])
    for o_ref, st, nb in zip(o_refs, out_starts, out_nblk):
        @pl.when(jnp.logical_and(i >= st, i < st + nb))
        def _(o_ref=o_ref):
            o_ref[...] = out
    if with_norm:
        ms = jnp.mean(out * out, axis=-1, keepdims=True)
        xn_ref[...] = (out * lax.rsqrt(ms + EPS) * g_ref[...]).astype(xn_ref.dtype)


def _combine(slots, x, gt_cols, ys, out_rows, norm_g=None, tm=256):
    t, d = x.shape
    nsteps = t // tm
    with_norm = norm_g is not None
    extra_in = [norm_g.reshape(1, d).astype(F32)] if with_norm else []
    extra_spec = [pl.BlockSpec((1, d), lambda i: (0, 0))] if with_norm else []
    extra_out = [pl.BlockSpec((tm, d), lambda i: (i, 0))] if with_norm else []
    extra_shape = [jax.ShapeDtypeStruct((t, d), BF16)] if with_norm else []
    smem = lambda nxt: pl.BlockSpec((tm,), lambda i: (jnp.minimum(i + nxt, nsteps - 1),), memory_space=pltpu.SMEM)
    out_nblk = [r // tm for r in out_rows]
    out_starts = [sum(out_nblk[:n]) for n in range(len(out_nblk))]
    out_specs = [pl.BlockSpec((tm, d), lambda i, st=st, nb=nb: (jnp.clip(i - st, 0, nb - 1), 0))
                 for st, nb in zip(out_starts, out_nblk)]
    return pl.pallas_call(
        functools.partial(_combine_kernel, tm=tm, nsteps=nsteps, out_starts=out_starts, out_nblk=out_nblk,
                          with_norm=with_norm),
        grid=(nsteps,),
        in_specs=[smem(0)] * TOP_K + [smem(1)] * TOP_K
        + [pl.BlockSpec((tm, d), lambda i: (i, 0)), pl.BlockSpec((tm, TOP_K), lambda i: (i, 0)),
           pl.BlockSpec(memory_space=pl.ANY)] + extra_spec,
        out_specs=out_specs + extra_out,
        scratch_shapes=[pltpu.VMEM((2, TOP_K, tm, d), F32), pltpu.SemaphoreType.DMA((2,))],
        out_shape=[jax.ShapeDtypeStruct((r, d), F32) for r in out_rows] + extra_shape,
        compiler_params=_cparams("arbitrary"),
        name="moe_combine",
    )(*slots, *slots, x, gt_cols, ys, *extra_in)


def _moe(x, ln_g, w_router, router_bias, layer, w_gate, w_up, w_down, out_rows, next_norm_g=None):
    t, d = x.shape
    n_exp = w_router.shape[1]
    xn, e, gt = _router(x, ln_g, w_router, router_bias)
    rank, cnt = _rank(e, n_exp)
    counts = cnt[:, 0]
    padded = (counts + MOE_BLK - 1) // MOE_BLK * MOE_BLK
    pad_end = jnp.cumsum(padded)
    pad_start = (pad_end - padded).astype(jnp.int32)
    cap = t * TOP_K + n_exp * MOE_BLK
    nblk = cap // MOE_BLK
    blk_first = jnp.arange(nblk, dtype=jnp.int32) * MOE_BLK
    blk_expert = jnp.minimum(jnp.sum(pad_end[None, :] <= blk_first[:, None], axis=1), n_exp - 1).astype(jnp.int32)
    n_used = (pad_end[-1] // MOE_BLK).astype(jnp.int32).reshape(1)
    eid = jnp.arange(n_exp, dtype=jnp.int32)
    later = jnp.logical_and(padded[None, :] > 0, eid[None, :] > eid[:, None])
    next_of = jnp.min(jnp.where(later, eid[None, :], n_exp), axis=1)
    next_of = jnp.where(next_of < n_exp, next_of, -1)
    next_expert = jnp.sum(jnp.where(blk_expert[:, None] == eid[None, :], next_of[None, :], 0), axis=1).astype(jnp.int32)
    sched = (blk_expert, n_used, next_expert)
    trail = n_used[0] + jnp.arange(n_exp, dtype=jnp.int32)
    fill_blk = jnp.concatenate([pad_end // MOE_BLK - 1, jnp.minimum(trail, nblk - 1)])
    fill_ok = jnp.concatenate([padded > 0, trail < nblk])
    fill = jnp.concatenate([jnp.maximum(fill_blk, 0), fill_ok.astype(jnp.int32)]).astype(jnp.int32)
    slot = _slots(e, rank, pad_start)
    slots = tuple(slot[k] for k in range(TOP_K))
    xs = _dispatch(slots, xn, fill, cap)
    ys = _expert_ffn(xs, sched, layer, w_gate, w_up, w_down)
    outs = _combine(slots, x, gt.T, ys, out_rows, norm_g=next_norm_g)
    return (tuple(outs), None) if next_norm_g is None else (tuple(outs[:-1]), outs[-1])


def _ab_layer(xp, xn, seqs, seg, ln_g, w_in, gate_bias, hnorm_g, qnorm_g, knorm_g, rel_bias, w_out):
    nh_a = gate_bias.shape[1]
    nh_b = rel_bias.shape[0]
    d_a, d_b = nh_a * HEAD_DIM, nh_b * HEAD_DIM
    xn = _rmsnorm(xp, ln_g) if xn is None else xn
    w = w_in.astype(BF16)
    g0 = 4 * d_a
    g1 = g0 + 4 * nh_a
    u_a = _matmul([xn], [w[:, :g0]], BF16)
    w_g = jnp.zeros((w.shape[0], 128), BF16).at[:, :4 * nh_a].set(w[:, g0:g1])
    gates = _matmul([xn], [w_g], F32)
    gain = jnp.concatenate([jnp.tile(qnorm_g, nh_b), jnp.tile(knorm_g, nh_b)]).reshape(1, -1).astype(F32)
    qk_b = _matmul([xn], [w[:, g1:g1 + 2 * d_b]], BF16, epi="headnorm", extra=(gain,), tn=1024)
    v_b = _matmul([xn], [w[:, g1 + 2 * d_b:]], BF16)
    h_f, h_b = _mlstm(u_a, gates, gate_bias, seqs, nh_a)
    y_a = _mlstm_gate(h_f, h_b, u_a, hnorm_g)
    y_b = _neighbourhood_attention(qk_b, v_b, _na_bias_table(rel_bias), seqs, seg)
    wo = w_out.astype(BF16)
    return _matmul([y_a, y_b], [wo[:d_a], wo[d_a:]], F32, epi="residual", extra=xp)


def _dilated_layer(xp, xn, seqs, ln_g, w_in, qnorm_g, knorm_g, w_out):
    d_c = w_in.shape[1] // 3
    nh = d_c // HEAD_DIM
    xn = _rmsnorm(xp, ln_g) if xn is None else xn
    w = w_in.astype(BF16)
    tm = 1024
    max_len = max(l for _, l in seqs)
    tables = _rope_tables(max_len)
    pos_tbl = [(s // tm, s // tm) for s, _ in seqs]
    pos_blk = lambda i: i - _seq_lookup(i, pos_tbl, 0)
    gain = jnp.concatenate([jnp.tile(qnorm_g, nh), jnp.tile(knorm_g, nh)]).reshape(1, -1).astype(F32)
    qk = _matmul([xn], [w[:, :2 * d_c]], F32, epi="headnorm_rope", extra=(gain,) + tables,
                 pos_blk=pos_blk, tm=tm, tn=1024)
    v = _matmul([xn], [w[:, 2 * d_c:]], F32, tm=tm)
    y = _dilated_attention(qk, v, seqs)
    return _matmul([y], [w_out.astype(BF16)], F32, epi="residual", extra=xp)


def kernel(x_prompt, x_sample, ab_ln, ab_w_in, ab_gate_bias, ab_hnorm, ab_qnorm, ab_knorm, ab_relbias, ab_w_out,
           c_ln, c_w_in, c_qnorm, c_knorm, c_w_out, ffn_ln, w_router, router_bias, w_gate, w_up, w_down):
    bp, lp, d = x_prompt.shape
    bs, ls, _ = x_sample.shape
    tp = bp * lp
    seqs = [(b * lp, lp) for b in range(bp)] + [(tp + b * ls, ls) for b in range(bs)]
    seg = math.gcd(lp, ls)
    xp = (x_prompt.reshape(tp, d), x_sample.reshape(bs * ls, d))
    depth = ffn_ln.shape[0]
    mixer_ln = lambda layer: ab_ln[layer // 2] if layer % 2 == 0 else c_ln[layer // 2]
    xn = None
    for layer in range(depth):
        j = layer // 2
        if layer % 2 == 0:
            x = _ab_layer(xp, xn, seqs, seg, ab_ln[j], ab_w_in[j], ab_gate_bias[j], ab_hnorm[j].reshape(-1),
                          ab_qnorm[j], ab_knorm[j], ab_relbias[j], ab_w_out[j])
        else:
            x = _dilated_layer(xp, xn, seqs, c_ln[j], c_w_in[j], c_qnorm[j], c_knorm[j], c_w_out[j])
        last = layer == depth - 1
        xp, xn = _moe(x, ffn_ln[layer], w_router, router_bias, layer, w_gate, w_up, w_down,
                      out_rows=(tp, bs * ls) if last else (tp + bs * ls,),
                      next_norm_g=None if last else mixer_ln(layer + 1))
    return xp[0].reshape(bp, lp, d), xp[1].reshape(bs, ls, d)
```

```python
import functools
import math

import jax
import jax.numpy as jnp
import numpy as np
from jax import lax
from jax.experimental import pallas as pl
from jax.experimental.pallas import tpu as pltpu

HEAD_DIM = 128
MLSTM_CHUNK = 128
GRID_W = 64
NA_ROWS = 8
NA_COLS = 16
DIL_PAIRS = ((128, 1), (512, 4), (2048, 16))
ROPE_THETA = 500000.0
ROPE_DIM = HEAD_DIM // 4
N_GROUPS = 4
TOP_K = 2
EPS = 1e-6
NEG = -1e30
VMEM_LIMIT_BYTES = 56 * 1024 * 1024

F32 = jnp.float32
BF16 = jnp.bfloat16


def _cparams(*sem):
    return pltpu.CompilerParams(dimension_semantics=sem, vmem_limit_bytes=VMEM_LIMIT_BYTES)


def _dot(a, b):
    return jnp.dot(a, b, preferred_element_type=F32)


def _dot_nt(a, b):
    return lax.dot_general(a, b, (((1,), (1,)), ((), ())), preferred_element_type=F32)


def _dot_tn(a, b):
    return lax.dot_general(a, b, (((0,), (0,)), ((), ())), preferred_element_type=F32)


def _split3(x):
    hi = x.astype(BF16)
    r1 = x - hi.astype(F32)
    mid = r1.astype(BF16)
    lo = (r1 - mid.astype(F32)).astype(BF16)
    return hi, mid, lo


def _seq_lookup(u, table, default):
    out = default
    for thr, val in table:
        out = jnp.where(u >= thr, val, out)
    return out


def _rmsnorm_kernel(*refs, starts):
    g_ref, o_ref = refs[-2:]
    x = _pick_part(refs[:-2], starts, pl.program_id(0))
    ms = jnp.mean(x * x, axis=-1, keepdims=True)
    o_ref[...] = (x * lax.rsqrt(ms + EPS) * g_ref[...]).astype(o_ref.dtype)


def _rmsnorm(parts, g, tm=512):
    d = parts[0].shape[1]
    t = sum(p.shape[0] for p in parts)
    specs, starts = _part_specs(parts, tm, d, lambda: 0)
    return pl.pallas_call(
        functools.partial(_rmsnorm_kernel, starts=starts),
        grid=(t // tm,),
        in_specs=specs + [pl.BlockSpec((1, d), lambda i: (0, 0))],
        out_specs=pl.BlockSpec((tm, d), lambda i: (i, 0)),
        out_shape=jax.ShapeDtypeStruct((t, d), BF16),
        compiler_params=_cparams("parallel"),
        name="rmsnorm",
    )(*parts, g.reshape(1, d).astype(F32))


EPI_COLS = 256


def _head_mats(n):
    k = lax.broadcasted_iota(jnp.int32, (n, n), 0)
    i = lax.broadcasted_iota(jnp.int32, (n, n), 1)
    shift = HEAD_DIM.bit_length() - 1
    same = jnp.right_shift(k, shift) == jnp.right_shift(i, shift)
    kl, il = jnp.bitwise_and(k, HEAD_DIM - 1), jnp.bitwise_and(i, HEAD_DIM - 1)
    half = ROPE_DIM // 2
    partner = jnp.where(il < half, il + half, jnp.where(il < 2 * half, il - half, -1))
    ones = jnp.where(same, 1.0, 0.0).astype(BF16)
    swap = jnp.where(jnp.logical_and(same, kl == partner), 1.0, 0.0).astype(BF16)
    return ones, swap


def _head_epilogue(acc, gain, rope, mats):
    ones, swap = mats
    y = acc * lax.rsqrt(_dot((acc * acc).astype(BF16), ones) * (1.0 / HEAD_DIM) + EPS) * gain
    if rope is not None:
        reps = acc.shape[1] // HEAD_DIM
        c, s = (jnp.concatenate([t] * reps, axis=1) for t in rope)
        y = y * c + _dot(y.astype(BF16), swap) * s
    return y


def _pick_part(refs, starts, i):
    val = refs[0][...]
    for ref, st in zip(refs[1:], starts[1:]):
        val = jnp.where(i >= st, ref[...], val)
    return val


def _mm_kernel(*refs, n_in, epi, res_starts):
    xs, ws, rest = refs[:n_in], refs[n_in:2 * n_in], refs[2 * n_in:]
    o_ref = rest[-1]
    if epi in ("headnorm", "headnorm_rope"):
        rope = (rest[1][...], rest[2][...]) if epi == "headnorm_rope" else None
        mats = _head_mats(EPI_COLS)
        x = xs[0][...]
        for c0 in range(0, o_ref.shape[1], 2 * EPI_COLS):
            acc = _dot(x, ws[0][:, c0:c0 + 2 * EPI_COLS])
            for h0 in range(0, acc.shape[1], EPI_COLS):
                cs = slice(c0 + h0, c0 + h0 + EPI_COLS)
                o_ref[:, cs] = _head_epilogue(acc[:, h0:h0 + EPI_COLS], rest[0][:, cs], rope, mats).astype(o_ref.dtype)
        return
    acc = _dot(xs[0][...], ws[0][...])
    for x_ref, w_ref in zip(xs[1:], ws[1:]):
        acc = acc + _dot(x_ref[...], w_ref[...])
    if epi == "plain":
        o_ref[...] = acc.astype(o_ref.dtype)
    elif epi == "residual":
        i = pl.program_id(0)
        ends = res_starts[1:] + [None]
        for ref, st, en in zip(rest[:-1], res_starts, ends):
            @pl.when(i >= st if en is None else jnp.logical_and(i >= st, i < en))
            def _(ref=ref):
                o_ref[...] = (acc + ref[...]).astype(o_ref.dtype)
    else:
        raise ValueError(epi)


def _part_specs(parts, tm, block_cols, col_of):
    specs, starts, st = [], [], 0
    for p in parts:
        nb = p.shape[0] // tm
        specs.append(pl.BlockSpec((tm, block_cols), lambda i, *j, st=st, nb=nb: (jnp.clip(i - st, 0, nb - 1), col_of(*j))))
        starts.append(st)
        st += nb
    return specs, starts


def _matmul(xs, ws, out_dtype, epi="plain", extra=(), pos_blk=None, tm=1024, tn=1024):
    t = xs[0].shape[0]
    n = ws[0].shape[1]
    tm, tn = min(tm, t), min(tn, n)
    in_specs = [pl.BlockSpec((tm, x.shape[1]), lambda i, j: (i, 0)) for x in xs]
    in_specs += [pl.BlockSpec((w.shape[0], tn), lambda i, j: (0, j)) for w in ws]
    res_starts = None
    if epi == "residual":
        specs, res_starts = _part_specs(extra, tm, tn, lambda j: j)
        in_specs += specs
    elif epi in ("headnorm", "headnorm_rope"):
        in_specs.append(pl.BlockSpec((1, tn), lambda i, j: (0, j)))
        if epi == "headnorm_rope":
            in_specs += [pl.BlockSpec((tm, HEAD_DIM), lambda i, j: (pos_blk(i), 0))] * 2
    return pl.pallas_call(
        functools.partial(_mm_kernel, n_in=len(xs), epi=epi, res_starts=res_starts),
        grid=(t // tm, n // tn),
        in_specs=in_specs,
        out_specs=pl.BlockSpec((tm, tn), lambda i, j: (i, j)),
        out_shape=jax.ShapeDtypeStruct((t, n), out_dtype),
        compiler_params=_cparams("parallel", "parallel"),
        name="matmul_" + epi,
    )(*xs, *ws, *extra)


def _mlstm_kernel(qf_ref, kf_ref, vf_ref, gf_ref, qb_ref, kb_ref, vb_ref, gb_ref, bias_ref,
                  hf_ref, hb_ref, s_ref, m_ref, *, nh, nchunks, start_chunks, last_chunks):
    c = pl.program_id(0)
    cb = nchunks - 1 - c
    ch = MLSTM_CHUNK
    is_start = functools.reduce(jnp.logical_or, [c == s for s in start_chunks])
    is_last = functools.reduce(jnp.logical_or, [cb == e for e in last_chunks])

    def _reset(d):
        s_ref[d] = jnp.zeros((nh, HEAD_DIM, 2 * HEAD_DIM), F32)
        m_ref[d] = jnp.zeros((nh, 1, HEAD_DIM), F32)

    pl.when(is_start)(lambda: _reset(0))
    pl.when(is_last)(lambda: _reset(1))

    row = lax.broadcasted_iota(jnp.int32, (ch, ch), 0)
    col = lax.broadcasted_iota(jnp.int32, (ch, ch), 1)
    scale = HEAD_DIM ** -0.5
    heads = lambda ref: jnp.stack([ref[:, h * HEAD_DIM:(h + 1) * HEAD_DIM] for h in range(nh)])
    bdot = lambda a, b, ca, cb: lax.dot_general(a, b, (((ca,), (cb,)), ((0,), (0,))), preferred_element_type=F32)

    for d, (q_ref, k_ref, v_ref, g_ref, h_ref) in enumerate(
            ((qf_ref, kf_ref, vf_ref, gf_ref, hf_ref), (qb_ref, kb_ref, vb_ref, gb_ref, hb_ref))):
        mask = (row >= col) if d == 0 else (col >= row)
        icol0 = 2 * nh * d
        fcol0 = icol0 + nh
        tot_row = ch - 1 if d == 0 else 0
        g = g_ref[...] + bias_ref[...]
        lf = jnp.minimum(g, 0.0) - jnp.log1p(jnp.exp(-jnp.abs(g)))
        tri = jnp.where(mask, 1.0, 0.0).astype(BF16)
        hi, mid, lo = _split3(lf)
        bcum = _dot(tri, hi) + _dot(tri, mid) + _dot(tri, lo)
        a = g - pltpu.roll(bcum, HEAD_DIM - nh, 1)
        a_t = a.T
        bcol = jnp.stack([bcum[:, fcol0 + h:fcol0 + h + 1] for h in range(nh)])
        acol = jnp.stack([a[:, icol0 + h:icol0 + h + 1] for h in range(nh)])
        arow = jnp.stack([a_t[icol0 + h:icol0 + h + 1, :] for h in range(nh)])
        b_tot = bcol[:, tot_row:tot_row + 1, :]
        m_st = m_ref[d][:, :, 0:1]
        s_prev = s_ref[d]

        log_d = jnp.where(mask[None], bcol + arow, NEG)
        m_intra = jnp.max(log_d, axis=2, keepdims=True)
        log_inter = bcol + m_st
        m_row = jnp.maximum(log_inter, m_intra)
        dmat = jnp.exp(log_d - m_row)
        q = heads(q_ref)
        ks = (heads(k_ref).astype(F32) * scale).astype(BF16)
        v1 = jnp.concatenate([heads(v_ref), jnp.ones((nh, ch, HEAD_DIM), BF16)], axis=2)
        p = bdot(q, ks, 2, 2) * dmat
        w_inter = jnp.exp(log_inter - m_row)
        nd = bdot(p.astype(BF16), v1, 2, 1) + w_inter * bdot(q, s_prev.astype(BF16), 2, 1)
        h_out = nd[:, :, :HEAD_DIM] / jnp.maximum(jnp.abs(nd[:, :, HEAD_DIM:]), jnp.exp(-m_row))
        for h in range(nh):
            h_ref[:, h * HEAD_DIM:(h + 1) * HEAD_DIM] = h_out[h]

        m_kv = b_tot + jnp.max(arow, axis=2, keepdims=True)
        m_new = jnp.maximum(b_tot + m_st, m_kv)
        decay = jnp.exp(b_tot + m_st - m_new)
        wexp = jnp.exp(b_tot + acol - m_new)
        kw = ks.astype(F32) * wexp
        s_ref[d] = decay * s_prev + bdot(kw.astype(BF16), v1, 1, 1)
        m_ref[d] = jnp.broadcast_to(m_new, (nh, 1, HEAD_DIM))


def _mlstm(u_a, gates, gate_bias, seqs, nh):
    t = u_a.shape[0]
    ch = MLSTM_CHUNK
    nchunks = t // ch
    da = nh * HEAD_DIM
    start_chunks = [s // ch for s, _ in seqs]
    last_chunks = [(s + l) // ch - 1 for s, l in seqs]
    fwd = lambda j: pl.BlockSpec((ch, da), lambda c, j=j: (c, j))
    bwd = lambda j: pl.BlockSpec((ch, da), lambda c, j=j: (nchunks - 1 - c, j))
    gspec_f = pl.BlockSpec((ch, 128), lambda c: (c, 0))
    gspec_b = pl.BlockSpec((ch, 128), lambda c: (nchunks - 1 - c, 0))
    bias = jnp.zeros((1, 128), F32).at[0, :4 * nh].set(gate_bias.reshape(-1).astype(F32))
    return pl.pallas_call(
        functools.partial(_mlstm_kernel, nh=nh, nchunks=nchunks, start_chunks=start_chunks,
                          last_chunks=last_chunks),
        grid=(nchunks,),
        in_specs=[fwd(0), fwd(1), fwd(2), gspec_f, bwd(0), bwd(1), bwd(2), gspec_b,
                  pl.BlockSpec((1, 128), lambda c: (0, 0))],
        out_specs=[pl.BlockSpec((ch, da), lambda c: (c, 0)),
                   pl.BlockSpec((ch, da), lambda c: (nchunks - 1 - c, 0))],
        out_shape=[jax.ShapeDtypeStruct((t, da), F32)] * 2,
        scratch_shapes=[pltpu.VMEM((2, nh, HEAD_DIM, 2 * HEAD_DIM), F32),
                        pltpu.VMEM((2, nh, 1, HEAD_DIM), F32)],
        compiler_params=_cparams("arbitrary"),
        name="mlstm",
    )(u_a, u_a, u_a, gates, u_a, u_a, u_a, gates, bias)


def _mlstm_gate_kernel(hf_ref, hb_ref, o_ref, g_ref, y_ref):
    hsum = hf_ref[...] + hb_ref[...]
    o = o_ref[...].astype(F32)
    g = g_ref[...]
    outs = []
    for h in range(hsum.shape[1] // HEAD_DIM):
        hs = slice(h * HEAD_DIM, (h + 1) * HEAD_DIM)
        a = hsum[:, hs]
        y = a * lax.rsqrt(jnp.mean(a * a, axis=-1, keepdims=True) + EPS) * g[:, hs]
        outs.append(jax.nn.sigmoid(o[:, hs]) * y)
    y_ref[...] = jnp.concatenate(outs, axis=1).astype(y_ref.dtype)


def _mlstm_gate(h_f, h_b, u_a, hnorm_g, tm=512):
    t, da = h_f.shape
    return pl.pallas_call(
        _mlstm_gate_kernel,
        grid=(t // tm,),
        in_specs=[pl.BlockSpec((tm, da), lambda i: (i, 0)), pl.BlockSpec((tm, da), lambda i: (i, 0)),
                  pl.BlockSpec((tm, da), lambda i: (i, 3)), pl.BlockSpec((1, da), lambda i: (0, 0))],
        out_specs=pl.BlockSpec((tm, da), lambda i: (i, 0)),
        out_shape=jax.ShapeDtypeStruct((t, da), BF16),
        compiler_params=_cparams("parallel"),
        name="mlstm_gate",
    )(h_f, h_b, u_a, hnorm_g.reshape(1, da).astype(F32))


NA_QROWS = 4
NA_KROWS = NA_QROWS + NA_ROWS
NA_UNROLL = 8


def _na_bias_table(rel_bias):
    h = rel_bias.shape[0]
    nr, nc = 2 * NA_ROWS - 1, 2 * NA_COLS - 1
    c = np.arange(GRID_W)
    dc = np.clip(c[None, :] - c[:, None], -(NA_COLS - 1), NA_COLS - 1) + NA_COLS - 1
    c_start = np.clip(c - NA_COLS // 2, 0, GRID_W - NA_COLS)
    col_ok = (c[None, :] >= c_start[:, None]) & (c[None, :] < c_start[:, None] + NA_COLS)
    var = np.arange(3)[:, None, None]
    qi = np.arange(NA_QROWS)[None, :, None]
    kr = np.arange(NA_KROWS)[None, None, :]
    dr = kr - var * NA_QROWS - qi
    first = np.where(var == 0, 0, np.where(var == 1, qi, NA_QROWS))
    row_ok = (kr >= first) & (kr < first + NA_ROWS)
    dr_idx = np.clip(dr + NA_ROWS - 1, 0, nr - 1).reshape(-1)
    onehot = (dc.reshape(1, -1) == np.arange(nc)[:, None]).astype(np.float32)
    cols = jnp.dot(rel_bias.astype(F32).reshape(h * nr, nc), onehot, precision=lax.Precision.HIGHEST)
    cols = cols.reshape(h, nr, GRID_W, GRID_W)
    tbl = jnp.stack([cols[:, int(i)] for i in dr_idx], axis=1)
    tbl = tbl.reshape(h, 3, NA_QROWS, NA_KROWS, GRID_W, GRID_W)
    ok = row_ok[:, :, :, None, None] & col_ok[None, None, None, :, :]
    tbl = jnp.where(ok[None], tbl, NEG)
    return jnp.transpose(tbl, (0, 1, 2, 4, 3, 5)).reshape(h, 3, NA_QROWS * GRID_W, NA_KROWS * GRID_W)


def _na_kernel(q_ref, kp_ref, km_ref, kn_ref, vp_ref, vm_ref, vn_ref, bias_ref, o_ref, kcat, vcat,
               *, seg_rows, row_lo, row_hi):
    s = pl.program_id(1)
    halo = NA_ROWS * GRID_W
    seg = seg_rows * GRID_W
    kcat[0:halo] = kp_ref[...]
    kcat[halo:halo + seg] = km_ref[...]
    kcat[halo + seg:halo + seg + halo] = kn_ref[...]
    vcat[0:halo, 0:HEAD_DIM] = vp_ref[...]
    vcat[halo:halo + seg, 0:HEAD_DIM] = vm_ref[...]
    vcat[halo + seg:halo + seg + halo, 0:HEAD_DIM] = vn_ref[...]
    vcat[:, HEAD_DIM:] = jnp.ones((seg + 2 * halo, HEAD_DIM), BF16)
    r_lo = _seq_lookup(s, row_lo, 0)
    r_hi = _seq_lookup(s, row_hi, 0)
    scale = HEAD_DIM ** -0.5
    nq = NA_QROWS * GRID_W
    nk = NA_KROWS * GRID_W

    def one_group(gl):
        r0 = s * seg_rows + gl * NA_QROWS
        u = jnp.clip(r0 - NA_ROWS // 2, r_lo, r_hi - NA_KROWS)
        off = pl.multiple_of((u - s * seg_rows + NA_ROWS) * GRID_W, GRID_W)
        qoff = pl.multiple_of(gl * nq, nq)
        q = q_ref[pl.ds(qoff, nq), :]
        sc = _dot_nt(q, kcat[pl.ds(off, nk), :]) * scale + bias_ref[(r0 - u) // NA_QROWS]
        e = jnp.exp(sc - jnp.max(sc, axis=1, keepdims=True))
        nd = _dot(e.astype(BF16), vcat[pl.ds(off, nk), :])
        o_ref[pl.ds(qoff, nq), :] = (nd[:, :HEAD_DIM] / nd[:, HEAD_DIM:]).astype(o_ref.dtype)

    def body(it, carry):
        for k in range(NA_UNROLL):
            one_group(it * NA_UNROLL + k)
        return carry

    lax.fori_loop(0, seg_rows // (NA_QROWS * NA_UNROLL), body, 0)


def _neighbourhood_attention(qk, v, bias_tbl, seqs, seg):
    t, dh = v.shape
    nh = dh // HEAD_DIM
    nseg = t // seg
    seg_rows = seg // GRID_W
    assert all(l // GRID_W >= NA_KROWS and (s // GRID_W) % NA_QROWS == 0 and (l // GRID_W) % NA_QROWS == 0
               for s, l in seqs)
    halo = NA_ROWS * GRID_W
    hpb = seg // halo
    nhb = t // halo
    row_lo = [(s // seg, s // GRID_W) for s, _ in seqs]
    row_hi = [(s // seg, (s + l) // GRID_W) for s, l in seqs]
    main = lambda c0: pl.BlockSpec((seg, HEAD_DIM), lambda h, s: (s, c0 + h))
    prev = lambda c0: pl.BlockSpec((halo, HEAD_DIM), lambda h, s: (jnp.maximum(s * hpb - 1, 0), c0 + h))
    nxt = lambda c0: pl.BlockSpec((halo, HEAD_DIM), lambda h, s: (jnp.minimum((s + 1) * hpb, nhb - 1), c0 + h))
    return pl.pallas_call(
        functools.partial(_na_kernel, seg_rows=seg_rows, row_lo=row_lo, row_hi=row_hi),
        grid=(nh, nseg),
        in_specs=[main(0), prev(nh), main(nh), nxt(nh), prev(0), main(0), nxt(0),
                  pl.BlockSpec((None, 3, NA_QROWS * GRID_W, NA_KROWS * GRID_W), lambda h, s: (h, 0, 0, 0))],
        out_specs=main(0),
        out_shape=jax.ShapeDtypeStruct((t, dh), BF16),
        scratch_shapes=[pltpu.VMEM((seg + 2 * halo, HEAD_DIM), BF16), pltpu.VMEM((seg + 2 * halo, 2 * HEAD_DIM), BF16)],
        compiler_params=_cparams("parallel", "parallel"),
        name="natten",
    )(qk, qk, qk, qk, v, v, v, bias_tbl)


DIL_BLK = 128
DIL_RADIUS = 64
DIL_MAX = max(d for _, d in DIL_PAIRS)
DIL_QB = DIL_BLK * DIL_MAX
DIL_HALO = DIL_RADIUS * DIL_MAX
DIL_NAT_MAX = 1
DIL_NAT_HALO = DIL_RADIUS * max(d for _, d in DIL_PAIRS if d <= DIL_NAT_MAX)


def _dilated_natural(b, d, q_ref, kcat, vcat, has_prev, has_next, num_s, mx_s, den_s):
    halo = DIL_RADIUS * d
    nk = DIL_BLK + 2 * halo
    i_io = lax.broadcasted_iota(jnp.int32, (DIL_BLK, nk), 0)
    j_io = lax.broadcasted_iota(jnp.int32, (DIL_BLK, nk), 1)
    rel = j_io - halo - i_io
    ok = jnp.logical_and(jnp.abs(rel) <= halo, jnp.bitwise_and(rel, d - 1) == 0)
    base = jnp.where(ok, 0.0, NEG)
    scale = HEAD_DIM ** -0.5
    for t in range(DIL_QB // DIL_BLK):
        k0 = DIL_NAT_HALO + DIL_BLK * t - halo
        n_prev = max(DIL_NAT_HALO - k0, 0)
        n_next = max(k0 + nk - (DIL_NAT_HALO + DIL_QB), 0)
        madd = base
        if n_prev:
            madd = jnp.where(jnp.logical_or(j_io >= n_prev, has_prev), madd, NEG)
        if n_next:
            madd = jnp.where(jnp.logical_or(j_io < nk - n_next, has_next), madd, NEG)
        qrows = pl.ds(DIL_BLK * t, DIL_BLK)
        sc = _dot_nt(q_ref[qrows, :].astype(BF16), kcat[k0:k0 + nk, :]) * scale + madd
        mx = jnp.max(sc, axis=1, keepdims=True)
        nd = _dot(jnp.exp(sc - mx).astype(BF16), vcat[k0:k0 + nk, :])
        num_s[b, qrows, :] = nd[:, :HEAD_DIM]
        mx_s[b, qrows, :] = jnp.broadcast_to(mx, (DIL_BLK, HEAD_DIM))
        den_s[b, qrows, :] = nd[:, HEAD_DIM:]


def _dilated_classes(b, d, q_ref, k_refs, v_refs, has_prev, has_next, num_s, mx_s, den_s):
    per_class = DIL_QB // d
    halo = DIL_RADIUS * d
    to_classes = lambda x: pltpu.einshape("mgd->gmd", x.reshape(x.shape[0] // d, d, HEAD_DIM))
    to_natural = lambda x: pltpu.einshape("gmd->mgd", x).reshape(DIL_QB, HEAD_DIM)
    bdot = lambda a, c, ca, cc: lax.dot_general(a, c, (((ca,), (cc,)), ((0,), (0,))), preferred_element_type=F32)
    nk = DIL_BLK + 2 * DIL_RADIUS

    def with_halo(refs):
        prv, cur, nxt = refs
        parts = [to_classes(prv[DIL_HALO - halo:DIL_HALO, :]), to_classes(cur[...]), to_classes(nxt[0:halo, :])]
        return jnp.concatenate(parts, axis=1).astype(BF16)

    qd = to_classes(q_ref[...]).astype(BF16)
    kd = with_halo(k_refs)
    vd = with_halo(v_refs)
    vd = jnp.concatenate([vd, jnp.ones(vd.shape, BF16)], axis=2)
    i_io = lax.broadcasted_iota(jnp.int32, (DIL_BLK, nk), 0)
    j_io = lax.broadcasted_iota(jnp.int32, (DIL_BLK, nk), 1)
    band = jnp.abs(j_io - DIL_RADIUS - i_io) <= DIL_RADIUS
    nj = per_class // DIL_BLK
    nums, mxs, dens = [], [], []
    for j in range(nj):
        ok = band
        if j == 0:
            ok = jnp.logical_and(ok, jnp.logical_or(j_io >= DIL_RADIUS, has_prev))
        if j == nj - 1:
            ok = jnp.logical_and(ok, jnp.logical_or(j_io < DIL_BLK + DIL_RADIUS, has_next))
        qs = slice(DIL_BLK * j, DIL_BLK * (j + 1))
        ks = slice(DIL_BLK * j, DIL_BLK * j + nk)
        sc = bdot(qd[:, qs], kd[:, ks], 2, 2) * (HEAD_DIM ** -0.5) + jnp.where(ok, 0.0, NEG)[None]
        mx = jnp.max(sc, axis=2, keepdims=True)
        nd = bdot(jnp.exp(sc - mx).astype(BF16), vd[:, ks], 2, 1)
        nums.append(nd[:, :, :HEAD_DIM])
        mxs.append(jnp.broadcast_to(mx, (d, DIL_BLK, HEAD_DIM)))
        dens.append(nd[:, :, HEAD_DIM:])
    cat = lambda xs: xs[0] if len(xs) == 1 else jnp.concatenate(xs, axis=1)
    num_s[b] = to_natural(cat(nums))
    mx_s[b] = to_natural(cat(mxs))
    den_s[b] = to_natural(cat(dens))


def _dilated_kernel(q_ref, kp_ref, kc_ref, kn_ref, vp_ref, vc_ref, vn_ref, y_ref, num_s, mx_s, den_s, kcat, vcat,
                    *, first_blocks, last_blocks):
    i = pl.program_id(1)
    has_prev = jnp.logical_not(functools.reduce(jnp.logical_or, [i == b for b in first_blocks]))
    has_next = jnp.logical_not(functools.reduce(jnp.logical_or, [i == b for b in last_blocks]))
    h = DIL_NAT_HALO
    kcat[0:h] = kp_ref[DIL_HALO - h:DIL_HALO, :].astype(BF16)
    kcat[h:h + DIL_QB] = kc_ref[...].astype(BF16)
    kcat[h + DIL_QB:h + DIL_QB + h] = kn_ref[0:h, :].astype(BF16)
    vcat[0:h, 0:HEAD_DIM] = vp_ref[DIL_HALO - h:DIL_HALO, :].astype(BF16)
    vcat[h:h + DIL_QB, 0:HEAD_DIM] = vc_ref[...].astype(BF16)
    vcat[h + DIL_QB:h + DIL_QB + h, 0:HEAD_DIM] = vn_ref[0:h, :].astype(BF16)
    vcat[:, HEAD_DIM:] = jnp.ones((DIL_QB + 2 * h, HEAD_DIM), BF16)
    nk = DIL_BLK + 2 * DIL_RADIUS
    j_io = lax.broadcasted_iota(jnp.int32, (DIL_BLK, nk), 0)
    s_io = lax.broadcasted_iota(jnp.int32, (DIL_BLK, nk), 1)
    band = jnp.abs(s_io - DIL_RADIUS - j_io) <= DIL_RADIUS
    ok_prev = jnp.logical_or(s_io >= DIL_RADIUS, has_prev)
    ok_next = jnp.logical_or(s_io < DIL_BLK + DIL_RADIUS, has_next)
    madd = {}
    for up in (False, True):
        for un in (False, True):
            ok = band
            ok = jnp.logical_and(ok, ok_prev) if up else ok
            ok = jnp.logical_and(ok, ok_next) if un else ok
            madd[up, un] = jnp.where(ok, 0.0, NEG)
    scale = HEAD_DIM ** -0.5
    ones_v = jnp.ones((nk, HEAD_DIM), BF16)

    def rows(start, size, d):
        return pl.ds(start, size, stride=d) if d > 1 else pl.ds(start, size)

    for b, (_, d) in enumerate(DIL_PAIRS):
        if d <= DIL_NAT_MAX:
            _dilated_natural(b, d, q_ref, kcat, vcat, has_prev, has_next, num_s, mx_s, den_s)
            continue
        if d % 8 == 0:
            _dilated_classes(b, d, q_ref, (kp_ref, kc_ref, kn_ref), (vp_ref, vc_ref, vn_ref), has_prev, has_next,
                             num_s, mx_s, den_s)
            continue
        per_class = DIL_QB // d
        nj = per_class // DIL_BLK
        for g in range(d):
            for j in range(nj):
                up, un = j == 0, j == nj - 1
                qrows = rows(g + d * DIL_BLK * j, DIL_BLK, d)
                m0 = max(DIL_BLK * j - DIL_RADIUS, 0)
                m1 = min(DIL_BLK * (j + 1) + DIL_RADIUS, per_class)
                kparts, vparts = [], []
                if up:
                    r = rows(DIL_HALO - DIL_RADIUS * d + g, DIL_RADIUS, d)
                    kparts.append(kp_ref[r, :])
                    vparts.append(vp_ref[r, :])
                r = rows(g + d * m0, m1 - m0, d)
                kparts.append(kc_ref[r, :])
                vparts.append(vc_ref[r, :])
                if un:
                    r = rows(g, DIL_RADIUS, d)
                    kparts.append(kn_ref[r, :])
                    vparts.append(vn_ref[r, :])
                kk = jnp.concatenate(kparts, axis=0).astype(BF16)
                vv = jnp.concatenate([jnp.concatenate(vparts, axis=0).astype(BF16), ones_v], axis=1)
                sc = _dot_nt(q_ref[qrows, :].astype(BF16), kk) * scale + madd[up, un]
                mx = jnp.max(sc, axis=1, keepdims=True)
                nd = _dot(jnp.exp(sc - mx).astype(BF16), vv)
                num_s[b, qrows, :] = nd[:, :HEAD_DIM]
                mx_s[b, qrows, :] = jnp.broadcast_to(mx, (DIL_BLK, HEAD_DIM))
                den_s[b, qrows, :] = nd[:, HEAD_DIM:]

    nb = len(DIL_PAIRS)
    mxs = [mx_s[b] for b in range(nb)]
    m_top = functools.reduce(jnp.maximum, mxs)
    w = [jnp.exp(m - m_top) for m in mxs]
    num = functools.reduce(jnp.add, [w[b] * num_s[b] for b in range(nb)])
    den = functools.reduce(jnp.add, [w[b] * den_s[b] for b in range(nb)])
    y_ref[...] = (num / den).astype(y_ref.dtype)


def _dilated_attention(qk, v, seqs):
    t, dh = v.shape
    nh = dh // HEAD_DIM
    assert all(window // (2 * dil) == DIL_RADIUS for window, dil in DIL_PAIRS)
    assert all(s % DIL_QB == 0 and l % DIL_QB == 0 for s, l in seqs)
    nq = t // DIL_QB
    hb = DIL_QB // DIL_HALO
    nhalo = t // DIL_HALO
    first_blocks = [s // DIL_QB for s, _ in seqs]
    last_blocks = [(s + l) // DIL_QB - 1 for s, l in seqs]
    cur = lambda c0: pl.BlockSpec((DIL_QB, HEAD_DIM), lambda h, i: (i, c0 + h))
    prev = lambda c0: pl.BlockSpec((DIL_HALO, HEAD_DIM), lambda h, i: (jnp.maximum(i * hb - 1, 0), c0 + h))
    nxt = lambda c0: pl.BlockSpec((DIL_HALO, HEAD_DIM), lambda h, i: (jnp.minimum((i + 1) * hb, nhalo - 1), c0 + h))
    nb = len(DIL_PAIRS)
    return pl.pallas_call(
        functools.partial(_dilated_kernel, first_blocks=first_blocks, last_blocks=last_blocks),
        grid=(nh, nq),
        in_specs=[cur(0), prev(nh), cur(nh), nxt(nh), prev(0), cur(0), nxt(0)],
        out_specs=cur(0),
        out_shape=jax.ShapeDtypeStruct((t, dh), BF16),
        scratch_shapes=[pltpu.VMEM((nb, DIL_QB, HEAD_DIM), F32)] * 3
        + [pltpu.VMEM((DIL_QB + 2 * DIL_NAT_HALO, HEAD_DIM), BF16),
           pltpu.VMEM((DIL_QB + 2 * DIL_NAT_HALO, 2 * HEAD_DIM), BF16)],
        compiler_params=_cparams("parallel", "parallel"),
        name="dilated",
    )(qk, qk, qk, qk, v, v, v)


def _rope_tables(max_len):
    half = ROPE_DIM // 2
    inv_freq = ROPE_THETA ** (-jnp.arange(half, dtype=F32) / half)
    ang = jnp.arange(max_len, dtype=F32)[:, None] * inv_freq[None, :]
    cos, sin = jnp.cos(ang), jnp.sin(ang)
    rest = HEAD_DIM - ROPE_DIM
    c = jnp.concatenate([cos, cos, jnp.ones((max_len, rest), F32)], axis=1)
    s = jnp.concatenate([-sin, sin, jnp.zeros((max_len, rest), F32)], axis=1)
    return c, s


def _router_kernel(x_ref, g_ref, wr_ref, rb_ref, xn_ref, e_ref, gt_ref, *, n_exp):
    x = x_ref[...]
    xn = x * lax.rsqrt(jnp.mean(x * x, axis=-1, keepdims=True) + EPS) * g_ref[...]
    xn_ref[...] = xn.astype(xn_ref.dtype)
    tm = x.shape[0]
    epg = n_exp // N_GROUPS
    xh, xm, _ = _split3(xn)
    wh, wm, _ = _split3(wr_ref[...])
    logits = _dot_nt(wh, xh) + _dot_nt(wh, xm) + _dot_nt(wm, xh)
    scores = jax.nn.sigmoid(logits)
    sel = (scores + rb_ref[:, 0:1]).reshape(N_GROUPS, epg, tm)
    sc3 = scores.reshape(N_GROUPS, epg, tm)
    io = lax.broadcasted_iota(jnp.int32, (N_GROUPS, epg, tm), 1)
    gio = lax.broadcasted_iota(jnp.int32, (N_GROUPS, 1, tm), 0)
    m1 = jnp.max(sel, axis=1, keepdims=True)
    i1 = jnp.min(jnp.where(sel == m1, io, epg), axis=1, keepdims=True)
    sel_b = jnp.where(io == i1, -jnp.inf, sel)
    m2 = jnp.max(sel_b, axis=1, keepdims=True)
    i2 = jnp.min(jnp.where(sel_b == m2, io, epg), axis=1, keepdims=True)
    gs = m1 + m2
    gbest = jnp.min(jnp.where(gs == jnp.max(gs, axis=0, keepdims=True), gio, N_GROUPS), axis=0, keepdims=True)
    in_g = gio == gbest
    outs_e, outs_w = [], []
    for ik in (i1, i2):
        pick = in_g & (io == ik)
        outs_e.append(jnp.sum(jnp.where(in_g, gbest * epg + ik, 0), axis=0))
        outs_w.append(jnp.sum(jnp.sum(jnp.where(pick, sc3, 0.0), axis=1, keepdims=True), axis=0))
    wsum = outs_w[0] + outs_w[1]
    e_ref[0:1, :] = outs_e[0]
    e_ref[1:2, :] = outs_e[1]
    gt_ref[0:1, :] = outs_w[0] / wsum
    gt_ref[1:2, :] = outs_w[1] / wsum


def _router(x, g, w_router, router_bias, tm=512):
    t, d = x.shape
    n_exp = w_router.shape[1]
    rb = jnp.broadcast_to(router_bias.astype(F32)[:, None], (n_exp, 128))
    return pl.pallas_call(
        functools.partial(_router_kernel, n_exp=n_exp),
        grid=(t // tm,),
        in_specs=[pl.BlockSpec((tm, d), lambda i: (i, 0)), pl.BlockSpec((1, d), lambda i: (0, 0)),
                  pl.BlockSpec((n_exp, d), lambda i: (0, 0)), pl.BlockSpec((n_exp, 128), lambda i: (0, 0))],
        out_specs=[pl.BlockSpec((tm, d), lambda i: (i, 0)), pl.BlockSpec((2, tm), lambda i: (0, i)),
                   pl.BlockSpec((2, tm), lambda i: (0, i))],
        out_shape=[jax.ShapeDtypeStruct((t, d), F32), jax.ShapeDtypeStruct((2, t), jnp.int32),
                   jax.ShapeDtypeStruct((2, t), F32)],
        compiler_params=_cparams("parallel"),
        name="router",
    )(x, g.reshape(1, d).astype(F32), w_router.T.astype(F32), rb)


def _rank_kernel(e_ref, rank_ref, cnt_ref, run_ref, *, n_exp):
    i = pl.program_id(0)
    tm = e_ref.shape[1]

    @pl.when(i == 0)
    def _():
        run_ref[...] = jnp.zeros_like(run_ref)

    eio = lax.broadcasted_iota(jnp.int32, (n_exp, tm), 0)
    oh = [(eio == e_ref[k:k + 1, :]) for k in range(TOP_K)]
    ohf = [jnp.where(o, 1.0, 0.0) for o in oh]
    both = ohf[0] + ohf[1]
    r_i = lax.broadcasted_iota(jnp.int32, (tm, tm), 0)
    c_i = lax.broadcasted_iota(jnp.int32, (tm, tm), 1)
    upper = jnp.where(r_i <= c_i, 1.0, 0.0).astype(BF16)
    cum = _dot(both.astype(BF16), upper)
    base = run_ref[:, 0:1] + cum - both
    for k in range(TOP_K):
        rank_ref[k:k + 1, :] = jnp.sum(jnp.where(oh[k], base, 0.0), axis=0, keepdims=True).astype(jnp.int32)
    run_new = run_ref[...] + cum[:, tm - 1:tm]
    run_ref[...] = run_new
    cnt_ref[...] = run_new.astype(jnp.int32)


def _rank(e, n_exp, tm=512):
    t = e.shape[1]
    return pl.pallas_call(
        functools.partial(_rank_kernel, n_exp=n_exp),
        grid=(t // tm,),
        in_specs=[pl.BlockSpec((2, tm), lambda i: (0, i))],
        out_specs=[pl.BlockSpec((2, tm), lambda i: (0, i)), pl.BlockSpec((n_exp, 128), lambda i: (0, 0))],
        out_shape=[jax.ShapeDtypeStruct((2, t), jnp.int32), jax.ShapeDtypeStruct((n_exp, 128), jnp.int32)],
        scratch_shapes=[pltpu.VMEM((n_exp, 128), F32)],
        compiler_params=_cparams("arbitrary"),
        name="moe_rank",
    )(e)


def _slots_kernel(ps_ref, e_ref, rank_ref, slot_ref, *, n_exp):
    e = e_ref[...]
    start = jnp.zeros_like(e)
    for k in range(n_exp):
        start = jnp.where(e == k, ps_ref[k], start)
    slot_ref[...] = start + rank_ref[...]


def _slots(e, rank, pad_start, tm=2048):
    t = e.shape[1]
    tm = min(tm, t)
    spec = pl.BlockSpec((TOP_K, tm), lambda i, ps: (0, i))
    return pl.pallas_call(
        functools.partial(_slots_kernel, n_exp=pad_start.shape[0]),
        grid_spec=pltpu.PrefetchScalarGridSpec(num_scalar_prefetch=1, grid=(t // tm,), in_specs=[spec, spec],
                                               out_specs=spec),
        out_shape=jax.ShapeDtypeStruct((TOP_K, t), jnp.int32),
        compiler_params=_cparams("parallel"),
        name="moe_slots",
    )(pad_start, e, rank)


MOE_BLK = 256


CAST_ROWS = 256


ROW_UNROLL = 8


def _start_rows(n_rows, make_copy):
    def body(r, c):
        for k in range(TOP_K):
            make_copy(k, r).start(priority=k)
        return c
    lax.fori_loop(0, n_rows, body, 0, unroll=ROW_UNROLL)


def _wait_rows(n_rows, make_copy):
    def body(r, c):
        for k in range(TOP_K):
            make_copy(k, 0).wait()
        return c
    lax.fori_loop(0, n_rows, body, 0, unroll=ROW_UNROLL)


def _dispatch_kernel(fill_ref, *rest, tm, n_fill):
    slot_refs, (xn_ref, xs_out, zbuf, sem, zsem) = rest[:TOP_K], rest[TOP_K:]
    i = pl.program_id(0)

    @pl.when(i == 0)
    def _():
        zbuf[...] = jnp.zeros_like(zbuf)

        def fill(j):
            rows = pl.ds(pl.multiple_of(fill_ref[j] * MOE_BLK, MOE_BLK), MOE_BLK)
            return pltpu.make_async_copy(zbuf, xs_out.at[rows, :], zsem)

        def start(j, c):
            pl.when(fill_ref[n_fill + j] > 0)(lambda: fill(j).start())
            return c

        def wait(j, c):
            pl.when(fill_ref[n_fill + j] > 0)(lambda: fill(j).wait())
            return c

        lax.fori_loop(0, n_fill, start, 0)
        lax.fori_loop(0, n_fill, wait, 0)

    def copy(k, r):
        return pltpu.make_async_copy(xn_ref.at[pl.ds(r, 1), :], xs_out.at[pl.ds(slot_refs[k][r], 1), :], sem)

    _start_rows(tm, copy)
    _wait_rows(tm, copy)


def _dispatch(slots, xn, fill, cap, tm=256):
    t, d = xn.shape
    smem = pl.BlockSpec((tm,), lambda i, f: (i,), memory_space=pltpu.SMEM)
    return pl.pallas_call(
        functools.partial(_dispatch_kernel, tm=tm, n_fill=fill.shape[0] // 2),
        grid_spec=pltpu.PrefetchScalarGridSpec(
            num_scalar_prefetch=1, grid=(t // tm,),
            in_specs=[smem] * TOP_K + [pl.BlockSpec((tm, d), lambda i, f: (i, 0))],
            out_specs=pl.BlockSpec(memory_space=pl.ANY),
            scratch_shapes=[pltpu.VMEM((MOE_BLK, d), F32), pltpu.SemaphoreType.DMA, pltpu.SemaphoreType.DMA]),
        out_shape=jax.ShapeDtypeStruct((cap, d), F32),
        compiler_params=_cparams("arbitrary"),
        name="moe_dispatch",
    )(fill, *slots, xn)


def _dot_casting(x, w_refs, s_refs):
    k = x.shape[1]
    step = min(CAST_ROWS, k)
    accs = [None] * len(w_refs)
    for r0 in range(0, k, step):
        rows = slice(r0, r0 + step)
        for n, (w_ref, s_ref) in enumerate(zip(w_refs, s_refs)):
            wc = w_ref[rows, :].astype(BF16)
            s_ref[rows, :] = wc
            part = _dot(x[:, rows], wc)
            accs[n] = part if accs[n] is None else accs[n] + part
    return accs


def _expert_changed(b, be_ref):
    return jnp.logical_or(b == 0, be_ref[b] != be_ref[jnp.maximum(b - 1, 0)])


def _expert_mm_kernel(be_ref, nb_ref, nx_ref, x_ref, *rest, layer, n_w, finish):
    w_hbm, o_ref, (wf, ws, sem) = rest[:n_w], rest[n_w], rest[n_w + 1:]
    b = pl.program_id(0)
    live = b < nb_ref[0]
    first = _expert_changed(b, be_ref)

    def fetch(e):
        return [pltpu.make_async_copy(w.at[layer, e], wf.at[n], sem.at[n]) for n, w in enumerate(w_hbm)]

    @pl.when(b == 0)
    def _():
        for c in fetch(be_ref[0]):
            c.start()

    @pl.when(jnp.logical_and(live, first))
    def _():
        for c in fetch(be_ref[b]):
            c.wait()
        outs = _dot_casting(x_ref[...].astype(BF16), [wf.at[n] for n in range(n_w)], [ws.at[n] for n in range(n_w)])
        o_ref[...] = finish(outs).astype(o_ref.dtype)

        @pl.when(nx_ref[b] >= 0)
        def _():
            for c in fetch(nx_ref[b]):
                c.start()

    @pl.when(jnp.logical_and(live, jnp.logical_not(first)))
    def _():
        x = x_ref[...].astype(BF16)
        o_ref[...] = finish([_dot(x, ws[n]) for n in range(n_w)]).astype(o_ref.dtype)

    @pl.when(jnp.logical_not(live))
    def _():
        o_ref[...] = jnp.zeros_like(o_ref)


def _swiglu(outs):
    hg, hu = outs
    return (hg * jax.nn.sigmoid(hg)) * hu


def _expert_mm(xs, sched, layer, ws_hbm, finish, out_dtype, name):
    blk_expert, n_used, next_expert = sched
    cap, k = xs.shape
    n = ws_hbm[0].shape[3]
    n_w = len(ws_hbm)
    return pl.pallas_call(
        functools.partial(_expert_mm_kernel, layer=layer, n_w=n_w, finish=finish),
        grid_spec=pltpu.PrefetchScalarGridSpec(
            num_scalar_prefetch=3, grid=(cap // MOE_BLK,),
            in_specs=[pl.BlockSpec((MOE_BLK, k), lambda b, be, nb, nx: (jnp.minimum(b, nb[0] - 1), 0))]
            + [pl.BlockSpec(memory_space=pl.ANY)] * n_w,
            out_specs=pl.BlockSpec((MOE_BLK, n), lambda b, be, nb, nx: (b, 0)),
            scratch_shapes=[pltpu.VMEM((n_w, k, n), F32), pltpu.VMEM((n_w, k, n), BF16),
                            pltpu.SemaphoreType.DMA((n_w,))]),
        out_shape=jax.ShapeDtypeStruct((cap, n), out_dtype),
        compiler_params=_cparams("arbitrary"),
        name=name,
    )(blk_expert, n_used, next_expert, xs, *ws_hbm)


def _expert_ffn(xs, sched, layer, w_gate, w_up, w_down):
    blk_expert, n_used, next_expert = sched
    cap, d = xs.shape
    f = w_gate.shape[3]
    return pl.pallas_call(
        functools.partial(_expert_ffn_kernel, layer=layer),
        grid_spec=pltpu.PrefetchScalarGridSpec(
            num_scalar_prefetch=3, grid=(cap // MOE_BLK,),
            in_specs=[pl.BlockSpec((MOE_BLK, d), lambda b, be, nb, nx: (jnp.minimum(b, nb[0] - 1), 0))]
            + [pl.BlockSpec(memory_space=pl.ANY)] * 3,
            out_specs=pl.BlockSpec((MOE_BLK, d), lambda b, be, nb, nx: (b, 0)),
            scratch_shapes=[pltpu.VMEM((2, d, f), F32), pltpu.VMEM((2, d, f), BF16),
                            pltpu.VMEM((1, f, d), F32), pltpu.VMEM((1, f, d), BF16),
                            pltpu.SemaphoreType.DMA((3,))]),
        out_shape=jax.ShapeDtypeStruct((cap, d), F32),
        compiler_params=_cparams("arbitrary"),
        name="moe_ffn",
    )(blk_expert, n_used, next_expert, xs, w_gate, w_up, w_down)


def _expert_ffn_kernel(be_ref, nb_ref, nx_ref, x_ref, wg_hbm, wu_hbm, wd_hbm, y_ref, wf1, ws1, wf2, ws2, sem, *, layer):
    b = pl.program_id(0)
    live = b < nb_ref[0]
    first = _expert_changed(b, be_ref)

    def fetch(e):
        return [pltpu.make_async_copy(wg_hbm.at[layer, e], wf1.at[0], sem.at[0]),
                pltpu.make_async_copy(wu_hbm.at[layer, e], wf1.at[1], sem.at[1]),
                pltpu.make_async_copy(wd_hbm.at[layer, e], wf2.at[0], sem.at[2])]

    @pl.when(b == 0)
    def _():
        for c in fetch(be_ref[0]):
            c.start()

    @pl.when(jnp.logical_and(live, first))
    def _():
        for c in fetch(be_ref[b]):
            c.wait()
        h = _swiglu(_dot_casting(x_ref[...].astype(BF16), [wf1.at[0], wf1.at[1]], [ws1.at[0], ws1.at[1]]))
        y_ref[...] = _dot_casting(h.astype(BF16), [wf2.at[0]], [ws2.at[0]])[0]

        @pl.when(nx_ref[b] >= 0)
        def _():
            for c in fetch(nx_ref[b]):
                c.start()

    @pl.when(jnp.logical_and(live, jnp.logical_not(first)))
    def _():
        x = x_ref[...].astype(BF16)
        h = _swiglu([_dot(x, ws1[0]), _dot(x, ws1[1])])
        y_ref[...] = _dot(h.astype(BF16), ws2[0])

    @pl.when(jnp.logical_not(live))
    def _():
        y_ref[...] = jnp.zeros_like(y_ref)


def _combine_kernel(*refs, tm, nsteps, out_starts, out_nblk, with_norm):
    slot_cur, slot_nxt = refs[:TOP_K], refs[TOP_K:2 * TOP_K]
    x_ref, gt_ref, ys_hbm = refs[2 * TOP_K:2 * TOP_K + 3]
    rest, (buf, sem) = refs[2 * TOP_K + 3:-2], refs[-2:]
    g_ref, rest = (rest[0], rest[1:]) if with_norm else (None, rest)
    o_refs, xn_ref = (rest[:-1], rest[-1]) if with_norm else (rest, None)
    i = pl.program_id(0)
    cur = lax.rem(i, 2)

    def gather(slot_refs, b):
        def copy(k, r):
            return pltpu.make_async_copy(ys_hbm.at[pl.ds(slot_refs[k][r], 1), :], buf.at[b, k, pl.ds(r, 1), :],
                                         sem.at[b])
        return copy

    pl.when(i == 0)(lambda: _start_rows(tm, gather(slot_cur, 0)))
    pl.when(i + 1 < nsteps)(lambda: _start_rows(tm, gather(slot_nxt, 1 - cur)))
    _wait_rows(tm, gather(slot_cur, cur))
    out = x_ref[...] + (gt_ref[:, 0:1] * buf[cur, 0] + gt_ref[:, 1:2] * buf[cur, 1])
    for o_ref, st, nb in zip(o_refs, out_starts, out_nblk):
        @pl.when(jnp.logical_and(i >= st, i < st + nb))
        def _(o_ref=o_ref):
            o_ref[...] = out
    if with_norm:
        ms = jnp.mean(out * out, axis=-1, keepdims=True)
        xn_ref[...] = (out * lax.rsqrt(ms + EPS) * g_ref[...]).astype(xn_ref.dtype)


def _combine(slots, x, gt_cols, ys, out_rows, norm_g=None, tm=256):
    t, d = x.shape
    nsteps = t // tm
    with_norm = norm_g is not None
    extra_in = [norm_g.reshape(1, d).astype(F32)] if with_norm else []
    extra_spec = [pl.BlockSpec((1, d), lambda i: (0, 0))] if with_norm else []
    extra_out = [pl.BlockSpec((tm, d), lambda i: (i, 0))] if with_norm else []
    extra_shape = [jax.ShapeDtypeStruct((t, d), BF16)] if with_norm else []
    smem = lambda nxt: pl.BlockSpec((tm,), lambda i: (jnp.minimum(i + nxt, nsteps - 1),), memory_space=pltpu.SMEM)
    out_nblk = [r // tm for r in out_rows]
    out_starts = [sum(out_nblk[:n]) for n in range(len(out_nblk))]
    out_specs = [pl.BlockSpec((tm, d), lambda i, st=st, nb=nb: (jnp.clip(i - st, 0, nb - 1), 0))
                 for st, nb in zip(out_starts, out_nblk)]
    return pl.pallas_call(
        functools.partial(_combine_kernel, tm=tm, nsteps=nsteps, out_starts=out_starts, out_nblk=out_nblk,
                          with_norm=with_norm),
        grid=(nsteps,),
        in_specs=[smem(0)] * TOP_K + [smem(1)] * TOP_K
        + [pl.BlockSpec((tm, d), lambda i: (i, 0)), pl.BlockSpec((tm, TOP_K), lambda i: (i, 0)),
           pl.BlockSpec(memory_space=pl.ANY)] + extra_spec,
        out_specs=out_specs + extra_out,
        scratch_shapes=[pltpu.VMEM((2, TOP_K, tm, d), F32), pltpu.SemaphoreType.DMA((2,))],
        out_shape=[jax.ShapeDtypeStruct((r, d), F32) for r in out_rows] + extra_shape,
        compiler_params=_cparams("arbitrary"),
        name="moe_combine",
    )(*slots, *slots, x, gt_cols, ys, *extra_in)


def _moe(x, ln_g, w_router, router_bias, layer, w_gate, w_up, w_down, out_rows, next_norm_g=None):
    t, d = x.shape
    n_exp = w_router.shape[1]
    xn, e, gt = _router(x, ln_g, w_router, router_bias)
    rank, cnt = _rank(e, n_exp)
    counts = cnt[:, 0]
    padded = (counts + MOE_BLK - 1) // MOE_BLK * MOE_BLK
    pad_end = jnp.cumsum(padded)
    pad_start = (pad_end - padded).astype(jnp.int32)
    cap = t * TOP_K + n_exp * MOE_BLK
    nblk = cap // MOE_BLK
    blk_first = jnp.arange(nblk, dtype=jnp.int32) * MOE_BLK
    blk_expert = jnp.minimum(jnp.sum(pad_end[None, :] <= blk_first[:, None], axis=1), n_exp - 1).astype(jnp.int32)
    n_used = (pad_end[-1] // MOE_BLK).astype(jnp.int32).reshape(1)
    eid = jnp.arange(n_exp, dtype=jnp.int32)
    later = jnp.logical_and(padded[None, :] > 0, eid[None, :] > eid[:, None])
    next_of = jnp.min(jnp.where(later, eid[None, :], n_exp), axis=1)
    next_of = jnp.where(next_of < n_exp, next_of, -1)
    next_expert = jnp.sum(jnp.where(blk_expert[:, None] == eid[None, :], next_of[None, :], 0), axis=1).astype(jnp.int32)
    sched = (blk_expert, n_used, next_expert)
    trail = n_used[0] + jnp.arange(n_exp, dtype=jnp.int32)
    fill_blk = jnp.concatenate([pad_end // MOE_BLK - 1, jnp.minimum(trail, nblk - 1)])
    fill_ok = jnp.concatenate([padded > 0, trail < nblk])
    fill = jnp.concatenate([jnp.maximum(fill_blk, 0), fill_ok.astype(jnp.int32)]).astype(jnp.int32)
    slot = _slots(e, rank, pad_start)
    slots = tuple(slot[k] for k in range(TOP_K))
    xs = _dispatch(slots, xn, fill, cap)
    ys = _expert_ffn(xs, sched, layer, w_gate, w_up, w_down)
    outs = _combine(slots, x, gt.T, ys, out_rows, norm_g=next_norm_g)
    return (tuple(outs), None) if next_norm_g is None else (tuple(outs[:-1]), outs[-1])


def _ab_layer(xp, xn, seqs, seg, ln_g, w_in, gate_bias, hnorm_g, qnorm_g, knorm_g, rel_bias, w_out):
    nh_a = gate_bias.shape[1]
    nh_b = rel_bias.shape[0]
    d_a, d_b = nh_a * HEAD_DIM, nh_b * HEAD_DIM
    xn = _rmsnorm(xp, ln_g) if xn is None else xn
    w = w_in.astype(BF16)
    g0 = 4 * d_a
    g1 = g0 + 4 * nh_a
    u_a = _matmul([xn], [w[:, :g0]], BF16)
    w_g = jnp.zeros((w.shape[0], 128), BF16).at[:, :4 * nh_a].set(w[:, g0:g1])
    gates = _matmul([xn], [w_g], F32)
    gain = jnp.concatenate([jnp.tile(qnorm_g, nh_b), jnp.tile(knorm_g, nh_b)]).reshape(1, -1).astype(F32)
    qk_b = _matmul([xn], [w[:, g1:g1 + 2 * d_b]], BF16, epi="headnorm", extra=(gain,), tn=1024)
    v_b = _matmul([xn], [w[:, g1 + 2 * d_b:]], BF16)
    h_f, h_b = _mlstm(u_a, gates, gate_bias, seqs, nh_a)
    y_a = _mlstm_gate(h_f, h_b, u_a, hnorm_g)
    y_b = _neighbourhood_attention(qk_b, v_b, _na_bias_table(rel_bias), seqs, seg)
    wo = w_out.astype(BF16)
    return _matmul([y_a, y_b], [wo[:d_a], wo[d_a:]], F32, epi="residual", extra=xp)


def _dilated_layer(xp, xn, seqs, ln_g, w_in, qnorm_g, knorm_g, w_out):
    d_c = w_in.shape[1] // 3
    nh = d_c // HEAD_DIM
    xn = _rmsnorm(xp, ln_g) if xn is None else xn
    w = w_in.astype(BF16)
    tm = 1024
    max_len = max(l for _, l in seqs)
    tables = _rope_tables(max_len)
    pos_tbl = [(s // tm, s // tm) for s, _ in seqs]
    pos_blk = lambda i: i - _seq_lookup(i, pos_tbl, 0)
    gain = jnp.concatenate([jnp.tile(qnorm_g, nh), jnp.tile(knorm_g, nh)]).reshape(1, -1).astype(F32)
    qk = _matmul([xn], [w[:, :2 * d_c]], F32, epi="headnorm_rope", extra=(gain,) + tables,
                 pos_blk=pos_blk, tm=tm, tn=1024)
    v = _matmul([xn], [w[:, 2 * d_c:]], F32, tm=tm)
    y = _dilated_attention(qk, v, seqs)
    return _matmul([y], [w_out.astype(BF16)], F32, epi="residual", extra=xp)


def kernel(x_prompt, x_sample, ab_ln, ab_w_in, ab_gate_bias, ab_hnorm, ab_qnorm, ab_knorm, ab_relbias, ab_w_out,
           c_ln, c_w_in, c_qnorm, c_knorm, c_w_out, ffn_ln, w_router, router_bias, w_gate, w_up, w_down):
    bp, lp, d = x_prompt.shape
    bs, ls, _ = x_sample.shape
    tp = bp * lp
    seqs = [(b * lp, lp) for b in range(bp)] + [(tp + b * ls, ls) for b in range(bs)]
    seg = math.gcd(lp, ls)
    xp = (x_prompt.reshape(tp, d), x_sample.reshape(bs * ls, d))
    depth = ffn_ln.shape[0]
    mixer_ln = lambda layer: ab_ln[layer // 2] if layer % 2 == 0 else c_ln[layer // 2]
    xn = None
    for layer in range(depth):
        j = layer // 2
        if layer % 2 == 0:
            x = _ab_layer(xp, xn, seqs, seg, ab_ln[j], ab_w_in[j], ab_gate_bias[j], ab_hnorm[j].reshape(-1),
                          ab_qnorm[j], ab_knorm[j], ab_relbias[j], ab_w_out[j])
        else:
            x = _dilated_layer(xp, xn, seqs, c_ln[j], c_w_in[j], c_qnorm[j], c_knorm[j], c_w_out[j])
        last = layer == depth - 1
        xp, xn = _moe(x, ffn_ln[layer], w_router, router_bias, layer, w_gate, w_up, w_down,
                      out_rows=(tp, bs * ls) if last else (tp + bs * ls,),
                      next_norm_g=None if last else mixer_ln(layer + 1))
    return xp[0].reshape(bp, lp, d), xp[1].reshape(bs, ls, d)
```
